```python
import math, functools
import jax, jax.numpy as jnp
from jax import lax
import numpy as np

D_MODEL = 1024
BATCH = 16
SEQ = 2048
DEPTH = 1
DEC_BATCH = 4
DEC_SEQ = 4096
PAST_LEN = 128

N_HEADS = 8
N_KV_HEADS = 2
HEAD_DIM = 64
ATTN_W = N_HEADS * HEAD_DIM
KV_W = N_KV_HEADS * HEAD_DIM
WINDOW = 128
BLOCK = 128
ROT_DIM = HEAD_DIM // 4
ROPE_THETA = 500000.0
CONV_W = D_MODEL // 2
CONV_K = 3
N_MEM = 256
X_HEADS = 4
X_HEAD_DIM = D_MODEL // X_HEADS
N_EXPERTS = 32
TOP_K = 4
D_FF = D_MODEL
SWIGLU_ALPHA = 1.702
SWIGLU_LIMIT = 7.0
MOE_BLOCK = 128
EPS = 1e-5
IN_W = ATTN_W + 2 * KV_W + 3 * CONV_W + 2 * D_MODEL
SPLITS = (ATTN_W, ATTN_W + KV_W, ATTN_W + 2 * KV_W, ATTN_W + 2 * KV_W + CONV_W,
          ATTN_W + 2 * KV_W + 2 * CONV_W, ATTN_W + 2 * KV_W + 3 * CONV_W)

kernel_name = "hybrid_bidir_encoder_swa_shortconv_moe"


def rms_norm(x, g):
    xf = x.astype(jnp.float32)
    y = xf * lax.rsqrt(jnp.mean(xf * xf, axis=-1, keepdims=True) + EPS)
    return (y * g.astype(jnp.float32)).astype(x.dtype)


def partial_rope(x):
    S = x.shape[1]
    half = ROT_DIM // 2
    inv_freq = ROPE_THETA ** (-(jnp.arange(half, dtype=jnp.float32) * 2.0) / ROT_DIM)
    ang = jnp.arange(S, dtype=jnp.float32)[:, None] * inv_freq[None, :]
    cos = jnp.cos(ang)[None, :, None, :].astype(x.dtype)
    sin = jnp.sin(ang)[None, :, None, :].astype(x.dtype)
    x1 = x[..., :half]
    x2 = x[..., half:ROT_DIM]
    return jnp.concatenate([x1 * cos - x2 * sin, x2 * cos + x1 * sin, x[..., ROT_DIM:]], axis=-1)


def window_attention(q, k, v, sink):
    B, S = q.shape[0], q.shape[1]
    nb = S // BLOCK
    G = N_HEADS // N_KV_HEADS
    qb = q.reshape(B, nb, BLOCK, N_KV_HEADS, G, HEAD_DIM)

    def band(t):
        tp = jnp.pad(t, ((0, 0), (BLOCK, BLOCK), (0, 0), (0, 0)))
        tp = tp.reshape(B, nb + 2, BLOCK, N_KV_HEADS, HEAD_DIM)
        return jnp.concatenate([tp[:, :-2], tp[:, 1:-1], tp[:, 2:]], axis=2)

    kb = band(k)
    vb = band(v)
    s = jnp.einsum('bnqkgd,bnjkd->bnkgqj', qb, kb).astype(jnp.float32) * (HEAD_DIM ** -0.5)
    blk = jnp.arange(nb)[:, None, None] * BLOCK
    qpos = blk + jnp.arange(BLOCK)[None, :, None]
    kpos = blk - BLOCK + jnp.arange(3 * BLOCK)[None, None, :]
    valid = (jnp.abs(qpos - kpos) <= WINDOW) & (kpos >= 0) & (kpos < S)
    s = jnp.where(valid[None, :, None, None], s, -jnp.inf)
    sk = sink.astype(jnp.float32).reshape(N_KV_HEADS, G)[None, None, :, :, None]
    m = jnp.maximum(s.max(axis=-1), sk)
    p = jnp.exp(s - m[..., None])
    denom = p.sum(axis=-1) + jnp.exp(sk - m)
    p = (p / denom[..., None]).astype(v.dtype)
    o = jnp.einsum('bnkgqj,bnjkd->bnqkgd', p, vb)
    return o.reshape(B, S, ATTN_W)


def short_conv(cb, cc, cx, w):
    u = cc * cx
    up = jnp.pad(u, ((0, 0), (1, 1), (0, 0)))
    y = up[:, :-2] * w[0] + up[:, 1:-1] * w[1] + up[:, 2:] * w[2]
    return cb * y


def memory_cross_attention(h, mem, w_q, w_kv, w_o):
    B, S = h.shape[0], h.shape[1]
    M = mem.shape[1]
    q = (h @ w_q).reshape(B, S, X_HEADS, X_HEAD_DIM)
    kv = (mem @ w_kv).reshape(B, M, 2, X_HEADS, X_HEAD_DIM)
    k = kv[:, :, 0]
    v = kv[:, :, 1]
    s = jnp.einsum('bshd,bmhd->bhsm', q, k).astype(jnp.float32) * (X_HEAD_DIM ** -0.5)
    p = jax.nn.softmax(s, axis=-1).astype(v.dtype)
    o = jnp.einsum('bhsm,bmhd->bshd', p, v).reshape(B, S, D_MODEL)
    return o @ w_o


def moe(x, w_router, b_router, w_gate, b_gate, w_up, b_up, w_down, b_down):
    T = x.shape[0]
    TK = T * TOP_K
    logits = (x @ w_router + b_router).astype(jnp.float32)
    top_val, top_idx = lax.top_k(logits, TOP_K)
    gate_w = jax.nn.softmax(top_val, axis=-1)
    flat_e = top_idx.reshape(-1)
    flat_tok = jnp.arange(TK, dtype=jnp.int32) // TOP_K
    flat_w = gate_w.reshape(-1).astype(x.dtype)
    order = jnp.argsort(flat_e)
    sorted_e = flat_e[order]
    counts = jnp.bincount(flat_e, length=N_EXPERTS)
    start = jnp.cumsum(counts) - counts
    pad_counts = (counts + MOE_BLOCK - 1) // MOE_BLOCK * MOE_BLOCK
    pad_end = jnp.cumsum(pad_counts)
    pad_start = pad_end - pad_counts
    rank = jnp.arange(TK, dtype=jnp.int32) - start[sorted_e]
    dest = pad_start[sorted_e] + rank
    n_blocks = -(-TK // MOE_BLOCK) + N_EXPERTS
    n_rows = n_blocks * MOE_BLOCK
    row_tok = jnp.full((n_rows,), T, dtype=jnp.int32).at[dest].set(flat_tok[order])
    row_w = jnp.zeros((n_rows,), x.dtype).at[dest].set(flat_w[order])
    block_expert = jnp.minimum(
        jnp.searchsorted(pad_end, jnp.arange(n_blocks) * MOE_BLOCK, side='right'), N_EXPERTS - 1)
    xpad = jnp.concatenate([x, jnp.zeros((1, x.shape[1]), x.dtype)], axis=0)

    def expert_group(args):
        tok, e = args
        xb = xpad[tok]
        a = xb @ w_gate[e] + b_gate[e]
        u = xb @ w_up[e] + b_up[e]
        a = jnp.minimum(a, SWIGLU_LIMIT)
        u = jnp.clip(u, -SWIGLU_LIMIT, SWIGLU_LIMIT)
        hid = a * jax.nn.sigmoid(SWIGLU_ALPHA * a) * (u + 1.0)
        return hid @ w_down[e] + b_down[e]

    y = lax.map(expert_group, (row_tok.reshape(n_blocks, MOE_BLOCK), block_expert))
    y = y.reshape(n_rows, x.shape[1]) * row_w[:, None]
    return jnp.zeros((T + 1, x.shape[1]), x.dtype).at[row_tok].add(y)[:T]


def run_trunk(x, mem, g_mix, w_in, sink, conv_w, w_attn_br, w_conv_br, w_mix_out,
              g_cross, g_mem, w_xq, w_xkv, w_xo, g_moe, w_router, b_router,
              w_gate, b_gate, w_up, b_up, w_down, b_down, g_final):
    B, S, D = x.shape
    h = x
    for l in range(DEPTH):
        hn = rms_norm(h, g_mix[l])
        proj = hn @ w_in[l]
        q, k, v, cb, cc, cx, gl = jnp.split(proj, SPLITS, axis=-1)
        q = partial_rope(q.reshape(B, S, N_HEADS, HEAD_DIM))
        k = partial_rope(k.reshape(B, S, N_KV_HEADS, HEAD_DIM))
        v = v.reshape(B, S, N_KV_HEADS, HEAD_DIM)
        attn_br = window_attention(q, k, v, sink[l]) @ w_attn_br[l]
        conv_br = short_conv(cb, cc, cx, conv_w[l]) @ w_conv_br[l]
        gates = jax.nn.sigmoid(gl.astype(jnp.float32)).astype(h.dtype).reshape(B, S, 2, D)
        merged = gates[:, :, 0] * attn_br + gates[:, :, 1] * conv_br
        h = h + merged @ w_mix_out[l]
        hn = rms_norm(h, g_cross[l])
        mn = rms_norm(mem, g_mem[l])
        h = h + memory_cross_attention(hn, mn, w_xq[l], w_xkv[l], w_xo[l])
        hn = rms_norm(h, g_moe[l]).reshape(B * S, D)
        h = h + moe(hn, w_router[l], b_router[l], w_gate[l], b_gate[l],
                    w_up[l], b_up[l], w_down[l], b_down[l]).reshape(B, S, D)
    return rms_norm(h, g_final)


def setup_inputs(seed: int = 0) -> dict:
    key = jax.random.key(seed)
    ks = jax.random.split(key, 32)
    f = jnp.float32
    L, D, E, F = DEPTH, D_MODEL, N_EXPERTS, D_FF

    def nrm(k, shape, scale):
        return jax.random.normal(k, shape, f) * scale

    def gain(k, shape):
        return 1.0 + 0.02 * jax.random.normal(k, shape, f)

    return {
        "x_prompt": nrm(ks[0], (BATCH, SEQ, D), 1.0),
        "x_sample": nrm(ks[1], (DEC_BATCH, DEC_SEQ, D), 1.0),
        "mem_prompt": nrm(ks[2], (BATCH, N_MEM, D), 1.0),
        "mem_sample": nrm(ks[3], (DEC_BATCH, N_MEM, D), 1.0),
        "g_mix": gain(ks[4], (L, D)),
        "w_in": nrm(ks[5], (L, D, IN_W), D ** -0.5),
        "sink": nrm(ks[6], (L, N_HEADS), 0.5),
        "conv_w": nrm(ks[7], (L, CONV_K, CONV_W), CONV_K ** -0.5),
        "w_attn_br": nrm(ks[8], (L, ATTN_W, D), ATTN_W ** -0.5),
        "w_conv_br": nrm(ks[9], (L, CONV_W, D), CONV_W ** -0.5),
        "w_mix_out": nrm(ks[10], (L, D, D), D ** -0.5),
        "g_cross": gain(ks[11], (L, D)),
        "g_mem": gain(ks[12], (L, D)),
        "w_xq": nrm(ks[13], (L, D, D), D ** -0.5),
        "w_xkv": nrm(ks[14], (L, D, 2 * D), D ** -0.5),
        "w_xo": nrm(ks[15], (L, D, D), D ** -0.5),
        "g_moe": gain(ks[16], (L, D)),
        "w_router": nrm(ks[17], (L, D, E), D ** -0.5),
        "b_router": nrm(ks[18], (L, E), 0.01),
        "w_gate": nrm(ks[19], (L, E, D, F), D ** -0.5),
        "b_gate": nrm(ks[20], (L, E, F), 0.01),
        "w_up": nrm(ks[21], (L, E, D, F), D ** -0.5),
        "b_up": nrm(ks[22], (L, E, F), 0.01),
        "w_down": nrm(ks[23], (L, E, F, D), F ** -0.5),
        "b_down": nrm(ks[24], (L, E, D), 0.01),
        "g_final": gain(ks[25], (D,)),
    }


def reference(x_prompt, x_sample, mem_prompt, mem_sample, g_mix, w_in, sink, conv_w,
              w_attn_br, w_conv_br, w_mix_out, g_cross, g_mem, w_xq, w_xkv, w_xo, g_moe,
              w_router, b_router, w_gate, b_gate, w_up, b_up, w_down, b_down, g_final):
    params = (g_mix, w_in, sink, conv_w, w_attn_br, w_conv_br, w_mix_out,
              g_cross, g_mem, w_xq, w_xkv, w_xo, g_moe, w_router, b_router,
              w_gate, b_gate, w_up, b_up, w_down, b_down, g_final)
    y_prompt = run_trunk(x_prompt, mem_prompt, *params)
    y_sample = run_trunk(x_sample, mem_sample, *params)
    return (y_prompt, y_sample)
```

```python
import functools

import numpy as np
import jax
import jax.numpy as jnp
from jax import lax
from jax.experimental import pallas as pl
from jax.experimental.pallas import tpu as pltpu

F32 = jnp.float32
BF16 = jnp.bfloat16

D_MODEL = 1024
N_HEADS = 8
N_KV_HEADS = 2
HEAD_DIM = 64
GROUP = N_HEADS // N_KV_HEADS
ATTN_W = N_HEADS * HEAD_DIM
KV_W = N_KV_HEADS * HEAD_DIM
QKV_W = ATTN_W + 2 * KV_W
WINDOW = 128
ROT_DIM = HEAD_DIM // 4
ROPE_THETA = 500000.0
CONV_W = D_MODEL // 2
REST_W = 3 * CONV_W + 2 * D_MODEL
N_MEM = 256
X_HEADS = 4
X_HEAD_DIM = D_MODEL // X_HEADS
N_EXPERTS = 32
TOP_K = 4
D_FF = D_MODEL
SWIGLU_ALPHA = 1.702
SWIGLU_LIMIT = 7.0
EPS = 1e-5

LANES = 128
SUBLANES = 8
KEY_BLOCK = WINDOW
TM_QKV = 512
TQ_ATTN = 256
TM_MIX = 256
TM_CROSS = 256
BM_MOE = 256
TC_COMB = 256
NEG_BIG = -1e30
VMEM_LIMIT = 56 * 1024 * 1024


def _rms(x, g):
    var = jnp.mean(x * x, axis=-1, keepdims=True)
    return x * lax.rsqrt(var + EPS) * g


def _tile_meta(groups, tile):
    pos, first, last, bidx = [], [], [], []
    b0 = 0
    for nb, s in groups:
        per = s // tile
        for b in range(nb):
            for j in range(per):
                pos.append(j)
                first.append(int(j == 0))
                last.append(int(j == per - 1))
                bidx.append(b0 + b)
        b0 += nb
    return jnp.asarray(np.array([pos, first, last, bidx], dtype=np.int32))


def _const_spec(shape):
    nd = len(shape)
    return pl.BlockSpec(shape, lambda *_: (0,) * nd)


def _params(vmem=VMEM_LIMIT):
    return pltpu.CompilerParams(dimension_semantics=("arbitrary",), vmem_limit_bytes=vmem)


def _qkv_kernel(meta_ref, x_ref, g_ref, w_ref, cos_ref, sa_ref, sb_ref, q_ref, kv_ref):
    del meta_ref
    xn = _rms(x_ref[...], g_ref[...]).astype(BF16)
    proj = jnp.dot(xn, w_ref[...], preferred_element_type=F32)
    c, sa, sb = cos_ref[...], sa_ref[...], sb_ref[...]
    n_rot = (ATTN_W + KV_W) // LANES
    for gi in range(n_rot):
        p = proj[:, gi * LANES:(gi + 1) * LANES]
        r = p * c + pltpu.roll(p, LANES - ROT_DIM // 2, 1) * sa + pltpu.roll(p, ROT_DIM // 2, 1) * sb
        if gi < ATTN_W // LANES:
            q_ref[:, gi * LANES:(gi + 1) * LANES] = (r * (HEAD_DIM ** -0.5)).astype(BF16)
        else:
            kv_ref[:, 0:KV_W] = r.astype(BF16)
    kv_ref[:, KV_W:2 * KV_W] = proj[:, ATTN_W + KV_W:QKV_W].astype(BF16)


def _rope_tables(s_max):
    half = ROT_DIM // 2
    inv_freq = ROPE_THETA ** (-(jnp.arange(half, dtype=F32) * 2.0) / ROT_DIM)
    ang = jnp.arange(s_max, dtype=F32)[:, None] * inv_freq[None, :]
    cos, sin = jnp.cos(ang), jnp.sin(ang)
    d = np.arange(LANES) % HEAD_DIM
    j = d % half
    cos_l, sin_l = cos[:, j], sin[:, j]
    rot = jnp.asarray(d < ROT_DIM)[None, :]
    lo = jnp.asarray(d < half)[None, :]
    hi = jnp.asarray((d >= half) & (d < ROT_DIM))[None, :]
    c = jnp.where(rot, cos_l, 1.0)
    sa = jnp.where(lo, -sin_l, 0.0)
    sb = jnp.where(hi, sin_l, 0.0)
    return c, sa, sb


def _qkv_call(x, g_mix, w_qkv, tables, groups):
    n = x.shape[0]
    tm = TM_QKV
    meta = _tile_meta(groups, tm)
    row = lambda i, m: (i, 0)
    tab = lambda i, m: (m[0, i], 0)
    gs = pltpu.PrefetchScalarGridSpec(
        num_scalar_prefetch=1, grid=(n // tm,),
        in_specs=[pl.BlockSpec((tm, D_MODEL), row),
                  _const_spec((1, D_MODEL)),
                  _const_spec((D_MODEL, QKV_W)),
                  pl.BlockSpec((tm, LANES), tab), pl.BlockSpec((tm, LANES), tab),
                  pl.BlockSpec((tm, LANES), tab)],
        out_specs=[pl.BlockSpec((tm, ATTN_W), row), pl.BlockSpec((tm, 2 * KV_W), row)])
    return pl.pallas_call(
        _qkv_kernel, grid_spec=gs, name="qkv_rope",
        out_shape=[jax.ShapeDtypeStruct((n, ATTN_W), BF16), jax.ShapeDtypeStruct((n, 2 * KV_W), BF16)],
        compiler_params=_params())(meta, x, g_mix, w_qkv, *tables)


def _attn_kernel(meta_ref, sink_ref, q_ref, kvp_ref, kvm_ref, kvn_ref, o_ref, kcat_ref):
    i = pl.program_id(0)
    kb = KEY_BLOCK
    tq = q_ref.shape[0]
    kcat_ref[0:kb, :] = kvp_ref[...]
    kcat_ref[kb:kb + tq, :] = kvm_ref[...]
    kcat_ref[kb + tq:kb + tq + kb, :] = kvn_ref[...]
    r = lax.broadcasted_iota(jnp.int32, (kb, 3 * kb), 0)
    c = lax.broadcasted_iota(jnp.int32, (kb, 3 * kb), 1)
    dlt = c - r
    nsub = tq // kb
    for s in range(nsub):
        first = meta_ref[1, i * nsub + s]
        last = meta_ref[2, i * nsub + s]
        lo = jnp.where(first == 1, kb, 0)
        hi = jnp.where(last == 1, 2 * kb, 3 * kb)
        valid = (dlt >= 0) & (dlt <= 2 * WINDOW) & (c >= lo) & (c < hi)
        kw = kcat_ref[s * kb:(s + 3) * kb, :]
        for kh in range(N_KV_HEADS):
            kk = kw[:, kh * HEAD_DIM:(kh + 1) * HEAD_DIM]
            vv = kw[:, KV_W + kh * HEAD_DIM:KV_W + (kh + 1) * HEAD_DIM]
            for g in range(GROUP):
                h = kh * GROUP + g
                qh = q_ref[s * kb:(s + 1) * kb, h * HEAD_DIM:(h + 1) * HEAD_DIM]
                sc = lax.dot_general(qh, kk, (((1,), (1,)), ((), ())), preferred_element_type=F32)
                sc = jnp.where(valid, sc, NEG_BIG)
                snk = sink_ref[h]
                m = jnp.maximum(jnp.max(sc, axis=-1, keepdims=True), snk)
                p = jnp.exp(sc - m)
                denom = jnp.sum(p, axis=-1, keepdims=True) + jnp.exp(snk - m)
                pn = (p / denom).astype(BF16)
                oh = jnp.dot(pn, vv, preferred_element_type=F32)
                o_ref[s * kb:(s + 1) * kb, h * HEAD_DIM:(h + 1) * HEAD_DIM] = oh.astype(BF16)


def _attn_call(q, kv, sink, groups):
    n = q.shape[0]
    tq, kb = TQ_ATTN, KEY_BLOCK
    per = tq // kb
    nkb = n // kb
    meta = _tile_meta(groups, kb)
    gs = pltpu.PrefetchScalarGridSpec(
        num_scalar_prefetch=1, grid=(n // tq,),
        in_specs=[pl.BlockSpec(memory_space=pltpu.SMEM),
                  pl.BlockSpec((tq, ATTN_W), lambda i, m: (i, 0)),
                  pl.BlockSpec((kb, 2 * KV_W), lambda i, m: (jnp.maximum(i * per - 1, 0), 0)),
                  pl.BlockSpec((tq, 2 * KV_W), lambda i, m: (i, 0)),
                  pl.BlockSpec((kb, 2 * KV_W), lambda i, m: (jnp.minimum((i + 1) * per, nkb - 1), 0))],
        out_specs=pl.BlockSpec((tq, ATTN_W), lambda i, m: (i, 0)),
        scratch_shapes=[pltpu.VMEM((tq + 2 * kb, 2 * KV_W), BF16)])
    return pl.pallas_call(
        _attn_kernel, grid_spec=gs, name="window_attn",
        out_shape=jax.ShapeDtypeStruct((n, ATTN_W), BF16),
        compiler_params=_params())(meta, sink, q, kv, kv, kv)


def _mixer_kernel(meta_ref, x_ref, xp_ref, xn_ref, o_ref, g_ref, w_ref, cw_ref, wa_ref, wc_ref,
                  wm_ref, h_ref):
    i = pl.program_id(0)
    tm = x_ref.shape[0]
    g = g_ref[...]
    x = x_ref[...]
    xb = _rms(x, g).astype(BF16)
    proj = jnp.dot(xb, w_ref[...], preferred_element_type=F32)
    cb = proj[:, 0:CONV_W]
    u = proj[:, CONV_W:2 * CONV_W] * proj[:, 2 * CONV_W:3 * CONV_W]
    xh = jnp.concatenate([xp_ref[...], xn_ref[...]], axis=0)
    xhb = _rms(xh, g).astype(BF16)
    ph = jnp.dot(xhb, w_ref[:, CONV_W:3 * CONV_W], preferred_element_type=F32)
    uh = ph[:, 0:CONV_W] * ph[:, CONV_W:2 * CONV_W]
    first = meta_ref[1, i]
    last = meta_ref[2, i]
    u_prev = jnp.where(first == 1, 0.0, uh[SUBLANES - 1:SUBLANES, :])
    u_next = jnp.where(last == 1, 0.0, uh[SUBLANES:SUBLANES + 1, :])
    row = lax.broadcasted_iota(jnp.int32, (tm, 1), 0)
    up = jnp.where(row == 0, u_prev, pltpu.roll(u, 1, 0))
    dn = jnp.where(row == tm - 1, u_next, pltpu.roll(u, tm - 1, 0))
    cw = cw_ref[...]
    y = up * cw[0:1, :] + u * cw[1:2, :] + dn * cw[2:3, :]
    conv = (cb * y).astype(BF16)
    conv_br = jnp.dot(conv, wc_ref[...], preferred_element_type=F32)
    attn_br = jnp.dot(o_ref[...], wa_ref[...], preferred_element_type=F32)
    g0 = jax.nn.sigmoid(proj[:, 3 * CONV_W:3 * CONV_W + D_MODEL])
    g1 = jax.nn.sigmoid(proj[:, 3 * CONV_W + D_MODEL:REST_W])
    merged = (g0 * attn_br + g1 * conv_br).astype(BF16)
    h_ref[...] = x + jnp.dot(merged, wm_ref[...], preferred_element_type=F32)


def _mixer_call(x, o, g_mix, w_rest, conv_w, w_attn_br, w_conv_br, w_mix_out, groups):
    n = x.shape[0]
    tm = TM_MIX
    per = tm // SUBLANES
    nsb = n // SUBLANES
    meta = _tile_meta(groups, tm)
    gs = pltpu.PrefetchScalarGridSpec(
        num_scalar_prefetch=1, grid=(n // tm,),
        in_specs=[pl.BlockSpec((tm, D_MODEL), lambda i, m: (i, 0)),
                  pl.BlockSpec((SUBLANES, D_MODEL), lambda i, m: (jnp.maximum(i * per - 1, 0), 0)),
                  pl.BlockSpec((SUBLANES, D_MODEL), lambda i, m: (jnp.minimum((i + 1) * per, nsb - 1), 0)),
                  pl.BlockSpec((tm, ATTN_W), lambda i, m: (i, 0)),
                  _const_spec((1, D_MODEL)),
                  _const_spec((D_MODEL, REST_W)),
                  _const_spec((3, CONV_W)),
                  _const_spec((ATTN_W, D_MODEL)),
                  _const_spec((CONV_W, D_MODEL)),
                  _const_spec((D_MODEL, D_MODEL))],
        out_specs=pl.BlockSpec((tm, D_MODEL), lambda i, m: (i, 0)))
    return pl.pallas_call(
        _mixer_kernel, grid_spec=gs, name="mixer",
        out_shape=jax.ShapeDtypeStruct((n, D_MODEL), F32),
        compiler_params=_params())(meta, x, x, x, o, g_mix, w_rest, conv_w, w_attn_br, w_conv_br,
                                   w_mix_out)


def _memkv_kernel(mem_ref, g_ref, w_ref, kv_ref):
    mn = _rms(mem_ref[0], g_ref[...]).astype(BF16)
    kv_ref[0] = jnp.dot(mn, w_ref[...], preferred_element_type=F32).astype(BF16)


def _memkv_call(mem, g_mem, w_xkv):
    nb = mem.shape[0]
    return pl.pallas_call(
        _memkv_kernel, grid=(nb,), name="mem_kv",
        in_specs=[pl.BlockSpec((1, N_MEM, D_MODEL), lambda b: (b, 0, 0)),
                  _const_spec((1, D_MODEL)),
                  _const_spec((D_MODEL, 2 * D_MODEL))],
        out_specs=pl.BlockSpec((1, N_MEM, 2 * D_MODEL), lambda b: (b, 0, 0)),
        out_shape=jax.ShapeDtypeStruct((nb, N_MEM, 2 * D_MODEL), BF16),
        compiler_params=_params())(mem, g_mem, w_xkv)


def _cross_kernel(meta_ref, h_ref, kv_ref, gc_ref, wq_ref, wo_ref, gm_ref, wr_ref, br_ref,
                  h2_ref, xs_ref, idx_ref, gw_ref):
    del meta_ref
    tm = h_ref.shape[0]
    h = h_ref[...]
    hn = _rms(h, gc_ref[...]).astype(BF16)
    q = (jnp.dot(hn, wq_ref[...], preferred_element_type=F32) * (X_HEAD_DIM ** -0.5)).astype(BF16)
    outs = []
    for hd in range(X_HEADS):
        qh = q[:, hd * X_HEAD_DIM:(hd + 1) * X_HEAD_DIM]
        kh = kv_ref[0, :, hd * X_HEAD_DIM:(hd + 1) * X_HEAD_DIM]
        vh = kv_ref[0, :, D_MODEL + hd * X_HEAD_DIM:D_MODEL + (hd + 1) * X_HEAD_DIM]
        s = lax.dot_general(qh, kh, (((1,), (1,)), ((), ())), preferred_element_type=F32)
        m = jnp.max(s, axis=-1, keepdims=True)
        p = jnp.exp(s - m)
        p = (p / jnp.sum(p, axis=-1, keepdims=True)).astype(BF16)
        outs.append(jnp.dot(p, vh, preferred_element_type=F32).astype(BF16))
    o = jnp.concatenate(outs, axis=1)
    h2 = h + jnp.dot(o, wo_ref[...], preferred_element_type=F32)
    h2_ref[...] = h2

    hn3 = _rms(h2, gm_ref[...])
    hi = hn3.astype(BF16)
    hi32 = hi.astype(F32)
    lo = (hn3 - hi32).astype(BF16)
    bits = pltpu.bitcast(hi32, jnp.uint32)
    half = D_MODEL // 2
    xs_ref[...] = bits[:, 0:half] | lax.shift_right_logical(bits[:, half:D_MODEL], jnp.uint32(16))

    nt = (((1,), (1,)), ((), ()))
    r1 = lax.dot_general(wr_ref[...], hi, nt, preferred_element_type=F32)
    r2 = lax.dot_general(wr_ref[...], lo, nt, preferred_element_type=F32)
    e = N_EXPERTS
    logits = ((r2[e:2 * e] + r2[0:e]) + r1[e:2 * e]) + r1[0:e] + br_ref[...]
    eio = lax.broadcasted_iota(jnp.int32, (e, tm), 0)
    cur = logits
    vals, sels = [], []
    for _ in range(TOP_K):
        mx = jnp.max(cur, axis=0, keepdims=True)
        sel = jnp.min(jnp.where(cur == mx, eio, e), axis=0, keepdims=True)
        vals.append(mx)
        sels.append(sel)
        cur = jnp.where(eio == sel, -jnp.inf, cur)
    ex = [jnp.exp(v - vals[0]) for v in vals]
    tot = ex[0] + ex[1] + ex[2] + ex[3]
    idx_ref[...] = jnp.concatenate(sels, axis=0)
    gw_ref[...] = jnp.concatenate([x / tot for x in ex] + [jnp.zeros((SUBLANES - TOP_K, tm), F32)], axis=0)


def _cross_call(h1, memkv, g_cross, w_xq, w_xo, g_moe, w_r2t, b_router, groups):
    n = h1.shape[0]
    tm = TM_CROSS
    meta = _tile_meta(groups, tm)
    row = lambda i, m: (i, 0)
    col = lambda i, m: (0, i)
    gs = pltpu.PrefetchScalarGridSpec(
        num_scalar_prefetch=1, grid=(n // tm,),
        in_specs=[pl.BlockSpec((tm, D_MODEL), row),
                  pl.BlockSpec((1, N_MEM, 2 * D_MODEL), lambda i, m: (m[3, i], 0, 0)),
                  _const_spec((1, D_MODEL)),
                  _const_spec((D_MODEL, D_MODEL)),
                  _const_spec((D_MODEL, D_MODEL)),
                  _const_spec((1, D_MODEL)),
                  _const_spec((2 * N_EXPERTS, D_MODEL)),
                  _const_spec((N_EXPERTS, 1))],
        out_specs=[pl.BlockSpec((tm, D_MODEL), row),
                   pl.BlockSpec((tm, D_MODEL // 2), row),
                   pl.BlockSpec((TOP_K, tm), col),
                   pl.BlockSpec((SUBLANES, tm), col)])
    return pl.pallas_call(
        _cross_kernel, grid_spec=gs, name="cross_router",
        out_shape=[jax.ShapeDtypeStruct((n, D_MODEL), F32),
                   jax.ShapeDtypeStruct((n, D_MODEL // 2), jnp.uint32),
                   jax.ShapeDtypeStruct((TOP_K, n), jnp.int32),
                   jax.ShapeDtypeStruct((SUBLANES, n), F32)],
        compiler_params=_params())(meta, h1, memkv, g_cross, w_xq, w_xo, g_moe, w_r2t, b_router)


ROW_UNROLL = 8


def _row_gather(src_hbm, dst, sem, idx_ref, n_rows):
    def body(j, carry):
        for t in range(ROW_UNROLL):
            r = j * ROW_UNROLL + t
            pltpu.make_async_copy(src_hbm.at[pl.ds(idx_ref[0, 0, r], 1), :], dst.at[pl.ds(r, 1), :], sem).start()
        return carry
    lax.fori_loop(0, n_rows // ROW_UNROLL, body, 0)


def _ffn_kernel(nb, bexp_ref, nused_ref, tok_ref, tokn_ref, x_hbm, wg_ref, bg_ref, wu_ref, bu_ref,
                wd_ref, bd_ref, y_ref, xbuf, sems):
    del bexp_ref
    b = pl.program_id(0)
    bm = y_ref.shape[0]
    nused = nused_ref[0]
    slot = b % 2

    @pl.when(b == 0)
    def _():
        _row_gather(x_hbm, xbuf.at[0], sems.at[0], tok_ref, bm)

    @pl.when((b + 1 < nb) & (b + 1 < nused))
    def _():
        _row_gather(x_hbm, xbuf.at[1 - slot], sems.at[1 - slot], tokn_ref, bm)

    @pl.when(b < nused)
    def _():
        pltpu.make_async_copy(xbuf.at[slot], xbuf.at[slot], sems.at[slot]).wait()
        w = xbuf[slot]
        xa = pltpu.bitcast(w & jnp.uint32(0xFFFF0000), F32).astype(BF16)
        xb = pltpu.bitcast(lax.shift_left(w, jnp.uint32(16)), F32).astype(BF16)
        half = D_MODEL // 2

        def proj(w_ref, b_ref):
            return (jnp.dot(xa, w_ref[0, 0:half, :], preferred_element_type=F32)
                    + jnp.dot(xb, w_ref[0, half:D_MODEL, :], preferred_element_type=F32) + b_ref[0])

        a = jnp.minimum(proj(wg_ref, bg_ref), SWIGLU_LIMIT)
        u = jnp.clip(proj(wu_ref, bu_ref), -SWIGLU_LIMIT, SWIGLU_LIMIT)
        hid = (a * jax.nn.sigmoid(SWIGLU_ALPHA * a) * (u + 1.0)).astype(BF16)
        y_ref[...] = jnp.dot(hid, wd_ref[0], preferred_element_type=F32) + bd_ref[0]

    @pl.when(b >= nused)
    def _():
        y_ref[...] = jnp.zeros_like(y_ref)


def _ffn_call(bexp, nused, row_tok, xs, w_gate, b_gate, w_up, b_up, w_down, b_down):
    n_blocks = bexp.shape[0]
    bm = BM_MOE
    half = D_MODEL // 2
    wspec = lambda r, c: pl.BlockSpec((1, r, c), lambda b, be, nu: (be[b], 0, 0))
    gs = pltpu.PrefetchScalarGridSpec(
        num_scalar_prefetch=2, grid=(n_blocks,),
        in_specs=[pl.BlockSpec((1, 1, bm), lambda b, be, nu: (b, 0, 0), memory_space=pltpu.SMEM),
                  pl.BlockSpec((1, 1, bm), lambda b, be, nu: (jnp.minimum(b + 1, n_blocks - 1), 0, 0),
                               memory_space=pltpu.SMEM),
                  pl.BlockSpec(memory_space=pl.ANY),
                  wspec(D_MODEL, D_FF), wspec(1, D_FF), wspec(D_MODEL, D_FF), wspec(1, D_FF),
                  wspec(D_FF, D_MODEL), wspec(1, D_MODEL)],
        out_specs=pl.BlockSpec((bm, D_MODEL), lambda b, be, nu: (b, 0)),
        scratch_shapes=[pltpu.VMEM((2, bm, half), jnp.uint32), pltpu.SemaphoreType.DMA((2,))])
    return pl.pallas_call(
        functools.partial(_ffn_kernel, n_blocks), grid_spec=gs, name="expert_ffn",
        out_shape=jax.ShapeDtypeStruct((n_blocks * bm, D_MODEL), F32),
        compiler_params=_params())(bexp, nused, row_tok, row_tok, xs, w_gate, b_gate, w_up, b_up,
                                   w_down, b_down)


def _combine_kernel(nt, dst_ref, dstn_ref, y_hbm, h_ref, gw_ref, gf_ref, out_ref, ybuf, sems):
    i = pl.program_id(0)
    tc = h_ref.shape[0]
    slot = i % 2

    @pl.when(i == 0)
    def _():
        _row_gather(y_hbm, ybuf.at[0], sems.at[0], dst_ref, TOP_K * tc)

    @pl.when(i + 1 < nt)
    def _():
        _row_gather(y_hbm, ybuf.at[1 - slot], sems.at[1 - slot], dstn_ref, TOP_K * tc)

    pltpu.make_async_copy(ybuf.at[slot], ybuf.at[slot], sems.at[slot]).wait()
    gwt = jnp.transpose(gw_ref[...])
    acc = h_ref[...]
    for k in range(TOP_K):
        acc = acc + gwt[:, k:k + 1] * ybuf[slot, k * tc:(k + 1) * tc, :]
    out_ref[...] = _rms(acc, gf_ref[...])


def _combine_call(dest, y, h2, gw, g_final, tile0, n_tiles):
    tc = TC_COMB
    last = tile0 + n_tiles - 1
    return pl.pallas_call(
        functools.partial(_combine_kernel, n_tiles), grid=(n_tiles,), name="moe_combine",
        in_specs=[pl.BlockSpec((1, 1, TOP_K * tc), lambda i: (tile0 + i, 0, 0), memory_space=pltpu.SMEM),
                  pl.BlockSpec((1, 1, TOP_K * tc), lambda i: (jnp.minimum(tile0 + i + 1, last), 0, 0),
                               memory_space=pltpu.SMEM),
                  pl.BlockSpec(memory_space=pl.ANY),
                  pl.BlockSpec((tc, D_MODEL), lambda i: (tile0 + i, 0)),
                  pl.BlockSpec((SUBLANES, tc), lambda i: (0, tile0 + i)),
                  _const_spec((1, D_MODEL))],
        out_specs=pl.BlockSpec((tc, D_MODEL), lambda i: (i, 0)),
        out_shape=jax.ShapeDtypeStruct((n_tiles * tc, D_MODEL), F32),
        scratch_shapes=[pltpu.VMEM((2, TOP_K * tc, D_MODEL), F32), pltpu.SemaphoreType.DMA((2,))],
        compiler_params=_params())(dest, dest, y, h2, gw, g_final)


def _route(idx, n):
    tk = TOP_K * n
    bm = BM_MOE
    flat_e = idx.reshape(-1)
    order = jnp.argsort(flat_e, stable=True).astype(jnp.int32)
    sorted_e = flat_e[order]
    counts = jnp.bincount(flat_e, length=N_EXPERTS).astype(jnp.int32)
    start = jnp.cumsum(counts) - counts
    pad_counts = (counts + bm - 1) // bm * bm
    pad_end = jnp.cumsum(pad_counts)
    pad_start = pad_end - pad_counts
    dest_sorted = pad_start[sorted_e] + jnp.arange(tk, dtype=jnp.int32) - start[sorted_e]
    n_blocks = -(-tk // bm) + N_EXPERTS
    row_tok = jnp.zeros((n_blocks * bm,), jnp.int32).at[dest_sorted].set(order % n)
    dest = jnp.zeros((tk,), jnp.int32).at[order].set(dest_sorted)
    bexp = jnp.minimum(jnp.searchsorted(pad_end, jnp.arange(n_blocks, dtype=jnp.int32) * bm, side="right"),
                       N_EXPERTS - 1).astype(jnp.int32)
    nused = (pad_end[-1:] // bm).astype(jnp.int32)
    return row_tok.reshape(n_blocks, 1, bm), dest.reshape(TOP_K, n), bexp, nused


def _forward(xs, mems, g_mix, w_in, sink, conv_w, w_attn_br, w_conv_br, w_mix_out, g_cross, g_mem,
             w_xq, w_xkv, w_xo, g_moe, w_router, b_router, w_gate, b_gate, w_up, b_up, w_down,
             b_down, g_final):
    groups = [(x.shape[0], x.shape[1]) for x in xs]
    for _, s in groups:
        assert s % max(TM_QKV, TQ_ATTN, TM_MIX, TM_CROSS, TC_COMB) == 0
    h = jnp.concatenate([x.reshape(-1, D_MODEL) for x in xs], axis=0)
    mem = jnp.concatenate(mems, axis=0)
    n = h.shape[0]
    tables = _rope_tables(max(s for _, s in groups))
    assert w_in.shape[0] == 1, "single-layer trunk: the final norm is fused into the combine kernel"
    l = 0
    row2 = lambda v: v.reshape(1, -1)
    w_in_b = w_in[l].astype(BF16)
    q, kv = _qkv_call(h, row2(g_mix[l]), w_in_b[:, :QKV_W], tables, groups)
    o = _attn_call(q, kv, sink[l], groups)
    h = _mixer_call(h, o, row2(g_mix[l]), w_in_b[:, QKV_W:], conv_w[l], w_attn_br[l].astype(BF16),
                    w_conv_br[l].astype(BF16), w_mix_out[l].astype(BF16), groups)
    memkv = _memkv_call(mem, row2(g_mem[l]), w_xkv[l].astype(BF16))
    wr = w_router[l]
    wr_hi = wr.astype(BF16)
    wr_lo = (wr - wr_hi.astype(F32)).astype(BF16)
    w_r2t = jnp.concatenate([wr_hi.T, wr_lo.T], axis=0)
    h2, xs_packed, idx, gw = _cross_call(h, memkv, row2(g_cross[l]), w_xq[l].astype(BF16),
                                         w_xo[l].astype(BF16), row2(g_moe[l]), w_r2t,
                                         b_router[l].reshape(-1, 1), groups)
    row_tok, dest, bexp, nused = _route(idx, n)
    y = _ffn_call(bexp, nused, row_tok, xs_packed, w_gate[l].astype(BF16), b_gate[l][:, None, :],
                  w_up[l].astype(BF16), b_up[l][:, None, :], w_down[l].astype(BF16),
                  b_down[l][:, None, :])
    tc = TC_COMB
    dest_t = dest.reshape(TOP_K, n // tc, tc).transpose(1, 0, 2).reshape(n // tc, 1, TOP_K * tc)
    outs, t0 = [], 0
    for nb, s in groups:
        nt = nb * s // tc
        outs.append(_combine_call(dest_t, y, h2, gw, row2(g_final), t0, nt))
        t0 += nt
    return tuple(o.reshape(x.shape) for o, x in zip(outs, xs))


def kernel(x_prompt, x_sample, mem_prompt, mem_sample, g_mix, w_in, sink, conv_w, w_attn_br, w_conv_br,
           w_mix_out, g_cross, g_mem, w_xq, w_xkv, w_xo, g_moe, w_router, b_router, w_gate, b_gate,
           w_up, b_up, w_down, b_down, g_final):
    return _forward([x_prompt, x_sample], [mem_prompt, mem_sample], g_mix, w_in, sink, conv_w,
                    w_attn_br, w_conv_br, w_mix_out, g_cross, g_mem, w_xq, w_xkv, w_xo, g_moe,
                    w_router, b_router, w_gate, b_gate, w_up, b_up, w_down, b_down, g_final)
```

```python
import functools

import numpy as np
import jax
import jax.numpy as jnp
from jax import lax
from jax.experimental import pallas as pl
from jax.experimental.pallas import tpu as pltpu

F32 = jnp.float32
BF16 = jnp.bfloat16

D_MODEL = 1024
N_HEADS = 8
N_KV_HEADS = 2
HEAD_DIM = 64
GROUP = N_HEADS // N_KV_HEADS
ATTN_W = N_HEADS * HEAD_DIM
KV_W = N_KV_HEADS * HEAD_DIM
QKV_W = ATTN_W + 2 * KV_W
WINDOW = 128
ROT_DIM = HEAD_DIM // 4
ROPE_THETA = 500000.0
CONV_W = D_MODEL // 2
REST_W = 3 * CONV_W + 2 * D_MODEL
N_MEM = 256
X_HEADS = 4
X_HEAD_DIM = D_MODEL // X_HEADS
N_EXPERTS = 32
TOP_K = 4
D_FF = D_MODEL
SWIGLU_ALPHA = 1.702
SWIGLU_LIMIT = 7.0
EPS = 1e-5

LANES = 128
SUBLANES = 8
KEY_BLOCK = WINDOW
TM_QKV = 512
TQ_ATTN = 256
TM_MIX = 256
TM_CROSS = 256
BM_MOE = 256
TC_COMB = 256
NEG_BIG = -1e30
VMEM_LIMIT = 56 * 1024 * 1024


def _rms(x, g):
    var = jnp.mean(x * x, axis=-1, keepdims=True)
    return x * lax.rsqrt(var + EPS) * g


def _tile_meta(groups, tile):
    pos, first, last, bidx = [], [], [], []
    b0 = 0
    for nb, s in groups:
        per = s // tile
        for b in range(nb):
            for j in range(per):
                pos.append(j)
                first.append(int(j == 0))
                last.append(int(j == per - 1))
                bidx.append(b0 + b)
        b0 += nb
    return jnp.asarray(np.array([pos, first, last, bidx], dtype=np.int32))


def _const_spec(shape):
    nd = len(shape)
    return pl.BlockSpec(shape, lambda *_: (0,) * nd)


def _params(vmem=VMEM_LIMIT):
    return pltpu.CompilerParams(dimension_semantics=("arbitrary",), vmem_limit_bytes=vmem)


def _qkv_kernel(meta_ref, x_ref, g_ref, w_ref, cos_ref, sa_ref, sb_ref, q_ref, kv_ref):
    del meta_ref
    xn = _rms(x_ref[...], g_ref[...]).astype(BF16)
    proj = jnp.dot(xn, w_ref[...], preferred_element_type=F32)
    c, sa, sb = cos_ref[...], sa_ref[...], sb_ref[...]
    n_rot = (ATTN_W + KV_W) // LANES
    for gi in range(n_rot):
        p = proj[:, gi * LANES:(gi + 1) * LANES]
        r = p * c + pltpu.roll(p, LANES - ROT_DIM // 2, 1) * sa + pltpu.roll(p, ROT_DIM // 2, 1) * sb
        if gi < ATTN_W // LANES:
            q_ref[:, gi * LANES:(gi + 1) * LANES] = (r * (HEAD_DIM ** -0.5)).astype(BF16)
        else:
            kv_ref[:, 0:KV_W] = r.astype(BF16)
    kv_ref[:, KV_W:2 * KV_W] = proj[:, ATTN_W + KV_W:QKV_W].astype(BF16)


def _rope_tables(s_max):
    half = ROT_DIM // 2
    inv_freq = ROPE_THETA ** (-(jnp.arange(half, dtype=F32) * 2.0) / ROT_DIM)
    ang = jnp.arange(s_max, dtype=F32)[:, None] * inv_freq[None, :]
    cos, sin = jnp.cos(ang), jnp.sin(ang)
    d = np.arange(LANES) % HEAD_DIM
    j = d % half
    cos_l, sin_l = cos[:, j], sin[:, j]
    rot = jnp.asarray(d < ROT_DIM)[None, :]
    lo = jnp.asarray(d < half)[None, :]
    hi = jnp.asarray((d >= half) & (d < ROT_DIM))[None, :]
    c = jnp.where(rot, cos_l, 1.0)
    sa = jnp.where(lo, -sin_l, 0.0)
    sb = jnp.where(hi, sin_l, 0.0)
    return c, sa, sb


def _qkv_call(x, g_mix, w_qkv, tables, groups):
    n = x.shape[0]
    tm = TM_QKV
    meta = _tile_meta(groups, tm)
    row = lambda i, m: (i, 0)
    tab = lambda i, m: (m[0, i], 0)
    gs = pltpu.PrefetchScalarGridSpec(
        num_scalar_prefetch=1, grid=(n // tm,),
        in_specs=[pl.BlockSpec((tm, D_MODEL), row),
                  _const_spec((1, D_MODEL)),
                  _const_spec((D_MODEL, QKV_W)),
                  pl.BlockSpec((tm, LANES), tab), pl.BlockSpec((tm, LANES), tab),
                  pl.BlockSpec((tm, LANES), tab)],
        out_specs=[pl.BlockSpec((tm, ATTN_W), row), pl.BlockSpec((tm, 2 * KV_W), row)])
    return pl.pallas_call(
        _qkv_kernel, grid_spec=gs, name="qkv_rope",
        out_shape=[jax.ShapeDtypeStruct((n, ATTN_W), BF16), jax.ShapeDtypeStruct((n, 2 * KV_W), BF16)],
        compiler_params=_params())(meta, x, g_mix, w_qkv, *tables)


def _attn_kernel(meta_ref, sink_ref, q_ref, kvp_ref, kvm_ref, kvn_ref, o_ref, kcat_ref):
    i = pl.program_id(0)
    kb = KEY_BLOCK
    tq = q_ref.shape[0]
    kcat_ref[0:kb, :] = kvp_ref[...]
    kcat_ref[kb:kb + tq, :] = kvm_ref[...]
    kcat_ref[kb + tq:kb + tq + kb, :] = kvn_ref[...]
    r = lax.broadcasted_iota(jnp.int32, (kb, 3 * kb), 0)
    c = lax.broadcasted_iota(jnp.int32, (kb, 3 * kb), 1)
    dlt = c - r
    nsub = tq // kb
    for s in range(nsub):
        first = meta_ref[1, i * nsub + s]
        last = meta_ref[2, i * nsub + s]
        lo = jnp.where(first == 1, kb, 0)
        hi = jnp.where(last == 1, 2 * kb, 3 * kb)
        valid = (dlt >= 0) & (dlt <= 2 * WINDOW) & (c >= lo) & (c < hi)
        kw = kcat_ref[s * kb:(s + 3) * kb, :]
        for kh in range(N_KV_HEADS):
            kk = kw[:, kh * HEAD_DIM:(kh + 1) * HEAD_DIM]
            vv = kw[:, KV_W + kh * HEAD_DIM:KV_W + (kh + 1) * HEAD_DIM]
            for g in range(GROUP):
                h = kh * GROUP + g
                qh = q_ref[s * kb:(s + 1) * kb, h * HEAD_DIM:(h + 1) * HEAD_DIM]
                sc = lax.dot_general(qh, kk, (((1,), (1,)), ((), ())), preferred_element_type=F32)
                sc = jnp.where(valid, sc, NEG_BIG)
                snk = sink_ref[h]
                m = jnp.maximum(jnp.max(sc, axis=-1, keepdims=True), snk)
                p = jnp.exp(sc - m)
                denom = jnp.sum(p, axis=-1, keepdims=True) + jnp.exp(snk - m)
                pn = (p / denom).astype(BF16)
                oh = jnp.dot(pn, vv, preferred_element_type=F32)
                o_ref[s * kb:(s + 1) * kb, h * HEAD_DIM:(h + 1) * HEAD_DIM] = oh.astype(BF16)


def _attn_call(q, kv, sink, groups):
    n = q.shape[0]
    tq, kb = TQ_ATTN, KEY_BLOCK
    per = tq // kb
    nkb = n // kb
    meta = _tile_meta(groups, kb)
    gs = pltpu.PrefetchScalarGridSpec(
        num_scalar_prefetch=1, grid=(n // tq,),
        in_specs=[pl.BlockSpec(memory_space=pltpu.SMEM),
                  pl.BlockSpec((tq, ATTN_W), lambda i, m: (i, 0)),
                  pl.BlockSpec((kb, 2 * KV_W), lambda i, m: (jnp.maximum(i * per - 1, 0), 0)),
                  pl.BlockSpec((tq, 2 * KV_W), lambda i, m: (i, 0)),
                  pl.BlockSpec((kb, 2 * KV_W), lambda i, m: (jnp.minimum((i + 1) * per, nkb - 1), 0))],
        out_specs=pl.BlockSpec((tq, ATTN_W), lambda i, m: (i, 0)),
        scratch_shapes=[pltpu.VMEM((tq + 2 * kb, 2 * KV_W), BF16)])
    return pl.pallas_call(
        _attn_kernel, grid_spec=gs, name="window_attn",
        out_shape=jax.ShapeDtypeStruct((n, ATTN_W), BF16),
        compiler_params=_params())(meta, sink, q, kv, kv, kv)


def _mixer_kernel(meta_ref, x_ref, xp_ref, xn_ref, o_ref, g_ref, w_ref, cw_ref, wa_ref, wc_ref,
                  wm_ref, h_ref):
    i = pl.program_id(0)
    tm = x_ref.shape[0]
    g = g_ref[...]
    x = x_ref[...]
    xb = _rms(x, g).astype(BF16)
    proj = jnp.dot(xb, w_ref[...], preferred_element_type=F32)
    cb = proj[:, 0:CONV_W]
    u = proj[:, CONV_W:2 * CONV_W] * proj[:, 2 * CONV_W:3 * CONV_W]
    xh = jnp.concatenate([xp_ref[...], xn_ref[...]], axis=0)
    xhb = _rms(xh, g).astype(BF16)
    ph = jnp.dot(xhb, w_ref[:, CONV_W:3 * CONV_W], preferred_element_type=F32)
    uh = ph[:, 0:CONV_W] * ph[:, CONV_W:2 * CONV_W]
    first = meta_ref[1, i]
    last = meta_ref[2, i]
    u_prev = jnp.where(first == 1, 0.0, uh[SUBLANES - 1:SUBLANES, :])
    u_next = jnp.where(last == 1, 0.0, uh[SUBLANES:SUBLANES + 1, :])
    row = lax.broadcasted_iota(jnp.int32, (tm, 1), 0)
    up = jnp.where(row == 0, u_prev, pltpu.roll(u, 1, 0))
    dn = jnp.where(row == tm - 1, u_next, pltpu.roll(u, tm - 1, 0))
    cw = cw_ref[...]
    y = up * cw[0:1, :] + u * cw[1:2, :] + dn * cw[2:3, :]
    conv = (cb * y).astype(BF16)
    conv_br = jnp.dot(conv, wc_ref[...], preferred_element_type=F32)
    attn_br = jnp.dot(o_ref[...], wa_ref[...], preferred_element_type=F32)
    g0 = jax.nn.sigmoid(proj[:, 3 * CONV_W:3 * CONV_W + D_MODEL])
    g1 = jax.nn.sigmoid(proj[:, 3 * CONV_W + D_MODEL:REST_W])
    merged = (g0 * attn_br + g1 * conv_br).astype(BF16)
    h_ref[...] = x + jnp.dot(merged, wm_ref[...], preferred_element_type=F32)


def _mixer_call(x, o, g_mix, w_rest, conv_w, w_attn_br, w_conv_br, w_mix_out, groups):
    n = x.shape[0]
    tm = TM_MIX
    per = tm // SUBLANES
    nsb = n // SUBLANES
    meta = _tile_meta(groups, tm)
    gs = pltpu.PrefetchScalarGridSpec(
        num_scalar_prefetch=1, grid=(n // tm,),
        in_specs=[pl.BlockSpec((tm, D_MODEL), lambda i, m: (i, 0)),
                  pl.BlockSpec((SUBLANES, D_MODEL), lambda i, m: (jnp.maximum(i * per - 1, 0), 0)),
                  pl.BlockSpec((SUBLANES, D_MODEL), lambda i, m: (jnp.minimum((i + 1) * per, nsb - 1), 0)),
                  pl.BlockSpec((tm, ATTN_W), lambda i, m: (i, 0)),
                  _const_spec((1, D_MODEL)),
                  _const_spec((D_MODEL, REST_W)),
                  _const_spec((3, CONV_W)),
                  _const_spec((ATTN_W, D_MODEL)),
                  _const_spec((CONV_W, D_MODEL)),
                  _const_spec((D_MODEL, D_MODEL))],
        out_specs=pl.BlockSpec((tm, D_MODEL), lambda i, m: (i, 0)))
    return pl.pallas_call(
        _mixer_kernel, grid_spec=gs, name="mixer",
        out_shape=jax.ShapeDtypeStruct((n, D_MODEL), F32),
        compiler_params=_params())(meta, x, x, x, o, g_mix, w_rest, conv_w, w_attn_br, w_conv_br,
                                   w_mix_out)


def _memkv_kernel(mem_ref, g_ref, w_ref, kv_ref):
    mn = _rms(mem_ref[0], g_ref[...]).astype(BF16)
    kv_ref[0] = jnp.dot(mn, w_ref[...], preferred_element_type=F32).astype(BF16)


def _memkv_call(mem, g_mem, w_xkv):
    nb = mem.shape[0]
    return pl.pallas_call(
        _memkv_kernel, grid=(nb,), name="mem_kv",
        in_specs=[pl.BlockSpec((1, N_MEM, D_MODEL), lambda b: (b, 0, 0)),
                  _const_spec((1, D_MODEL)),
                  _const_spec((D_MODEL, 2 * D_MODEL))],
        out_specs=pl.BlockSpec((1, N_MEM, 2 * D_MODEL), lambda b: (b, 0, 0)),
        out_shape=jax.ShapeDtypeStruct((nb, N_MEM, 2 * D_MODEL), BF16),
        compiler_params=_params())(mem, g_mem, w_xkv)


def _cross_kernel(meta_ref, h_ref, kv_ref, gc_ref, wq_ref, wo_ref, gm_ref, wr_ref, br_ref,
                  h2_ref, xs_ref, idx_ref, gw_ref):
    del meta_ref
    tm = h_ref.shape[0]
    h = h_ref[...]
    hn = _rms(h, gc_ref[...]).astype(BF16)
    q = (jnp.dot(hn, wq_ref[...], preferred_element_type=F32) * (X_HEAD_DIM ** -0.5)).astype(BF16)
    outs = []
    for hd in range(X_HEADS):
        qh = q[:, hd * X_HEAD_DIM:(hd + 1) * X_HEAD_DIM]
        kh = kv_ref[0, :, hd * X_HEAD_DIM:(hd + 1) * X_HEAD_DIM]
        vh = kv_ref[0, :, D_MODEL + hd * X_HEAD_DIM:D_MODEL + (hd + 1) * X_HEAD_DIM]
        s = lax.dot_general(qh, kh, (((1,), (1,)), ((), ())), preferred_element_type=F32)
        m = jnp.max(s, axis=-1, keepdims=True)
        p = jnp.exp(s - m)
        p = (p / jnp.sum(p, axis=-1, keepdims=True)).astype(BF16)
        outs.append(jnp.dot(p, vh, preferred_element_type=F32).astype(BF16))
    o = jnp.concatenate(outs, axis=1)
    h2 = h + jnp.dot(o, wo_ref[...], preferred_element_type=F32)
    h2_ref[...] = h2

    hn3 = _rms(h2, gm_ref[...])
    hi = hn3.astype(BF16)
    hi32 = hi.astype(F32)
    lo = (hn3 - hi32).astype(BF16)
    xs_ref[...] = hn3

    nt = (((1,), (1,)), ((), ()))
    r1 = lax.dot_general(wr_ref[...], hi, nt, preferred_element_type=F32)
    r2 = lax.dot_general(wr_ref[...], lo, nt, preferred_element_type=F32)
    e = N_EXPERTS
    logits = ((r2[e:2 * e] + r2[0:e]) + r1[e:2 * e]) + r1[0:e] + br_ref[...]
    eio = lax.broadcasted_iota(jnp.int32, (e, tm), 0)
    cur = logits
    vals, sels = [], []
    for _ in range(TOP_K):
        mx = jnp.max(cur, axis=0, keepdims=True)
        sel = jnp.min(jnp.where(cur == mx, eio, e), axis=0, keepdims=True)
        vals.append(mx)
        sels.append(sel)
        cur = jnp.where(eio == sel, -jnp.inf, cur)
    ex = [jnp.exp(v - vals[0]) for v in vals]
    tot = ex[0] + ex[1] + ex[2] + ex[3]
    idx_ref[...] = jnp.concatenate(sels, axis=0)
    gw_ref[...] = jnp.concatenate([x / tot for x in ex] + [jnp.zeros((SUBLANES - TOP_K, tm), F32)], axis=0)


def _cross_call(h1, memkv, g_cross, w_xq, w_xo, g_moe, w_r2t, b_router, groups):
    n = h1.shape[0]
    tm = TM_CROSS
    meta = _tile_meta(groups, tm)
    row = lambda i, m: (i, 0)
    col = lambda i, m: (0, i)
    gs = pltpu.PrefetchScalarGridSpec(
        num_scalar_prefetch=1, grid=(n // tm,),
        in_specs=[pl.BlockSpec((tm, D_MODEL), row),
                  pl.BlockSpec((1, N_MEM, 2 * D_MODEL), lambda i, m: (m[3, i], 0, 0)),
                  _const_spec((1, D_MODEL)),
                  _const_spec((D_MODEL, D_MODEL)),
                  _const_spec((D_MODEL, D_MODEL)),
                  _const_spec((1, D_MODEL)),
                  _const_spec((2 * N_EXPERTS, D_MODEL)),
                  _const_spec((N_EXPERTS, 1))],
        out_specs=[pl.BlockSpec((tm, D_MODEL), row),
                   pl.BlockSpec((tm, D_MODEL), row),
                   pl.BlockSpec((TOP_K, tm), col),
                   pl.BlockSpec((SUBLANES, tm), col)])
    return pl.pallas_call(
        _cross_kernel, grid_spec=gs, name="cross_router",
        out_shape=[jax.ShapeDtypeStruct((n, D_MODEL), F32),
                   jax.ShapeDtypeStruct((n, D_MODEL), F32),
                   jax.ShapeDtypeStruct((TOP_K, n), jnp.int32),
                   jax.ShapeDtypeStruct((SUBLANES, n), F32)],
        compiler_params=_params())(meta, h1, memkv, g_cross, w_xq, w_xo, g_moe, w_r2t, b_router)


TP_PLAN = 512
LOG2_BM = BM_MOE.bit_length() - 1
assert 1 << LOG2_BM == BM_MOE


def _plan_kernel(n_tiles, idx_ref, dest_ref, meta_ref, tri_ref, acc_ref, run_ref):
    i = pl.program_id(0)
    e, tp, td = N_EXPERTS, TP_PLAN, TC_COMB
    nb_pad = meta_ref.shape[1]
    eio = lax.broadcasted_iota(jnp.int32, (e, tp), 0)
    idx = idx_ref[...]
    onehots = [eio == idx[k:k + 1, :] for k in range(TOP_K)]
    s = jnp.where(onehots[0], 1.0, 0.0)
    for k in range(1, TOP_K):
        s = s + jnp.where(onehots[k], 1.0, 0.0)

    @pl.when(i == 0)
    def _():
        acc_ref[...] = jnp.zeros_like(acc_ref)
        r = lax.broadcasted_iota(jnp.int32, (tp, tp), 0)
        c = lax.broadcasted_iota(jnp.int32, (tp, tp), 1)
        tri_ref[...] = jnp.where(r < c, 1.0, 0.0).astype(BF16)

    @pl.when(i < n_tiles)
    def _():
        acc_ref[...] += s

    @pl.when(i == n_tiles)
    def _():
        counts = jnp.sum(acc_ref[...], axis=1, keepdims=True).astype(jnp.int32)
        cnt = jnp.broadcast_to(counts, (e, LANES))
        pad = lax.shift_left(lax.shift_right_logical(cnt + (BM_MOE - 1), LOG2_BM), LOG2_BM)
        row = lax.broadcasted_iota(jnp.int32, (e, LANES), 0)
        pad_end = pad
        sft = 1
        while sft < e:
            pad_end = pad_end + jnp.where(row >= sft, pltpu.roll(pad_end, sft, 0), 0)
            sft *= 2
        pad_start = pad_end - pad
        run_ref[...] = pad_start.astype(F32)
        bpos = lax.broadcasted_iota(jnp.int32, (e, nb_pad), 1) * BM_MOE
        ebl = lax.broadcasted_iota(jnp.int32, (e, nb_pad), 0)
        done = jnp.where(pad_end[:, 0:1] <= bpos, 1.0, 0.0)
        bexp = jnp.minimum(jnp.sum(done, axis=0, keepdims=True).astype(jnp.int32), e - 1)
        row_end = (pad_start + cnt)[:, 0:1].astype(F32)
        rend_b = jnp.sum(jnp.where(ebl == bexp, row_end, 0.0), axis=0, keepdims=True).astype(jnp.int32)
        nvalid = jnp.clip(rend_b - bpos[0:1, :], 0, BM_MOE)
        total = jnp.sum(jnp.where(ebl == e - 1, pad_end[:, 0:1].astype(F32), 0.0), axis=0, keepdims=True)
        nused = lax.shift_right_logical(total.astype(jnp.int32), LOG2_BM)
        meta_ref[...] = jnp.concatenate(
            [bexp, nvalid, nused, jnp.zeros((SUBLANES - 3, nb_pad), jnp.int32)], axis=0)

    @pl.when(i >= n_tiles)
    def _():
        before = jnp.dot(s.astype(BF16), tri_ref[...], preferred_element_type=F32)
        base = run_ref[:, 0:1] + before
        dk = [jnp.sum(jnp.where(onehots[k], base, 0.0), axis=0, keepdims=True).astype(jnp.int32)
              for k in range(TOP_K)]
        for j in range(tp // td):
            dest_ref[j] = jnp.concatenate([d[:, j * td:(j + 1) * td] for d in dk], axis=1)
        run_ref[...] = run_ref[...] + jnp.sum(s, axis=1, keepdims=True)


def _plan_call(idx, n_blocks):
    n = idx.shape[1]
    tp, td = TP_PLAN, TC_COMB
    n_tiles = n // tp
    nb_pad = -(-n_blocks // LANES) * LANES
    per = tp // td
    return pl.pallas_call(
        functools.partial(_plan_kernel, n_tiles), grid=(2 * n_tiles,), name="route_plan",
        in_specs=[pl.BlockSpec((TOP_K, tp), lambda i: (0, i % n_tiles))],
        out_specs=[pl.BlockSpec((per, 1, TOP_K * td), lambda i: (jnp.maximum(i - n_tiles, 0), 0, 0)),
                   pl.BlockSpec((SUBLANES, nb_pad), lambda i: (0, 0))],
        out_shape=[jax.ShapeDtypeStruct((n // td, 1, TOP_K * td), jnp.int32),
                   jax.ShapeDtypeStruct((SUBLANES, nb_pad), jnp.int32)],
        scratch_shapes=[pltpu.VMEM((tp, tp), BF16), pltpu.VMEM((N_EXPERTS, tp), F32),
                        pltpu.VMEM((N_EXPERTS, LANES), F32)],
        compiler_params=_params())(idx)


ROW_UNROLL = 8


def _dispatch_kernel(n_blocks, nvalid_ref, dst_ref, xs_ref, out_hbm, zero_ref, sem, zsem):
    td = xs_ref.shape[0]
    n_rows = TOP_K * td
    bm = zero_ref.shape[0]

    @pl.when(pl.program_id(0) == 0)
    def _():
        zero_ref[...] = jnp.zeros_like(zero_ref)

        def fill(b):
            return pltpu.make_async_copy(zero_ref, out_hbm.at[pl.ds(pl.multiple_of(b * bm, bm), bm), :], zsem)

        def start(b, carry):
            @pl.when(nvalid_ref[b] < bm)
            def _():
                fill(b).start()
            return carry

        def wait(b, carry):
            @pl.when(nvalid_ref[b] < bm)
            def _():
                fill(b).wait()
            return carry
        lax.fori_loop(0, n_blocks, start, 0)
        lax.fori_loop(0, n_blocks, wait, 0)

    def body(j, carry):
        for u in range(ROW_UNROLL):
            r = j * ROW_UNROLL + u
            t = r & (td - 1)
            pltpu.make_async_copy(xs_ref.at[pl.ds(t, 1), :], out_hbm.at[pl.ds(dst_ref[0, 0, r], 1), :],
                                  sem).start()
        return carry
    lax.fori_loop(0, n_rows // ROW_UNROLL, body, 0)
    pltpu.make_async_copy(out_hbm.at[pl.ds(0, n_rows), :], out_hbm.at[pl.ds(0, n_rows), :], sem).wait()


def _dispatch_call(nvalid, dest, xs, n_blocks):
    n, width = xs.shape
    td, bm = TC_COMB, BM_MOE
    assert td & (td - 1) == 0
    gs = pltpu.PrefetchScalarGridSpec(
        num_scalar_prefetch=1, grid=(n // td,),
        in_specs=[pl.BlockSpec((1, 1, TOP_K * td), lambda i, nv: (i, 0, 0), memory_space=pltpu.SMEM),
                  pl.BlockSpec((td, width), lambda i, nv: (i, 0))],
        out_specs=pl.BlockSpec(memory_space=pl.ANY),
        scratch_shapes=[pltpu.VMEM((bm, width), xs.dtype), pltpu.SemaphoreType.DMA(()),
                        pltpu.SemaphoreType.DMA(())])
    return pl.pallas_call(
        functools.partial(_dispatch_kernel, n_blocks), grid_spec=gs, name="moe_dispatch",
        out_shape=jax.ShapeDtypeStruct((n_blocks * bm, width), xs.dtype),
        compiler_params=_params())(nvalid, dest, xs)


def _row_gather(src_hbm, dst, sem, idx_ref, n_rows):
    def body(j, carry):
        for t in range(ROW_UNROLL):
            r = j * ROW_UNROLL + t
            pltpu.make_async_copy(src_hbm.at[pl.ds(idx_ref[0, 0, r], 1), :], dst.at[pl.ds(r, 1), :], sem).start()
        return carry
    lax.fori_loop(0, n_rows // ROW_UNROLL, body, 0)


def _ffn_kernel(bexp_ref, nused_ref, x_ref, wg_ref, bg_ref, wu_ref, bu_ref, wd_ref, bd_ref,
                y_ref, wgb, wub, wdb):
    b = pl.program_id(0)
    nused = nused_ref[0]

    @pl.when((b == 0) | (bexp_ref[b] != bexp_ref[jnp.maximum(b - 1, 0)]))
    def _():
        wgb[...] = wg_ref[0].astype(BF16)
        wub[...] = wu_ref[0].astype(BF16)
        wdb[...] = wd_ref[0].astype(BF16)

    @pl.when(b < nused)
    def _():
        x = x_ref[...].astype(BF16)

        def proj(w_ref, b_ref):
            return jnp.dot(x, w_ref[...], preferred_element_type=F32) + b_ref[0]

        a = jnp.minimum(proj(wgb, bg_ref), SWIGLU_LIMIT)
        u = jnp.clip(proj(wub, bu_ref), -SWIGLU_LIMIT, SWIGLU_LIMIT)
        hid = (a * jax.nn.sigmoid(SWIGLU_ALPHA * a) * (u + 1.0)).astype(BF16)
        y_ref[...] = jnp.dot(hid, wdb[...], preferred_element_type=F32) + bd_ref[0]

    @pl.when(b >= nused)
    def _():
        y_ref[...] = jnp.zeros_like(y_ref)


def _ffn_call(bexp, nused, xs_sorted, w_gate, b_gate, w_up, b_up, w_down, b_down):
    bm = BM_MOE
    n_blocks = xs_sorted.shape[0] // bm
    wspec = lambda r, c: pl.BlockSpec((1, r, c), lambda b, be, nu: (be[b], 0, 0))
    gs = pltpu.PrefetchScalarGridSpec(
        num_scalar_prefetch=2, grid=(n_blocks,),
        in_specs=[pl.BlockSpec((bm, D_MODEL), lambda b, be, nu: (jnp.minimum(b, nu[0] - 1), 0)),
                  wspec(D_MODEL, D_FF), wspec(1, D_FF), wspec(D_MODEL, D_FF), wspec(1, D_FF),
                  wspec(D_FF, D_MODEL), wspec(1, D_MODEL)],
        out_specs=pl.BlockSpec((bm, D_MODEL), lambda b, be, nu: (b, 0)),
        scratch_shapes=[pltpu.VMEM((D_MODEL, D_FF), BF16), pltpu.VMEM((D_MODEL, D_FF), BF16),
                        pltpu.VMEM((D_FF, D_MODEL), BF16)])
    return pl.pallas_call(
        _ffn_kernel, grid_spec=gs, name="expert_ffn",
        out_shape=jax.ShapeDtypeStruct((n_blocks * bm, D_MODEL), F32),
        compiler_params=_params())(bexp, nused, xs_sorted, w_gate, b_gate, w_up, b_up, w_down, b_down)


def _combine_kernel(nt, dst_ref, dstn_ref, y_hbm, h_ref, gw_ref, gf_ref, out_ref, ybuf, sems):
    i = pl.program_id(0)
    tc = h_ref.shape[0]
    slot = i % 2

    @pl.when(i == 0)
    def _():
        _row_gather(y_hbm, ybuf.at[0], sems.at[0], dst_ref, TOP_K * tc)

    @pl.when(i + 1 < nt)
    def _():
        _row_gather(y_hbm, ybuf.at[1 - slot], sems.at[1 - slot], dstn_ref, TOP_K * tc)

    pltpu.make_async_copy(ybuf.at[slot], ybuf.at[slot], sems.at[slot]).wait()
    gwt = jnp.transpose(gw_ref[...])
    acc = h_ref[...]
    for k in range(TOP_K):
        acc = acc + gwt[:, k:k + 1] * ybuf[slot, k * tc:(k + 1) * tc, :]
    out_ref[...] = _rms(acc, gf_ref[...])


def _combine_call(dest, y, h2, gw, g_final, tile0, n_tiles):
    tc = TC_COMB
    last = tile0 + n_tiles - 1
    return pl.pallas_call(
        functools.partial(_combine_kernel, n_tiles), grid=(n_tiles,), name="moe_combine",
        in_specs=[pl.BlockSpec((1, 1, TOP_K * tc), lambda i: (tile0 + i, 0, 0), memory_space=pltpu.SMEM),
                  pl.BlockSpec((1, 1, TOP_K * tc), lambda i: (jnp.minimum(tile0 + i + 1, last), 0, 0),
                               memory_space=pltpu.SMEM),
                  pl.BlockSpec(memory_space=pl.ANY),
                  pl.BlockSpec((tc, D_MODEL), lambda i: (tile0 + i, 0)),
                  pl.BlockSpec((SUBLANES, tc), lambda i: (0, tile0 + i)),
                  _const_spec((1, D_MODEL))],
        out_specs=pl.BlockSpec((tc, D_MODEL), lambda i: (i, 0)),
        out_shape=jax.ShapeDtypeStruct((n_tiles * tc, D_MODEL), F32),
        scratch_shapes=[pltpu.VMEM((2, TOP_K * tc, D_MODEL), F32), pltpu.SemaphoreType.DMA((2,))],
        compiler_params=_params())(dest, dest, y, h2, gw, g_final)


def _forward(xs, mems, g_mix, w_in, sink, conv_w, w_attn_br, w_conv_br, w_mix_out, g_cross, g_mem,
             w_xq, w_xkv, w_xo, g_moe, w_router, b_router, w_gate, b_gate, w_up, b_up, w_down,
             b_down, g_final):
    groups = [(x.shape[0], x.shape[1]) for x in xs]
    for _, s in groups:
        assert s % max(TM_QKV, TQ_ATTN, TM_MIX, TM_CROSS, TC_COMB) == 0
    h = jnp.concatenate([x.reshape(-1, D_MODEL) for x in xs], axis=0)
    mem = jnp.concatenate(mems, axis=0)
    n = h.shape[0]
    tables = _rope_tables(max(s for _, s in groups))
    assert w_in.shape[0] == 1, "single-layer trunk: the final norm is fused into the combine kernel"
    l = 0
    row2 = lambda v: v.reshape(1, -1)
    w_in_b = w_in[l].astype(BF16)
    q, kv = _qkv_call(h, row2(g_mix[l]), w_in_b[:, :QKV_W], tables, groups)
    o = _attn_call(q, kv, sink[l], groups)
    h = _mixer_call(h, o, row2(g_mix[l]), w_in_b[:, QKV_W:], conv_w[l], w_attn_br[l].astype(BF16),
                    w_conv_br[l].astype(BF16), w_mix_out[l].astype(BF16), groups)
    memkv = _memkv_call(mem, row2(g_mem[l]), w_xkv[l].astype(BF16))
    wr = w_router[l]
    wr_hi = wr.astype(BF16)
    wr_lo = (wr - wr_hi.astype(F32)).astype(BF16)
    w_r2t = jnp.concatenate([wr_hi.T, wr_lo.T], axis=0)
    h2, xs_packed, idx, gw = _cross_call(h, memkv, row2(g_cross[l]), w_xq[l].astype(BF16),
                                         w_xo[l].astype(BF16), row2(g_moe[l]), w_r2t,
                                         b_router[l].reshape(-1, 1), groups)
    n_blocks = -(-TOP_K * n // BM_MOE) + N_EXPERTS
    dest, meta = _plan_call(idx, n_blocks)
    xs_sorted = _dispatch_call(meta[1], dest, xs_packed, n_blocks)
    y = _ffn_call(meta[0], meta[2, 0:1], xs_sorted, w_gate[l], b_gate[l][:, None, :],
                  w_up[l], b_up[l][:, None, :], w_down[l], b_down[l][:, None, :])
    tc = TC_COMB
    outs, t0 = [], 0
    for nb, s in groups:
        nt = nb * s // tc
        outs.append(_combine_call(dest, y, h2, gw, row2(g_final), t0, nt))
        t0 += nt
    return tuple(o.reshape(x.shape) for o, x in zip(outs, xs))


def kernel(x_prompt, x_sample, mem_prompt, mem_sample, g_mix, w_in, sink, conv_w, w_attn_br, w_conv_br,
           w_mix_out, g_cross, g_mem, w_xq, w_xkv, w_xo, g_moe, w_router, b_router, w_gate, b_gate,
           w_up, b_up, w_down, b_down, g_final):
    return _forward([x_prompt, x_sample], [mem_prompt, mem_sample], g_mix, w_in, sink, conv_w,
                    w_attn_br, w_conv_br, w_mix_out, g_cross, g_mem, w_xq, w_xkv, w_xo, g_moe,
                    w_router, b_router, w_gate, b_gate, w_up, b_up, w_down, b_down, g_final)
```

```python
import functools

import numpy as np
import jax
import jax.numpy as jnp
from jax import lax
from jax.experimental import pallas as pl
from jax.experimental.pallas import tpu as pltpu

F32 = jnp.float32
BF16 = jnp.bfloat16

D_MODEL = 1024
N_HEADS = 8
N_KV_HEADS = 2
HEAD_DIM = 64
GROUP = N_HEADS // N_KV_HEADS
ATTN_W = N_HEADS * HEAD_DIM
KV_W = N_KV_HEADS * HEAD_DIM
QKV_W = ATTN_W + 2 * KV_W
WINDOW = 128
ROT_DIM = HEAD_DIM // 4
ROPE_THETA = 500000.0
CONV_W = D_MODEL // 2
REST_W = 3 * CONV_W + 2 * D_MODEL
N_MEM = 256
X_HEADS = 4
X_HEAD_DIM = D_MODEL // X_HEADS
N_EXPERTS = 32
TOP_K = 4
D_FF = D_MODEL
SWIGLU_ALPHA = 1.702
SWIGLU_LIMIT = 7.0
EPS = 1e-5

LANES = 128
SUBLANES = 8
KEY_BLOCK = WINDOW
TM_QKV = 512
TQ_ATTN = 256
TM_MIX = 256
TM_CROSS = 256
BM_MOE = 256
TC_COMB = 256
NEG_BIG = -1e30
VMEM_LIMIT = 56 * 1024 * 1024


def _rms(x, g):
    var = jnp.mean(x * x, axis=-1, keepdims=True)
    return x * lax.rsqrt(var + EPS) * g


def _tile_meta(groups, tile):
    pos, first, last, bidx = [], [], [], []
    b0 = 0
    for nb, s in groups:
        per = s // tile
        for b in range(nb):
            for j in range(per):
                pos.append(j)
                first.append(int(j == 0))
                last.append(int(j == per - 1))
                bidx.append(b0 + b)
        b0 += nb
    return jnp.asarray(np.array([pos, first, last, bidx], dtype=np.int32))


def _const_spec(shape):
    nd = len(shape)
    return pl.BlockSpec(shape, lambda *_: (0,) * nd)


def _params(vmem=VMEM_LIMIT):
    return pltpu.CompilerParams(dimension_semantics=("arbitrary",), vmem_limit_bytes=vmem)


def _qkv_kernel(na, meta_ref, xa_ref, xb_ref, g_ref, w_ref, cos_ref, sa_ref, sb_ref, q_ref, kv_ref):
    del meta_ref
    x = jnp.where(pl.program_id(0) < na, xa_ref[...], xb_ref[...])
    xn = _rms(x, g_ref[...]).astype(BF16)
    proj = jnp.dot(xn, w_ref[...], preferred_element_type=F32)
    c, sa, sb = cos_ref[...], sa_ref[...], sb_ref[...]
    n_rot = (ATTN_W + KV_W) // LANES
    for gi in range(n_rot):
        p = proj[:, gi * LANES:(gi + 1) * LANES]
        r = p * c + pltpu.roll(p, LANES - ROT_DIM // 2, 1) * sa + pltpu.roll(p, ROT_DIM // 2, 1) * sb
        if gi < ATTN_W // LANES:
            q_ref[:, gi * LANES:(gi + 1) * LANES] = (r * (HEAD_DIM ** -0.5)).astype(BF16)
        else:
            kv_ref[:, 0:KV_W] = r.astype(BF16)
    kv_ref[:, KV_W:2 * KV_W] = proj[:, ATTN_W + KV_W:QKV_W].astype(BF16)


def _rope_tables(s_max):
    half = ROT_DIM // 2
    inv_freq = ROPE_THETA ** (-(jnp.arange(half, dtype=F32) * 2.0) / ROT_DIM)
    ang = jnp.arange(s_max, dtype=F32)[:, None] * inv_freq[None, :]
    cos, sin = jnp.cos(ang), jnp.sin(ang)
    d = np.arange(LANES) % HEAD_DIM
    j = d % half
    cos_l, sin_l = cos[:, j], sin[:, j]
    rot = jnp.asarray(d < ROT_DIM)[None, :]
    lo = jnp.asarray(d < half)[None, :]
    hi = jnp.asarray((d >= half) & (d < ROT_DIM))[None, :]
    c = jnp.where(rot, cos_l, 1.0)
    sa = jnp.where(lo, -sin_l, 0.0)
    sb = jnp.where(hi, sin_l, 0.0)
    return c, sa, sb


def _qkv_call(xa, xb, g_mix, w_qkv, tables, groups):
    tm = TM_QKV
    na = xa.shape[0] // tm
    n = xa.shape[0] + xb.shape[0]
    meta = _tile_meta(groups, tm)
    row = lambda i, m: (i, 0)
    tab = lambda i, m: (m[0, i], 0)
    gs = pltpu.PrefetchScalarGridSpec(
        num_scalar_prefetch=1, grid=(n // tm,),
        in_specs=[pl.BlockSpec((tm, D_MODEL), lambda i, m: (jnp.minimum(i, na - 1), 0)),
                  pl.BlockSpec((tm, D_MODEL), lambda i, m: (jnp.maximum(i - na, 0), 0)),
                  _const_spec((1, D_MODEL)),
                  _const_spec((D_MODEL, QKV_W)),
                  pl.BlockSpec((tm, LANES), tab), pl.BlockSpec((tm, LANES), tab),
                  pl.BlockSpec((tm, LANES), tab)],
        out_specs=[pl.BlockSpec((tm, ATTN_W), row), pl.BlockSpec((tm, 2 * KV_W), row)])
    return pl.pallas_call(
        functools.partial(_qkv_kernel, na), grid_spec=gs, name="qkv_rope",
        out_shape=[jax.ShapeDtypeStruct((n, ATTN_W), BF16), jax.ShapeDtypeStruct((n, 2 * KV_W), BF16)],
        compiler_params=_params())(meta, xa, xb, g_mix, w_qkv, *tables)


def _attn_kernel(meta_ref, sink_ref, q_ref, kvp_ref, kvm_ref, kvn_ref, o_ref, kcat_ref):
    i = pl.program_id(0)
    kb = KEY_BLOCK
    tq = q_ref.shape[0]
    kcat_ref[0:kb, :] = kvp_ref[...]
    kcat_ref[kb:kb + tq, :] = kvm_ref[...]
    kcat_ref[kb + tq:kb + tq + kb, :] = kvn_ref[...]
    r = lax.broadcasted_iota(jnp.int32, (kb, 3 * kb), 0)
    c = lax.broadcasted_iota(jnp.int32, (kb, 3 * kb), 1)
    dlt = c - r
    klane = lax.broadcasted_iota(jnp.int32, (3 * kb, KV_W), 1)
    olane = lax.broadcasted_iota(jnp.int32, (GROUP * kb, KV_W), 1)
    nsub = tq // kb
    for s in range(nsub):
        first = meta_ref[1, i * nsub + s]
        last = meta_ref[2, i * nsub + s]
        lo = jnp.where(first == 1, kb, 0)
        hi = jnp.where(last == 1, 2 * kb, 3 * kb)
        valid = (dlt >= 0) & (dlt <= 2 * WINDOW) & (c >= lo) & (c < hi)
        bias = jnp.where(valid, 0.0, NEG_BIG)
        bias = jnp.concatenate([bias] * GROUP, axis=0)
        kw = kcat_ref[s * kb:(s + 3) * kb, :]
        kc = kw[:, 0:KV_W]
        vc = jnp.concatenate([kw[:, KV_W:2 * KV_W], jnp.ones((3 * kb, KV_W), BF16)], axis=1)
        q4 = jnp.concatenate([q_ref[s * kb:(s + 1) * kb, g * KV_W:(g + 1) * KV_W] for g in range(GROUP)],
                             axis=0)
        res = []
        for kh in range(N_KV_HEADS):
            own = (klane >= kh * HEAD_DIM) & (klane < (kh + 1) * HEAD_DIM)
            kk = jnp.where(own, kc, jnp.zeros_like(kc))
            sc = lax.dot_general(q4, kk, (((1,), (1,)), ((), ())), preferred_element_type=F32) + bias
            snk = jnp.concatenate([jnp.full((kb, 1), sink_ref[kh * GROUP + g], F32) for g in range(GROUP)],
                                  axis=0)
            m = jnp.maximum(jnp.max(sc, axis=-1, keepdims=True), snk)
            p = jnp.exp(sc - m).astype(BF16)
            pv = jnp.dot(p, vc, preferred_element_type=F32)
            denom = pv[:, KV_W:KV_W + 1] + jnp.exp(snk - m)
            res.append(pv[:, 0:KV_W] / denom)
        out = res[0]
        for kh in range(1, N_KV_HEADS):
            out = jnp.where(olane >= kh * HEAD_DIM, res[kh], out)
        out = out.astype(BF16)
        for g in range(GROUP):
            o_ref[s * kb:(s + 1) * kb, g * KV_W:(g + 1) * KV_W] = out[g * kb:(g + 1) * kb, :]


def _attn_call(q, kv, sink, groups):
    n = q.shape[0]
    tq, kb = TQ_ATTN, KEY_BLOCK
    per = tq // kb
    nkb = n // kb
    meta = _tile_meta(groups, kb)
    gs = pltpu.PrefetchScalarGridSpec(
        num_scalar_prefetch=1, grid=(n // tq,),
        in_specs=[pl.BlockSpec(memory_space=pltpu.SMEM),
                  pl.BlockSpec((tq, ATTN_W), lambda i, m: (i, 0)),
                  pl.BlockSpec((kb, 2 * KV_W), lambda i, m: (jnp.maximum(i * per - 1, 0), 0)),
                  pl.BlockSpec((tq, 2 * KV_W), lambda i, m: (i, 0)),
                  pl.BlockSpec((kb, 2 * KV_W), lambda i, m: (jnp.minimum((i + 1) * per, nkb - 1), 0))],
        out_specs=pl.BlockSpec((tq, ATTN_W), lambda i, m: (i, 0)),
        scratch_shapes=[pltpu.VMEM((tq + 2 * kb, 2 * KV_W), BF16)])
    return pl.pallas_call(
        _attn_kernel, grid_spec=gs, name="window_attn",
        out_shape=jax.ShapeDtypeStruct((n, ATTN_W), BF16),
        compiler_params=_params())(meta, sink, q, kv, kv, kv)


def _mixer_kernel(na, meta_ref, xa_ref, xpa_ref, xna_ref, xb_ref, xpb_ref, xnb_ref, o_ref, g_ref, w_ref,
                  cw_ref, wa_ref, wc_ref, wm_ref, h_ref):
    i = pl.program_id(0)
    tm = xa_ref.shape[0]
    g = g_ref[...]
    in_a = i < na
    x = jnp.where(in_a, xa_ref[...], xb_ref[...])
    xb = _rms(x, g).astype(BF16)
    proj = jnp.dot(xb, w_ref[...], preferred_element_type=F32)
    cb = proj[:, 0:CONV_W]
    u = proj[:, CONV_W:2 * CONV_W] * proj[:, 2 * CONV_W:3 * CONV_W]
    xh = jnp.concatenate([jnp.where(in_a, xpa_ref[...], xpb_ref[...]),
                          jnp.where(in_a, xna_ref[...], xnb_ref[...])], axis=0)
    xhb = _rms(xh, g).astype(BF16)
    ph = jnp.dot(xhb, w_ref[:, CONV_W:3 * CONV_W], preferred_element_type=F32)
    uh = ph[:, 0:CONV_W] * ph[:, CONV_W:2 * CONV_W]
    first = meta_ref[1, i]
    last = meta_ref[2, i]
    u_prev = jnp.where(first == 1, 0.0, uh[SUBLANES - 1:SUBLANES, :])
    u_next = jnp.where(last == 1, 0.0, uh[SUBLANES:SUBLANES + 1, :])
    row = lax.broadcasted_iota(jnp.int32, (tm, 1), 0)
    up = jnp.where(row == 0, u_prev, pltpu.roll(u, 1, 0))
    dn = jnp.where(row == tm - 1, u_next, pltpu.roll(u, tm - 1, 0))
    cw = cw_ref[...]
    y = up * cw[0:1, :] + u * cw[1:2, :] + dn * cw[2:3, :]
    conv = (cb * y).astype(BF16)
    conv_br = jnp.dot(conv, wc_ref[...], preferred_element_type=F32)
    attn_br = jnp.dot(o_ref[...], wa_ref[...], preferred_element_type=F32)
    g0 = jax.nn.sigmoid(proj[:, 3 * CONV_W:3 * CONV_W + D_MODEL])
    g1 = jax.nn.sigmoid(proj[:, 3 * CONV_W + D_MODEL:REST_W])
    merged = (g0 * attn_br + g1 * conv_br).astype(BF16)
    h_ref[...] = x + jnp.dot(merged, wm_ref[...], preferred_element_type=F32)


def _mixer_call(xa, xb, o, g_mix, w_rest, conv_w, w_attn_br, w_conv_br, w_mix_out, groups):
    tm = TM_MIX
    per = tm // SUBLANES
    na, nb = xa.shape[0] // tm, xb.shape[0] // tm
    n = xa.shape[0] + xb.shape[0]
    meta = _tile_meta(groups, tm)

    def x_specs(tile_of, n_tiles):
        last8 = n_tiles * per - 1
        return [pl.BlockSpec((tm, D_MODEL), lambda i, m: (jnp.clip(tile_of(i), 0, n_tiles - 1), 0)),
                pl.BlockSpec((SUBLANES, D_MODEL), lambda i, m: (jnp.clip(tile_of(i) * per - 1, 0, last8), 0)),
                pl.BlockSpec((SUBLANES, D_MODEL), lambda i, m: (jnp.clip((tile_of(i) + 1) * per, 0, last8), 0))]

    gs = pltpu.PrefetchScalarGridSpec(
        num_scalar_prefetch=1, grid=(n // tm,),
        in_specs=x_specs(lambda i: i, na) + x_specs(lambda i: i - na, nb) + [
                  pl.BlockSpec((tm, ATTN_W), lambda i, m: (i, 0)),
                  _const_spec((1, D_MODEL)),
                  _const_spec((D_MODEL, REST_W)),
                  _const_spec((3, CONV_W)),
                  _const_spec((ATTN_W, D_MODEL)),
                  _const_spec((CONV_W, D_MODEL)),
                  _const_spec((D_MODEL, D_MODEL))],
        out_specs=pl.BlockSpec((tm, D_MODEL), lambda i, m: (i, 0)))
    return pl.pallas_call(
        functools.partial(_mixer_kernel, na), grid_spec=gs, name="mixer",
        out_shape=jax.ShapeDtypeStruct((n, D_MODEL), F32),
        compiler_params=_params())(meta, xa, xa, xa, xb, xb, xb, o, g_mix, w_rest, conv_w, w_attn_br,
                                   w_conv_br, w_mix_out)


def _memkv_kernel(mem_ref, g_ref, w_ref, kv_ref):
    mn = _rms(mem_ref[0], g_ref[...]).astype(BF16)
    kv_ref[0] = jnp.dot(mn, w_ref[...], preferred_element_type=F32).astype(BF16)


def _memkv_call(mem, g_mem, w_xkv):
    nb = mem.shape[0]
    return pl.pallas_call(
        _memkv_kernel, grid=(nb,), name="mem_kv",
        in_specs=[pl.BlockSpec((1, N_MEM, D_MODEL), lambda b: (b, 0, 0)),
                  _const_spec((1, D_MODEL)),
                  _const_spec((D_MODEL, 2 * D_MODEL))],
        out_specs=pl.BlockSpec((1, N_MEM, 2 * D_MODEL), lambda b: (b, 0, 0)),
        out_shape=jax.ShapeDtypeStruct((nb, N_MEM, 2 * D_MODEL), BF16),
        compiler_params=_params())(mem, g_mem, w_xkv)


def _cross_kernel(meta_ref, h_ref, kv_ref, gc_ref, wq_ref, wo_ref, gm_ref, wr_ref, br_ref,
                  h2_ref, xs_ref, idx_ref, gw_ref):
    del meta_ref
    tm = h_ref.shape[0]
    h = h_ref[...]
    hn = _rms(h, gc_ref[...]).astype(BF16)
    q = (jnp.dot(hn, wq_ref[...], preferred_element_type=F32) * (X_HEAD_DIM ** -0.5)).astype(BF16)
    outs = []
    for hd in range(X_HEADS):
        qh = q[:, hd * X_HEAD_DIM:(hd + 1) * X_HEAD_DIM]
        kh = kv_ref[0, :, hd * X_HEAD_DIM:(hd + 1) * X_HEAD_DIM]
        vh = kv_ref[0, :, D_MODEL + hd * X_HEAD_DIM:D_MODEL + (hd + 1) * X_HEAD_DIM]
        s = lax.dot_general(qh, kh, (((1,), (1,)), ((), ())), preferred_element_type=F32)
        m = jnp.max(s, axis=-1, keepdims=True)
        p = jnp.exp(s - m)
        p = (p / jnp.sum(p, axis=-1, keepdims=True)).astype(BF16)
        outs.append(jnp.dot(p, vh, preferred_element_type=F32).astype(BF16))
    o = jnp.concatenate(outs, axis=1)
    h2 = h + jnp.dot(o, wo_ref[...], preferred_element_type=F32)
    h2_ref[...] = h2

    hn3 = _rms(h2, gm_ref[...])
    hi = hn3.astype(BF16)
    hi32 = hi.astype(F32)
    lo = (hn3 - hi32).astype(BF16)
    xs_ref[...] = hn3

    nt = (((1,), (1,)), ((), ()))
    r1 = lax.dot_general(wr_ref[...], hi, nt, preferred_element_type=F32)
    r2 = lax.dot_general(wr_ref[...], lo, nt, preferred_element_type=F32)
    e = N_EXPERTS
    logits = ((r2[e:2 * e] + r2[0:e]) + r1[e:2 * e]) + r1[0:e] + br_ref[...]
    eio = lax.broadcasted_iota(jnp.int32, (e, tm), 0)
    cur = logits
    vals, sels = [], []
    for _ in range(TOP_K):
        mx = jnp.max(cur, axis=0, keepdims=True)
        sel = jnp.min(jnp.where(cur == mx, eio, e), axis=0, keepdims=True)
        vals.append(mx)
        sels.append(sel)
        cur = jnp.where(eio == sel, -jnp.inf, cur)
    ex = [jnp.exp(v - vals[0]) for v in vals]
    tot = ex[0] + ex[1] + ex[2] + ex[3]
    idx_ref[...] = jnp.concatenate(sels, axis=0)
    gw_ref[...] = jnp.concatenate([x / tot for x in ex] + [jnp.zeros((SUBLANES - TOP_K, tm), F32)], axis=0)


def _cross_call(h1, memkv, g_cross, w_xq, w_xo, g_moe, w_r2t, b_router, groups):
    n = h1.shape[0]
    tm = TM_CROSS
    meta = _tile_meta(groups, tm)
    row = lambda i, m: (i, 0)
    col = lambda i, m: (0, i)
    gs = pltpu.PrefetchScalarGridSpec(
        num_scalar_prefetch=1, grid=(n // tm,),
        in_specs=[pl.BlockSpec((tm, D_MODEL), row),
                  pl.BlockSpec((1, N_MEM, 2 * D_MODEL), lambda i, m: (m[3, i], 0, 0)),
                  _const_spec((1, D_MODEL)),
                  _const_spec((D_MODEL, D_MODEL)),
                  _const_spec((D_MODEL, D_MODEL)),
                  _const_spec((1, D_MODEL)),
                  _const_spec((2 * N_EXPERTS, D_MODEL)),
                  _const_spec((N_EXPERTS, 1))],
        out_specs=[pl.BlockSpec((tm, D_MODEL), row),
                   pl.BlockSpec((tm, D_MODEL), row),
                   pl.BlockSpec((TOP_K, tm), col),
                   pl.BlockSpec((SUBLANES, tm), col)])
    return pl.pallas_call(
        _cross_kernel, grid_spec=gs, name="cross_router",
        out_shape=[jax.ShapeDtypeStruct((n, D_MODEL), F32),
                   jax.ShapeDtypeStruct((n, D_MODEL), F32),
                   jax.ShapeDtypeStruct((TOP_K, n), jnp.int32),
                   jax.ShapeDtypeStruct((SUBLANES, n), F32)],
        compiler_params=_params())(meta, h1, memkv, g_cross, w_xq, w_xo, g_moe, w_r2t, b_router)


TP_PLAN = 512
LOG2_BM = BM_MOE.bit_length() - 1
assert 1 << LOG2_BM == BM_MOE


def _plan_kernel(n_tiles, idx_ref, dest_ref, meta_ref, tri_ref, acc_ref, run_ref):
    i = pl.program_id(0)
    e, tp, td = N_EXPERTS, TP_PLAN, TC_COMB
    nb_pad = meta_ref.shape[1]
    eio = lax.broadcasted_iota(jnp.int32, (e, tp), 0)
    idx = idx_ref[...]
    onehots = [eio == idx[k:k + 1, :] for k in range(TOP_K)]
    s = jnp.where(onehots[0], 1.0, 0.0)
    for k in range(1, TOP_K):
        s = s + jnp.where(onehots[k], 1.0, 0.0)

    @pl.when(i == 0)
    def _():
        acc_ref[...] = jnp.zeros_like(acc_ref)
        r = lax.broadcasted_iota(jnp.int32, (tp, tp), 0)
        c = lax.broadcasted_iota(jnp.int32, (tp, tp), 1)
        tri_ref[...] = jnp.where(r < c, 1.0, 0.0).astype(BF16)

    @pl.when(i < n_tiles)
    def _():
        acc_ref[...] += s

    @pl.when(i == n_tiles)
    def _():
        counts = jnp.sum(acc_ref[...], axis=1, keepdims=True).astype(jnp.int32)
        cnt = jnp.broadcast_to(counts, (e, LANES))
        pad = lax.shift_left(lax.shift_right_logical(cnt + (BM_MOE - 1), LOG2_BM), LOG2_BM)
        row = lax.broadcasted_iota(jnp.int32, (e, LANES), 0)
        pad_end = pad
        sft = 1
        while sft < e:
            pad_end = pad_end + jnp.where(row >= sft, pltpu.roll(pad_end, sft, 0), 0)
            sft *= 2
        pad_start = pad_end - pad
        run_ref[...] = pad_start.astype(F32)
        bpos = lax.broadcasted_iota(jnp.int32, (e, nb_pad), 1) * BM_MOE
        ebl = lax.broadcasted_iota(jnp.int32, (e, nb_pad), 0)
        done = jnp.where(pad_end[:, 0:1] <= bpos, 1.0, 0.0)
        bexp = jnp.minimum(jnp.sum(done, axis=0, keepdims=True).astype(jnp.int32), e - 1)
        row_end = (pad_start + cnt)[:, 0:1].astype(F32)
        rend_b = jnp.sum(jnp.where(ebl == bexp, row_end, 0.0), axis=0, keepdims=True).astype(jnp.int32)
        nvalid = jnp.clip(rend_b - bpos[0:1, :], 0, BM_MOE)
        total = jnp.sum(jnp.where(ebl == e - 1, pad_end[:, 0:1].astype(F32), 0.0), axis=0, keepdims=True)
        nused = lax.shift_right_logical(total.astype(jnp.int32), LOG2_BM)
        meta_ref[...] = jnp.concatenate(
            [bexp, nvalid, nused, jnp.zeros((SUBLANES - 3, nb_pad), jnp.int32)], axis=0)

    @pl.when(i >= n_tiles)
    def _():
        before = jnp.dot(s.astype(BF16), tri_ref[...], preferred_element_type=F32)
        base = run_ref[:, 0:1] + before
        dk = [jnp.sum(jnp.where(onehots[k], base, 0.0), axis=0, keepdims=True).astype(jnp.int32)
              for k in range(TOP_K)]
        for j in range(tp // td):
            dest_ref[j] = jnp.concatenate([d[:, j * td:(j + 1) * td] for d in dk], axis=1)
        run_ref[...] = run_ref[...] + jnp.sum(s, axis=1, keepdims=True)


def _plan_call(idx, n_blocks):
    n = idx.shape[1]
    tp, td = TP_PLAN, TC_COMB
    n_tiles = n // tp
    nb_pad = -(-n_blocks // LANES) * LANES
    per = tp // td
    return pl.pallas_call(
        functools.partial(_plan_kernel, n_tiles), grid=(2 * n_tiles,), name="route_plan",
        in_specs=[pl.BlockSpec((TOP_K, tp), lambda i: (0, i % n_tiles))],
        out_specs=[pl.BlockSpec((per, 1, TOP_K * td), lambda i: (jnp.maximum(i - n_tiles, 0), 0, 0)),
                   pl.BlockSpec((SUBLANES, nb_pad), lambda i: (0, 0))],
        out_shape=[jax.ShapeDtypeStruct((n // td, 1, TOP_K * td), jnp.int32),
                   jax.ShapeDtypeStruct((SUBLANES, nb_pad), jnp.int32)],
        scratch_shapes=[pltpu.VMEM((tp, tp), BF16), pltpu.VMEM((N_EXPERTS, tp), F32),
                        pltpu.VMEM((N_EXPERTS, LANES), F32)],
        compiler_params=_params())(idx)


ROW_UNROLL = 8


def _dispatch_kernel(n_blocks, nvalid_ref, dst_ref, xs_ref, out_hbm, zero_ref, sem, zsem):
    td = xs_ref.shape[0]
    n_rows = TOP_K * td
    bm = zero_ref.shape[0]

    @pl.when(pl.program_id(0) == 0)
    def _():
        zero_ref[...] = jnp.zeros_like(zero_ref)

        def fill(b):
            return pltpu.make_async_copy(zero_ref, out_hbm.at[pl.ds(pl.multiple_of(b * bm, bm), bm), :], zsem)

        def start(b, carry):
            @pl.when(nvalid_ref[b] < bm)
            def _():
                fill(b).start()
            return carry

        def wait(b, carry):
            @pl.when(nvalid_ref[b] < bm)
            def _():
                fill(b).wait()
            return carry
        lax.fori_loop(0, n_blocks, start, 0)
        lax.fori_loop(0, n_blocks, wait, 0)

    def body(j, carry):
        for u in range(ROW_UNROLL):
            r = j * ROW_UNROLL + u
            t = r & (td - 1)
            pltpu.make_async_copy(xs_ref.at[pl.ds(t, 1), :], out_hbm.at[pl.ds(dst_ref[0, 0, r], 1), :],
                                  sem).start()
        return carry
    lax.fori_loop(0, n_rows // ROW_UNROLL, body, 0)
    pltpu.make_async_copy(out_hbm.at[pl.ds(0, n_rows), :], out_hbm.at[pl.ds(0, n_rows), :], sem).wait()


def _dispatch_call(nvalid, dest, xs, n_blocks):
    n, width = xs.shape
    td, bm = TC_COMB, BM_MOE
    assert td & (td - 1) == 0
    gs = pltpu.PrefetchScalarGridSpec(
        num_scalar_prefetch=1, grid=(n // td,),
        in_specs=[pl.BlockSpec((1, 1, TOP_K * td), lambda i, nv: (i, 0, 0), memory_space=pltpu.SMEM),
                  pl.BlockSpec((td, width), lambda i, nv: (i, 0))],
        out_specs=pl.BlockSpec(memory_space=pl.ANY),
        scratch_shapes=[pltpu.VMEM((bm, width), xs.dtype), pltpu.SemaphoreType.DMA(()),
                        pltpu.SemaphoreType.DMA(())])
    return pl.pallas_call(
        functools.partial(_dispatch_kernel, n_blocks), grid_spec=gs, name="moe_dispatch",
        out_shape=jax.ShapeDtypeStruct((n_blocks * bm, width), xs.dtype),
        compiler_params=_params())(nvalid, dest, xs)


def _row_gather(src_hbm, dst, sem, idx_ref, n_rows):
    def body(j, carry):
        for t in range(ROW_UNROLL):
            r = j * ROW_UNROLL + t
            pltpu.make_async_copy(src_hbm.at[pl.ds(idx_ref[0, 0, r], 1), :], dst.at[pl.ds(r, 1), :], sem).start()
        return carry
    lax.fori_loop(0, n_rows // ROW_UNROLL, body, 0)


def _ffn_kernel(bexp_ref, nused_ref, x_ref, wg_ref, bg_ref, wu_ref, bu_ref, wd_ref, bd_ref,
                y_ref, wgb, wub, wdb):
    b = pl.program_id(0)
    nused = nused_ref[0]

    @pl.when((b == 0) | (bexp_ref[b] != bexp_ref[jnp.maximum(b - 1, 0)]))
    def _():
        wgb[...] = wg_ref[0].astype(BF16)
        wub[...] = wu_ref[0].astype(BF16)
        wdb[...] = wd_ref[0].astype(BF16)

    @pl.when(b < nused)
    def _():
        x = x_ref[...].astype(BF16)

        def proj(w_ref, b_ref):
            return jnp.dot(x, w_ref[...], preferred_element_type=F32) + b_ref[0]

        a = jnp.minimum(proj(wgb, bg_ref), SWIGLU_LIMIT)
        u = jnp.clip(proj(wub, bu_ref), -SWIGLU_LIMIT, SWIGLU_LIMIT)
        hid = (a * jax.nn.sigmoid(SWIGLU_ALPHA * a) * (u + 1.0)).astype(BF16)
        y_ref[...] = jnp.dot(hid, wdb[...], preferred_element_type=F32) + bd_ref[0]

    @pl.when(b >= nused)
    def _():
        y_ref[...] = jnp.zeros_like(y_ref)


def _ffn_call(bexp, nused, xs_sorted, w_gate, b_gate, w_up, b_up, w_down, b_down):
    bm = BM_MOE
    n_blocks = xs_sorted.shape[0] // bm
    wspec = lambda r, c: pl.BlockSpec((1, r, c), lambda b, be, nu: (be[b], 0, 0))
    gs = pltpu.PrefetchScalarGridSpec(
        num_scalar_prefetch=2, grid=(n_blocks,),
        in_specs=[pl.BlockSpec((bm, D_MODEL), lambda b, be, nu: (jnp.minimum(b, nu[0] - 1), 0)),
                  wspec(D_MODEL, D_FF), wspec(1, D_FF), wspec(D_MODEL, D_FF), wspec(1, D_FF),
                  wspec(D_FF, D_MODEL), wspec(1, D_MODEL)],
        out_specs=pl.BlockSpec((bm, D_MODEL), lambda b, be, nu: (b, 0)),
        scratch_shapes=[pltpu.VMEM((D_MODEL, D_FF), BF16), pltpu.VMEM((D_MODEL, D_FF), BF16),
                        pltpu.VMEM((D_FF, D_MODEL), BF16)])
    return pl.pallas_call(
        _ffn_kernel, grid_spec=gs, name="expert_ffn",
        out_shape=jax.ShapeDtypeStruct((n_blocks * bm, D_MODEL), F32),
        compiler_params=_params())(bexp, nused, xs_sorted, w_gate, b_gate, w_up, b_up, w_down, b_down)


def _combine_kernel(nt, dst_ref, dstn_ref, y_hbm, h_ref, gw_ref, gf_ref, out_ref, ybuf, sems):
    i = pl.program_id(0)
    tc = h_ref.shape[0]
    slot = i % 2

    @pl.when(i == 0)
    def _():
        _row_gather(y_hbm, ybuf.at[0], sems.at[0], dst_ref, TOP_K * tc)

    @pl.when(i + 1 < nt)
    def _():
        _row_gather(y_hbm, ybuf.at[1 - slot], sems.at[1 - slot], dstn_ref, TOP_K * tc)

    pltpu.make_async_copy(ybuf.at[slot], ybuf.at[slot], sems.at[slot]).wait()
    gwt = jnp.transpose(gw_ref[...])
    acc = h_ref[...]
    for k in range(TOP_K):
        acc = acc + gwt[:, k:k + 1] * ybuf[slot, k * tc:(k + 1) * tc, :]
    out_ref[...] = _rms(acc, gf_ref[...])


def _combine_call(dest, y, h2, gw, g_final, tile0, n_tiles):
    tc = TC_COMB
    last = tile0 + n_tiles - 1
    return pl.pallas_call(
        functools.partial(_combine_kernel, n_tiles), grid=(n_tiles,), name="moe_combine",
        in_specs=[pl.BlockSpec((1, 1, TOP_K * tc), lambda i: (tile0 + i, 0, 0), memory_space=pltpu.SMEM),
                  pl.BlockSpec((1, 1, TOP_K * tc), lambda i: (jnp.minimum(tile0 + i + 1, last), 0, 0),
                               memory_space=pltpu.SMEM),
                  pl.BlockSpec(memory_space=pl.ANY),
                  pl.BlockSpec((tc, D_MODEL), lambda i: (tile0 + i, 0)),
                  pl.BlockSpec((SUBLANES, tc), lambda i: (0, tile0 + i)),
                  _const_spec((1, D_MODEL))],
        out_specs=pl.BlockSpec((tc, D_MODEL), lambda i: (i, 0)),
        out_shape=jax.ShapeDtypeStruct((n_tiles * tc, D_MODEL), F32),
        scratch_shapes=[pltpu.VMEM((2, TOP_K * tc, D_MODEL), F32), pltpu.SemaphoreType.DMA((2,))],
        compiler_params=_params())(dest, dest, y, h2, gw, g_final)


def _forward(xs, mems, g_mix, w_in, sink, conv_w, w_attn_br, w_conv_br, w_mix_out, g_cross, g_mem,
             w_xq, w_xkv, w_xo, g_moe, w_router, b_router, w_gate, b_gate, w_up, b_up, w_down,
             b_down, g_final):
    groups = [(x.shape[0], x.shape[1]) for x in xs]
    for _, s in groups:
        assert s % max(TM_QKV, TQ_ATTN, TM_MIX, TM_CROSS, TC_COMB) == 0
    assert len(xs) == 2, "two request groups"
    xa, xb = (x.reshape(-1, D_MODEL) for x in xs)
    mem = jnp.concatenate(mems, axis=0)
    n = xa.shape[0] + xb.shape[0]
    tables = _rope_tables(max(s for _, s in groups))
    assert w_in.shape[0] == 1, "single-layer trunk: the final norm is fused into the combine kernel"
    l = 0
    row2 = lambda v: v.reshape(1, -1)
    w_in_b = w_in[l].astype(BF16)
    perm = np.array([(kv * GROUP + g) * HEAD_DIM + d
                     for g in range(GROUP) for kv in range(N_KV_HEADS) for d in range(HEAD_DIM)])
    w_qkv = jnp.concatenate([w_in_b[:, :ATTN_W][:, perm], w_in_b[:, ATTN_W:QKV_W]], axis=1)
    q, kv = _qkv_call(xa, xb, row2(g_mix[l]), w_qkv, tables, groups)
    o = _attn_call(q, kv, sink[l], groups)
    h = _mixer_call(xa, xb, o, row2(g_mix[l]), w_in_b[:, QKV_W:], conv_w[l],
                    w_attn_br[l].astype(BF16)[perm, :], w_conv_br[l].astype(BF16),
                    w_mix_out[l].astype(BF16), groups)
    memkv = _memkv_call(mem, row2(g_mem[l]), w_xkv[l].astype(BF16))
    wr = w_router[l]
    wr_hi = wr.astype(BF16)
    wr_lo = (wr - wr_hi.astype(F32)).astype(BF16)
    w_r2t = jnp.concatenate([wr_hi.T, wr_lo.T], axis=0)
    h2, xs_packed, idx, gw = _cross_call(h, memkv, row2(g_cross[l]), w_xq[l].astype(BF16),
                                         w_xo[l].astype(BF16), row2(g_moe[l]), w_r2t,
                                         b_router[l].reshape(-1, 1), groups)
    n_blocks = -(-TOP_K * n // BM_MOE) + N_EXPERTS
    dest, meta = _plan_call(idx, n_blocks)
    xs_sorted = _dispatch_call(meta[1], dest, xs_packed, n_blocks)
    y = _ffn_call(meta[0], meta[2, 0:1], xs_sorted, w_gate[l], b_gate[l][:, None, :],
                  w_up[l], b_up[l][:, None, :], w_down[l], b_down[l][:, None, :])
    tc = TC_COMB
    outs, t0 = [], 0
    for nb, s in groups:
        nt = nb * s // tc
        outs.append(_combine_call(dest, y, h2, gw, row2(g_final), t0, nt))
        t0 += nt
    return tuple(o.reshape(x.shape) for o, x in zip(outs, xs))


def kernel(x_prompt, x_sample, mem_prompt, mem_sample, g_mix, w_in, sink, conv_w, w_attn_br, w_conv_br,
           w_mix_out, g_cross, g_mem, w_xq, w_xkv, w_xo, g_moe, w_router, b_router, w_gate, b_gate,
           w_up, b_up, w_down, b_down, g_final):
    return _forward([x_prompt, x_sample], [mem_prompt, mem_sample], g_mix, w_in, sink, conv_w,
                    w_attn_br, w_conv_br, w_mix_out, g_cross, g_mem, w_xq, w_xkv, w_xo, g_moe,
                    w_router, b_router, w_gate, b_gate, w_up, b_up, w_down, b_down, g_final)
```

```python
import functools

import numpy as np
import jax
import jax.numpy as jnp
from jax import lax
from jax.experimental import pallas as pl
from jax.experimental.pallas import tpu as pltpu

F32 = jnp.float32
BF16 = jnp.bfloat16

D_MODEL = 1024
N_HEADS = 8
N_KV_HEADS = 2
HEAD_DIM = 64
GROUP = N_HEADS // N_KV_HEADS
ATTN_W = N_HEADS * HEAD_DIM
KV_W = N_KV_HEADS * HEAD_DIM
QKV_W = ATTN_W + 2 * KV_W
WINDOW = 128
ROT_DIM = HEAD_DIM // 4
ROPE_THETA = 500000.0
CONV_W = D_MODEL // 2
REST_W = 3 * CONV_W + 2 * D_MODEL
N_MEM = 256
X_HEADS = 4
X_HEAD_DIM = D_MODEL // X_HEADS
N_EXPERTS = 32
TOP_K = 4
D_FF = D_MODEL
SWIGLU_ALPHA = 1.702
SWIGLU_LIMIT = 7.0
EPS = 1e-5

LANES = 128
SUBLANES = 8
KEY_BLOCK = WINDOW
TM_QKV = 512
TQ_ATTN = 256
TM_MIX = 256
TM_CROSS = 256
BM_MOE = 256
TC_COMB = 256
NEG_BIG = -1e30
VMEM_LIMIT = 56 * 1024 * 1024


def _rms(x, g):
    var = jnp.mean(x * x, axis=-1, keepdims=True)
    return x * lax.rsqrt(var + EPS) * g


def _tile_meta(groups, tile):
    pos, first, last, bidx = [], [], [], []
    b0 = 0
    for nb, s in groups:
        per = s // tile
        for b in range(nb):
            for j in range(per):
                pos.append(j)
                first.append(int(j == 0))
                last.append(int(j == per - 1))
                bidx.append(b0 + b)
        b0 += nb
    return jnp.asarray(np.array([pos, first, last, bidx], dtype=np.int32))


def _const_spec(shape):
    nd = len(shape)
    return pl.BlockSpec(shape, lambda *_: (0,) * nd)


def _params(vmem=VMEM_LIMIT):
    return pltpu.CompilerParams(dimension_semantics=("arbitrary",), vmem_limit_bytes=vmem)


def _qkv_kernel(na, meta_ref, xa_ref, xb_ref, g_ref, w_ref, cos_ref, sa_ref, sb_ref, q_ref, kv_ref):
    del meta_ref
    x = jnp.where(pl.program_id(0) < na, xa_ref[...], xb_ref[...])
    xn = _rms(x, g_ref[...]).astype(BF16)
    proj = jnp.dot(xn, w_ref[...], preferred_element_type=F32)
    c, sa, sb = cos_ref[...], sa_ref[...], sb_ref[...]
    n_rot = (ATTN_W + KV_W) // LANES
    for gi in range(n_rot):
        p = proj[:, gi * LANES:(gi + 1) * LANES]
        r = p * c + pltpu.roll(p, LANES - ROT_DIM // 2, 1) * sa + pltpu.roll(p, ROT_DIM // 2, 1) * sb
        if gi < ATTN_W // LANES:
            q_ref[:, gi * LANES:(gi + 1) * LANES] = (r * (HEAD_DIM ** -0.5)).astype(BF16)
        else:
            kv_ref[:, 0:KV_W] = r.astype(BF16)
    kv_ref[:, KV_W:2 * KV_W] = proj[:, ATTN_W + KV_W:QKV_W].astype(BF16)


def _rope_tables(s_max):
    half = ROT_DIM // 2
    inv_freq = ROPE_THETA ** (-(jnp.arange(half, dtype=F32) * 2.0) / ROT_DIM)
    ang = jnp.arange(s_max, dtype=F32)[:, None] * inv_freq[None, :]
    cos, sin = jnp.cos(ang), jnp.sin(ang)
    d = np.arange(LANES) % HEAD_DIM
    j = d % half
    cos_l, sin_l = cos[:, j], sin[:, j]
    rot = jnp.asarray(d < ROT_DIM)[None, :]
    lo = jnp.asarray(d < half)[None, :]
    hi = jnp.asarray((d >= half) & (d < ROT_DIM))[None, :]
    c = jnp.where(rot, cos_l, 1.0)
    sa = jnp.where(lo, -sin_l, 0.0)
    sb = jnp.where(hi, sin_l, 0.0)
    return c, sa, sb


def _qkv_call(xa, xb, g_mix, w_qkv, tables, groups):
    tm = TM_QKV
    na = xa.shape[0] // tm
    n = xa.shape[0] + xb.shape[0]
    meta = _tile_meta(groups, tm)
    row = lambda i, m: (i, 0)
    tab = lambda i, m: (m[0, i], 0)
    gs = pltpu.PrefetchScalarGridSpec(
        num_scalar_prefetch=1, grid=(n // tm,),
        in_specs=[pl.BlockSpec((tm, D_MODEL), lambda i, m: (jnp.minimum(i, na - 1), 0)),
                  pl.BlockSpec((tm, D_MODEL), lambda i, m: (jnp.maximum(i - na, 0), 0)),
                  _const_spec((1, D_MODEL)),
                  _const_spec((D_MODEL, QKV_W)),
                  pl.BlockSpec((tm, LANES), tab), pl.BlockSpec((tm, LANES), tab),
                  pl.BlockSpec((tm, LANES), tab)],
        out_specs=[pl.BlockSpec((tm, ATTN_W), row), pl.BlockSpec((tm, 2 * KV_W), row)])
    return pl.pallas_call(
        functools.partial(_qkv_kernel, na), grid_spec=gs, name="qkv_rope",
        out_shape=[jax.ShapeDtypeStruct((n, ATTN_W), BF16), jax.ShapeDtypeStruct((n, 2 * KV_W), BF16)],
        compiler_params=_params())(meta, xa, xb, g_mix, w_qkv, *tables)


def _attn_kernel(meta_ref, sink_ref, q_ref, kvp_ref, kvm_ref, kvn_ref, o_ref, kcat_ref):
    i = pl.program_id(0)
    kb = KEY_BLOCK
    tq = q_ref.shape[0]
    kcat_ref[0:kb, :] = kvp_ref[...]
    kcat_ref[kb:kb + tq, :] = kvm_ref[...]
    kcat_ref[kb + tq:kb + tq + kb, :] = kvn_ref[...]
    r = lax.broadcasted_iota(jnp.int32, (kb, 3 * kb), 0)
    c = lax.broadcasted_iota(jnp.int32, (kb, 3 * kb), 1)
    dlt = c - r
    klane = lax.broadcasted_iota(jnp.int32, (3 * kb, KV_W), 1)
    olane = lax.broadcasted_iota(jnp.int32, (GROUP * kb, KV_W), 1)
    nsub = tq // kb
    for s in range(nsub):
        first = meta_ref[1, i * nsub + s]
        last = meta_ref[2, i * nsub + s]
        lo = jnp.where(first == 1, kb, 0)
        hi = jnp.where(last == 1, 2 * kb, 3 * kb)
        valid = (dlt >= 0) & (dlt <= 2 * WINDOW) & (c >= lo) & (c < hi)
        bias = jnp.where(valid, 0.0, NEG_BIG)
        bias = jnp.concatenate([bias] * GROUP, axis=0)
        kw = kcat_ref[s * kb:(s + 3) * kb, :]
        kc = kw[:, 0:KV_W]
        vc = jnp.concatenate([kw[:, KV_W:2 * KV_W], jnp.ones((3 * kb, KV_W), BF16)], axis=1)
        q4 = jnp.concatenate([q_ref[s * kb:(s + 1) * kb, g * KV_W:(g + 1) * KV_W] for g in range(GROUP)],
                             axis=0)
        res = []
        for kh in range(N_KV_HEADS):
            own = (klane >= kh * HEAD_DIM) & (klane < (kh + 1) * HEAD_DIM)
            kk = jnp.where(own, kc, jnp.zeros_like(kc))
            sc = lax.dot_general(q4, kk, (((1,), (1,)), ((), ())), preferred_element_type=F32) + bias
            snk = jnp.concatenate([jnp.full((kb, 1), sink_ref[kh * GROUP + g], F32) for g in range(GROUP)],
                                  axis=0)
            m = jnp.maximum(jnp.max(sc, axis=-1, keepdims=True), snk)
            p = jnp.exp(sc - m).astype(BF16)
            pv = jnp.dot(p, vc, preferred_element_type=F32)
            denom = pv[:, KV_W:KV_W + 1] + jnp.exp(snk - m)
            res.append(pv[:, 0:KV_W] / denom)
        out = res[0]
        for kh in range(1, N_KV_HEADS):
            out = jnp.where(olane >= kh * HEAD_DIM, res[kh], out)
        out = out.astype(BF16)
        for g in range(GROUP):
            o_ref[s * kb:(s + 1) * kb, g * KV_W:(g + 1) * KV_W] = out[g * kb:(g + 1) * kb, :]


def _attn_call(q, kv, sink, groups):
    n = q.shape[0]
    tq, kb = TQ_ATTN, KEY_BLOCK
    per = tq // kb
    nkb = n // kb
    meta = _tile_meta(groups, kb)
    gs = pltpu.PrefetchScalarGridSpec(
        num_scalar_prefetch=1, grid=(n // tq,),
        in_specs=[pl.BlockSpec(memory_space=pltpu.SMEM),
                  pl.BlockSpec((tq, ATTN_W), lambda i, m: (i, 0)),
                  pl.BlockSpec((kb, 2 * KV_W), lambda i, m: (jnp.maximum(i * per - 1, 0), 0)),
                  pl.BlockSpec((tq, 2 * KV_W), lambda i, m: (i, 0)),
                  pl.BlockSpec((kb, 2 * KV_W), lambda i, m: (jnp.minimum((i + 1) * per, nkb - 1), 0))],
        out_specs=pl.BlockSpec((tq, ATTN_W), lambda i, m: (i, 0)),
        scratch_shapes=[pltpu.VMEM((tq + 2 * kb, 2 * KV_W), BF16)])
    return pl.pallas_call(
        _attn_kernel, grid_spec=gs, name="window_attn",
        out_shape=jax.ShapeDtypeStruct((n, ATTN_W), BF16),
        compiler_params=_params())(meta, sink, q, kv, kv, kv)


def _mixer_kernel(na, meta_ref, xa_ref, xpa_ref, xna_ref, xb_ref, xpb_ref, xnb_ref, o_ref, g_ref, w_ref,
                  cw_ref, wa_ref, wc_ref, wm_ref, h_ref):
    i = pl.program_id(0)
    tm = xa_ref.shape[0]
    g = g_ref[...]
    in_a = i < na
    x = jnp.where(in_a, xa_ref[...], xb_ref[...])
    xb = _rms(x, g).astype(BF16)
    proj = jnp.dot(xb, w_ref[...], preferred_element_type=F32)
    cb = proj[:, 0:CONV_W]
    u = proj[:, CONV_W:2 * CONV_W] * proj[:, 2 * CONV_W:3 * CONV_W]
    xh = jnp.concatenate([jnp.where(in_a, xpa_ref[...], xpb_ref[...]),
                          jnp.where(in_a, xna_ref[...], xnb_ref[...])], axis=0)
    xhb = _rms(xh, g).astype(BF16)
    ph = jnp.dot(xhb, w_ref[:, CONV_W:3 * CONV_W], preferred_element_type=F32)
    uh = ph[:, 0:CONV_W] * ph[:, CONV_W:2 * CONV_W]
    first = meta_ref[1, i]
    last = meta_ref[2, i]
    u_prev = jnp.where(first == 1, 0.0, uh[SUBLANES - 1:SUBLANES, :])
    u_next = jnp.where(last == 1, 0.0, uh[SUBLANES:SUBLANES + 1, :])
    row = lax.broadcasted_iota(jnp.int32, (tm, 1), 0)
    up = jnp.where(row == 0, u_prev, pltpu.roll(u, 1, 0))
    dn = jnp.where(row == tm - 1, u_next, pltpu.roll(u, tm - 1, 0))
    cw = cw_ref[...]
    y = up * cw[0:1, :] + u * cw[1:2, :] + dn * cw[2:3, :]
    conv = (cb * y).astype(BF16)
    conv_br = jnp.dot(conv, wc_ref[...], preferred_element_type=F32)
    attn_br = jnp.dot(o_ref[...], wa_ref[...], preferred_element_type=F32)
    g0 = jax.nn.sigmoid(proj[:, 3 * CONV_W:3 * CONV_W + D_MODEL])
    g1 = jax.nn.sigmoid(proj[:, 3 * CONV_W + D_MODEL:REST_W])
    merged = (g0 * attn_br + g1 * conv_br).astype(BF16)
    h_ref[...] = x + jnp.dot(merged, wm_ref[...], preferred_element_type=F32)


def _mixer_call(xa, xb, o, g_mix, w_rest, conv_w, w_attn_br, w_conv_br, w_mix_out, groups):
    tm = TM_MIX
    per = tm // SUBLANES
    na, nb = xa.shape[0] // tm, xb.shape[0] // tm
    n = xa.shape[0] + xb.shape[0]
    meta = _tile_meta(groups, tm)

    def x_specs(tile_of, n_tiles):
        last8 = n_tiles * per - 1
        return [pl.BlockSpec((tm, D_MODEL), lambda i, m: (jnp.clip(tile_of(i), 0, n_tiles - 1), 0)),
                pl.BlockSpec((SUBLANES, D_MODEL), lambda i, m: (jnp.clip(tile_of(i) * per - 1, 0, last8), 0)),
                pl.BlockSpec((SUBLANES, D_MODEL), lambda i, m: (jnp.clip((tile_of(i) + 1) * per, 0, last8), 0))]

    gs = pltpu.PrefetchScalarGridSpec(
        num_scalar_prefetch=1, grid=(n // tm,),
        in_specs=x_specs(lambda i: i, na) + x_specs(lambda i: i - na, nb) + [
                  pl.BlockSpec((tm, ATTN_W), lambda i, m: (i, 0)),
                  _const_spec((1, D_MODEL)),
                  _const_spec((D_MODEL, REST_W)),
                  _const_spec((3, CONV_W)),
                  _const_spec((ATTN_W, D_MODEL)),
                  _const_spec((CONV_W, D_MODEL)),
                  _const_spec((D_MODEL, D_MODEL))],
        out_specs=pl.BlockSpec((tm, D_MODEL), lambda i, m: (i, 0)))
    return pl.pallas_call(
        functools.partial(_mixer_kernel, na), grid_spec=gs, name="mixer",
        out_shape=jax.ShapeDtypeStruct((n, D_MODEL), F32),
        compiler_params=_params())(meta, xa, xa, xa, xb, xb, xb, o, g_mix, w_rest, conv_w, w_attn_br,
                                   w_conv_br, w_mix_out)


def _memkv_kernel(mem_ref, g_ref, w_ref, kv_ref):
    mn = _rms(mem_ref[0], g_ref[...]).astype(BF16)
    kv_ref[0] = jnp.dot(mn, w_ref[...], preferred_element_type=F32).astype(BF16)


def _memkv_call(mem, g_mem, w_xkv):
    nb = mem.shape[0]
    return pl.pallas_call(
        _memkv_kernel, grid=(nb,), name="mem_kv",
        in_specs=[pl.BlockSpec((1, N_MEM, D_MODEL), lambda b: (b, 0, 0)),
                  _const_spec((1, D_MODEL)),
                  _const_spec((D_MODEL, 2 * D_MODEL))],
        out_specs=pl.BlockSpec((1, N_MEM, 2 * D_MODEL), lambda b: (b, 0, 0)),
        out_shape=jax.ShapeDtypeStruct((nb, N_MEM, 2 * D_MODEL), BF16),
        compiler_params=_params())(mem, g_mem, w_xkv)


def _cross_kernel(meta_ref, h_ref, kv_ref, gc_ref, wq_ref, wo_ref, gm_ref, wr_ref, br_ref,
                  h2_ref, xs_ref, idx_ref):
    del meta_ref
    tm = h_ref.shape[0]
    h = h_ref[...]
    hn = _rms(h, gc_ref[...]).astype(BF16)
    q = (jnp.dot(hn, wq_ref[...], preferred_element_type=F32) * (X_HEAD_DIM ** -0.5)).astype(BF16)
    outs = []
    for hd in range(X_HEADS):
        qh = q[:, hd * X_HEAD_DIM:(hd + 1) * X_HEAD_DIM]
        kh = kv_ref[0, :, hd * X_HEAD_DIM:(hd + 1) * X_HEAD_DIM]
        vh = kv_ref[0, :, D_MODEL + hd * X_HEAD_DIM:D_MODEL + (hd + 1) * X_HEAD_DIM]
        s = lax.dot_general(qh, kh, (((1,), (1,)), ((), ())), preferred_element_type=F32)
        m = jnp.max(s, axis=-1, keepdims=True)
        p = jnp.exp(s - m)
        p = (p / jnp.sum(p, axis=-1, keepdims=True)).astype(BF16)
        outs.append(jnp.dot(p, vh, preferred_element_type=F32).astype(BF16))
    o = jnp.concatenate(outs, axis=1)
    h2 = h + jnp.dot(o, wo_ref[...], preferred_element_type=F32)
    h2_ref[...] = h2

    hn3 = _rms(h2, gm_ref[...])
    hi = hn3.astype(BF16)
    hi32 = hi.astype(F32)
    lo = (hn3 - hi32).astype(BF16)
    xs_ref[:, 0:D_MODEL] = hn3

    nt = (((1,), (1,)), ((), ()))
    r1 = lax.dot_general(wr_ref[...], hi, nt, preferred_element_type=F32)
    r2 = lax.dot_general(wr_ref[...], lo, nt, preferred_element_type=F32)
    e = N_EXPERTS
    logits = ((r2[e:2 * e] + r2[0:e]) + r1[e:2 * e]) + r1[0:e] + br_ref[...]
    eio = lax.broadcasted_iota(jnp.int32, (e, tm), 0)
    cur = logits
    vals, sels = [], []
    for _ in range(TOP_K):
        mx = jnp.max(cur, axis=0, keepdims=True)
        sel = jnp.min(jnp.where(cur == mx, eio, e), axis=0, keepdims=True)
        vals.append(mx)
        sels.append(sel)
        cur = jnp.where(eio == sel, -jnp.inf, cur)
    ex = [jnp.exp(v - vals[0]) for v in vals]
    tot = ex[0] + ex[1] + ex[2] + ex[3]
    idx_ref[...] = jnp.concatenate(sels, axis=0)
    gws = [x / tot for x in ex]
    w_hi = [g.astype(BF16).astype(F32) for g in gws]
    w_lo = [(g - h).astype(BF16).astype(F32) for g, h in zip(gws, w_hi)]
    wt = jnp.transpose(jnp.concatenate(w_hi + w_lo, axis=0))
    et = jnp.transpose(jnp.concatenate([s.astype(F32) for s in sels]
                                       + [jnp.zeros((SUBLANES - TOP_K, tm), F32)], axis=0))
    xs_ref[:, D_MODEL:XS_W] = jnp.concatenate(
        [wt, et, jnp.zeros((tm, LANES - 2 * SUBLANES), F32)], axis=1)


def _cross_call(h1, memkv, g_cross, w_xq, w_xo, g_moe, w_r2t, b_router, groups):
    n = h1.shape[0]
    tm = TM_CROSS
    meta = _tile_meta(groups, tm)
    row = lambda i, m: (i, 0)
    col = lambda i, m: (0, i)
    gs = pltpu.PrefetchScalarGridSpec(
        num_scalar_prefetch=1, grid=(n // tm,),
        in_specs=[pl.BlockSpec((tm, D_MODEL), row),
                  pl.BlockSpec((1, N_MEM, 2 * D_MODEL), lambda i, m: (m[3, i], 0, 0)),
                  _const_spec((1, D_MODEL)),
                  _const_spec((D_MODEL, D_MODEL)),
                  _const_spec((D_MODEL, D_MODEL)),
                  _const_spec((1, D_MODEL)),
                  _const_spec((2 * N_EXPERTS, D_MODEL)),
                  _const_spec((N_EXPERTS, 1))],
        out_specs=[pl.BlockSpec((tm, D_MODEL), row),
                   pl.BlockSpec((tm, XS_W), row),
                   pl.BlockSpec((TOP_K, tm), col)])
    return pl.pallas_call(
        _cross_kernel, grid_spec=gs, name="cross_router",
        out_shape=[jax.ShapeDtypeStruct((n, D_MODEL), F32),
                   jax.ShapeDtypeStruct((n, XS_W), F32),
                   jax.ShapeDtypeStruct((TOP_K, n), jnp.int32)],
        compiler_params=_params())(meta, h1, memkv, g_cross, w_xq, w_xo, g_moe, w_r2t, b_router)


LOG2_BM = BM_MOE.bit_length() - 1
assert 1 << LOG2_BM == BM_MOE
SEG_ALIGN = SUBLANES
LOG2_SEG = SEG_ALIGN.bit_length() - 1
PIECE_ROWS = 2 * SEG_ALIGN
K_ROWS = 1280
M_CHUNK = 640
MAX_FULL = K_ROWS // PIECE_ROWS
assert 1 << LOG2_SEG == SEG_ALIGN and K_ROWS % M_CHUNK == 0 and MAX_FULL <= LANES
assert K_ROWS >= TOP_K * TC_COMB + N_EXPERTS * (SEG_ALIGN - 1)
DESC_FULL_DST, DESC_FULL_SRC, DESC_REM_DST, DESC_REM_SRC, DESC_COUNT = range(5)
AUX_W_HI, AUX_W_LO, AUX_EID = 0, TOP_K, 2 * TOP_K
XS_W = D_MODEL + LANES


def _sublane_scan(x, n):
    row = lax.broadcasted_iota(jnp.int32, x.shape, 0)
    sft = 1
    while sft < n:
        x = x + jnp.where(row >= sft, pltpu.roll(x, sft, 0), 0)
        sft *= 2
    return x


def _owner(ends, lane):
    ebl = lax.broadcasted_iota(jnp.int32, ends.shape, 0)
    owner = jnp.sum(jnp.where(ends <= lane, 1.0, 0.0), axis=0, keepdims=True).astype(jnp.int32)
    return ebl == owner


def _pick(sel, val):
    return jnp.sum(jnp.where(sel, val.astype(F32), 0.0), axis=0, keepdims=True).astype(jnp.int32)


def _plan_kernel(n_tiles, idx_ref, kpos_ref, desc_ref, meta_ref, tri_ref, acc_ref, run_ref):
    i = pl.program_id(0)
    e, tp = N_EXPERTS, TC_COMB
    nb_pad = meta_ref.shape[1]
    eio = lax.broadcasted_iota(jnp.int32, (e, tp), 0)
    idx = idx_ref[...]
    onehots = [eio == idx[k:k + 1, :] for k in range(TOP_K)]
    s = jnp.where(onehots[0], 1.0, 0.0)
    for k in range(1, TOP_K):
        s = s + jnp.where(onehots[k], 1.0, 0.0)
    cnt_t = jnp.sum(s, axis=1, keepdims=True).astype(jnp.int32)
    cnt = jnp.broadcast_to(cnt_t, (e, LANES))
    seg = lax.shift_left(lax.shift_right_logical(cnt + (SEG_ALIGN - 1), LOG2_SEG), LOG2_SEG)

    @pl.when(i == 0)
    def _():
        acc_ref[...] = jnp.zeros_like(acc_ref)
        r = lax.broadcasted_iota(jnp.int32, (tp, tp), 0)
        c = lax.broadcasted_iota(jnp.int32, (tp, tp), 1)
        tri_ref[...] = jnp.where(r < c, 1.0, 0.0).astype(BF16)

    @pl.when(i < n_tiles)
    def _():
        acc_ref[...] += seg

    @pl.when(i == n_tiles)
    def _():
        tot = acc_ref[...]
        pad = lax.shift_left(lax.shift_right_logical(tot + (BM_MOE - 1), LOG2_BM), LOG2_BM)
        pad_end = _sublane_scan(pad, e)
        pad_start = pad_end - pad
        run_ref[...] = pad_start
        bpos = lax.broadcasted_iota(jnp.int32, (e, nb_pad), 1) * BM_MOE
        ebl = lax.broadcasted_iota(jnp.int32, (e, nb_pad), 0)
        done = jnp.where(pad_end[:, 0:1] <= bpos, 1.0, 0.0)
        bexp = jnp.minimum(jnp.sum(done, axis=0, keepdims=True).astype(jnp.int32), e - 1)
        row_end = (pad_start + tot)[:, 0:1].astype(F32)
        rend_b = jnp.sum(jnp.where(ebl == bexp, row_end, 0.0), axis=0, keepdims=True).astype(jnp.int32)
        nvalid = jnp.clip(rend_b - bpos[0:1, :], 0, BM_MOE)
        total = jnp.sum(jnp.where(ebl == e - 1, pad_end[:, 0:1].astype(F32), 0.0), axis=0, keepdims=True)
        nused = lax.shift_right_logical(total.astype(jnp.int32), LOG2_BM)
        meta_ref[...] = jnp.concatenate(
            [bexp, nvalid, nused, jnp.zeros((SUBLANES - 3, nb_pad), jnp.int32)], axis=0)

    @pl.when(i >= n_tiles)
    def _():
        before = jnp.dot(s.astype(BF16), tri_ref[...], preferred_element_type=F32)
        kbase = _sublane_scan(seg, e) - seg
        pos = kbase[:, 0:1].astype(F32) + before
        kp = [jnp.sum(jnp.where(onehots[k], pos, 0.0), axis=0, keepdims=True).astype(jnp.int32)
              for k in range(TOP_K)]
        kpos_ref[0] = jnp.concatenate(kp + [jnp.zeros((SUBLANES - TOP_K, tp), jnp.int32)], axis=0)

        start = run_ref[...]
        lane = lax.broadcasted_iota(jnp.int32, (e, LANES), 1)
        lane1 = lane[0:1, :]
        units = lax.shift_right_logical(seg, LOG2_SEG)
        nfull = lax.shift_right_logical(units, 1)
        fend = _sublane_scan(nfull, e)
        sel = _owner(fend, lane)
        foff = (lane1 - _pick(sel, fend - nfull)) * PIECE_ROWS
        n_full = fend[e - 1:e, :]
        full_ok = lane1 < n_full
        full_dst = jnp.where(full_ok, _pick(sel, start) + foff, 0)
        full_src = jnp.where(full_ok, _pick(sel, kbase) + foff, 0)
        pend = _sublane_scan(units & 1, e)
        sel = _owner(pend, lane)
        n_rem = pend[e - 1:e, :]
        rem_ok = lane1 < n_rem
        rem_dst = jnp.where(rem_ok, _pick(sel, start + nfull * PIECE_ROWS), 0)
        rem_src = jnp.where(rem_ok, _pick(sel, kbase + nfull * PIECE_ROWS), 0)
        counts = jnp.where(lane1 == 0, n_full, jnp.where(lane1 == 1, n_rem, 0))
        desc_ref[0] = jnp.concatenate(
            [full_dst, full_src, rem_dst, rem_src, counts, jnp.zeros((SUBLANES - 5, LANES), jnp.int32)], axis=0)
        run_ref[...] = run_ref[...] + seg


def _plan_call(idx, n_blocks):
    n = idx.shape[1]
    tp = TC_COMB
    n_tiles = n // tp
    nb_pad = -(-n_blocks // LANES) * LANES
    tile = lambda i: (jnp.maximum(i - n_tiles, 0), 0, 0)
    return pl.pallas_call(
        functools.partial(_plan_kernel, n_tiles), grid=(2 * n_tiles,), name="route_plan",
        in_specs=[pl.BlockSpec((TOP_K, tp), lambda i: (0, i % n_tiles))],
        out_specs=[pl.BlockSpec((1, SUBLANES, tp), tile),
                   pl.BlockSpec((1, SUBLANES, LANES), tile),
                   pl.BlockSpec((SUBLANES, nb_pad), lambda i: (0, 0))],
        out_shape=[jax.ShapeDtypeStruct((n_tiles, SUBLANES, tp), jnp.int32),
                   jax.ShapeDtypeStruct((n_tiles, SUBLANES, LANES), jnp.int32),
                   jax.ShapeDtypeStruct((SUBLANES, nb_pad), jnp.int32)],
        scratch_shapes=[pltpu.VMEM((tp, tp), BF16), pltpu.VMEM((N_EXPERTS, LANES), jnp.int32),
                        pltpu.VMEM((N_EXPERTS, LANES), jnp.int32)],
        compiler_params=_params())(idx)


def _slot_hits(kp, iota):
    hit = jnp.where(iota == kp[0], 1.0, 0.0)
    for k in range(1, TOP_K):
        hit = hit + jnp.where(iota == kp[k], 1.0, 0.0)
    return hit


def _for_each_copy(n_full, n_rem, fn):
    def full(j, carry):
        fn(DESC_FULL_DST, j, PIECE_ROWS)
        return carry

    def rem(j, carry):
        fn(DESC_REM_DST, j, SEG_ALIGN)
        return carry
    lax.fori_loop(0, n_full, full, 0)
    lax.fori_loop(0, n_rem, rem, 0)


def _dispatch_kernel(n_tiles, n_blocks, nvalid_ref, desc_ref, kpos_ref, xs_ref, out_hbm, zero_ref, buf,
                     sems, zsem, issued_ref):
    i = pl.program_id(0)
    td = xs_ref.shape[0]
    bm = zero_ref.shape[0]
    cur = i % 2

    def copy(b, dst, src, rows):
        return pltpu.make_async_copy(buf.at[b, pl.ds(pl.multiple_of(src, SEG_ALIGN), rows), :],
                                     out_hbm.at[pl.ds(pl.multiple_of(dst, SEG_ALIGN), rows), :], sems.at[b])

    def drain(b):
        _for_each_copy(issued_ref[b, 0], issued_ref[b, 1], lambda row, j, rows: copy(b, 0, 0, rows).wait())

    @pl.when(i == 0)
    def _():
        zero_ref[...] = jnp.zeros_like(zero_ref)

        def fill(b):
            return pltpu.make_async_copy(zero_ref, out_hbm.at[pl.ds(pl.multiple_of(b * bm, bm), bm), :], zsem)

        def start(b, carry):
            @pl.when(nvalid_ref[b] < bm)
            def _():
                fill(b).start()
            return carry

        def wait(b, carry):
            @pl.when(nvalid_ref[b] < bm)
            def _():
                fill(b).wait()
            return carry
        lax.fori_loop(0, n_blocks, start, 0)
        lax.fori_loop(0, n_blocks, wait, 0)

    @pl.when(i >= 2)
    def _():
        drain(cur)

    kp = kpos_ref[0]
    x = xs_ref[...].astype(BF16)
    for c in range(K_ROWS // M_CHUNK):
        rows = lax.broadcasted_iota(jnp.int32, (M_CHUNK, td), 0) + c * M_CHUNK
        sel = _slot_hits([kp[k:k + 1, :] for k in range(TOP_K)], rows).astype(BF16)
        buf[cur, c * M_CHUNK:(c + 1) * M_CHUNK, :] = jnp.dot(sel, x, preferred_element_type=F32)

    n_full, n_rem = desc_ref[0, DESC_COUNT, 0], desc_ref[0, DESC_COUNT, 1]
    _for_each_copy(n_full, n_rem,
                   lambda row, j, rows: copy(cur, desc_ref[0, row, j], desc_ref[0, row + 1, j], rows).start())
    issued_ref[cur, 0] = n_full
    issued_ref[cur, 1] = n_rem

    @pl.when(i == n_tiles - 1)
    def _():
        drain(cur)

        @pl.when(i >= 1)
        def _():
            drain(1 - cur)


def _dispatch_call(nvalid, desc, kpos, xs, n_blocks):
    n, width = xs.shape
    td, bm = TC_COMB, BM_MOE
    n_tiles = n // td
    gs = pltpu.PrefetchScalarGridSpec(
        num_scalar_prefetch=1, grid=(n_tiles,),
        in_specs=[pl.BlockSpec((1, SUBLANES, LANES), lambda i, nv: (i, 0, 0), memory_space=pltpu.SMEM),
                  pl.BlockSpec((1, SUBLANES, td), lambda i, nv: (i, 0, 0)),
                  pl.BlockSpec((td, width), lambda i, nv: (i, 0))],
        out_specs=pl.BlockSpec(memory_space=pl.ANY),
        scratch_shapes=[pltpu.VMEM((bm, width), xs.dtype), pltpu.VMEM((2, K_ROWS, width), xs.dtype),
                        pltpu.SemaphoreType.DMA((2,)), pltpu.SemaphoreType.DMA(()),
                        pltpu.SMEM((2, 2), jnp.int32)])
    return pl.pallas_call(
        functools.partial(_dispatch_kernel, n_tiles, n_blocks), grid_spec=gs, name="moe_dispatch",
        out_shape=jax.ShapeDtypeStruct((n_blocks * bm, width), xs.dtype),
        compiler_params=_params())(nvalid, desc, kpos, xs)


def _ffn_kernel(bexp_ref, nused_ref, x_ref, wg_ref, bg_ref, wu_ref, bu_ref, wd_ref, bd_ref,
                y_ref, wgb, wub, wdb):
    b = pl.program_id(0)
    nused = nused_ref[0]

    @pl.when((b == 0) | (bexp_ref[b] != bexp_ref[jnp.maximum(b - 1, 0)]))
    def _():
        wgb[...] = wg_ref[0].astype(BF16)
        wub[...] = wu_ref[0].astype(BF16)
        wdb[...] = wd_ref[0].astype(BF16)

    @pl.when(b < nused)
    def _():
        x = x_ref[:, 0:D_MODEL].astype(BF16)
        aux = x_ref[:, D_MODEL:XS_W]
        eid = bexp_ref[b].astype(F32)
        row_w = jnp.zeros((x.shape[0], 1), F32)
        for k in range(TOP_K):
            w_k = aux[:, AUX_W_HI + k:AUX_W_HI + k + 1] + aux[:, AUX_W_LO + k:AUX_W_LO + k + 1]
            row_w = row_w + jnp.where(aux[:, AUX_EID + k:AUX_EID + k + 1] == eid, w_k, 0.0)

        def proj(w_ref, b_ref):
            return jnp.dot(x, w_ref[...], preferred_element_type=F32) + b_ref[0]

        a = jnp.minimum(proj(wgb, bg_ref), SWIGLU_LIMIT)
        u = jnp.clip(proj(wub, bu_ref), -SWIGLU_LIMIT, SWIGLU_LIMIT)
        hid = (a * jax.nn.sigmoid(SWIGLU_ALPHA * a) * (u + 1.0)).astype(BF16)
        y_ref[...] = (jnp.dot(hid, wdb[...], preferred_element_type=F32) + bd_ref[0]) * row_w

    @pl.when(b >= nused)
    def _():
        y_ref[...] = jnp.zeros_like(y_ref)


def _ffn_call(bexp, nused, xs_sorted, w_gate, b_gate, w_up, b_up, w_down, b_down):
    bm = BM_MOE
    n_blocks = xs_sorted.shape[0] // bm
    wspec = lambda r, c: pl.BlockSpec((1, r, c), lambda b, be, nu: (be[b], 0, 0))
    gs = pltpu.PrefetchScalarGridSpec(
        num_scalar_prefetch=2, grid=(n_blocks,),
        in_specs=[pl.BlockSpec((bm, XS_W), lambda b, be, nu: (jnp.minimum(b, nu[0] - 1), 0)),
                  wspec(D_MODEL, D_FF), wspec(1, D_FF), wspec(D_MODEL, D_FF), wspec(1, D_FF),
                  wspec(D_FF, D_MODEL), wspec(1, D_MODEL)],
        out_specs=pl.BlockSpec((bm, D_MODEL), lambda b, be, nu: (b, 0)),
        scratch_shapes=[pltpu.VMEM((D_MODEL, D_FF), BF16), pltpu.VMEM((D_MODEL, D_FF), BF16),
                        pltpu.VMEM((D_FF, D_MODEL), BF16)])
    return pl.pallas_call(
        _ffn_kernel, grid_spec=gs, name="expert_ffn",
        out_shape=jax.ShapeDtypeStruct((n_blocks * bm, D_MODEL), F32),
        compiler_params=_params())(bexp, nused, xs_sorted, w_gate, b_gate, w_up, b_up, w_down, b_down)


def _combine_kernel(nt, desc_ref, descn_ref, kpos_ref, y_hbm, h_ref, gf_ref, out_ref, ybuf, sems):
    i = pl.program_id(0)
    tc = h_ref.shape[0]
    cur = i % 2

    def copy(b, dst, src, rows):
        return pltpu.make_async_copy(y_hbm.at[pl.ds(pl.multiple_of(dst, SEG_ALIGN), rows), :],
                                     ybuf.at[b, pl.ds(pl.multiple_of(src, SEG_ALIGN), rows), :], sems.at[b])

    def fetch(b, ref):
        _for_each_copy(ref[0, DESC_COUNT, 0], ref[0, DESC_COUNT, 1],
                       lambda row, j, rows: copy(b, ref[0, row, j], ref[0, row + 1, j], rows).start())

    @pl.when(i == 0)
    def _():
        ybuf[...] = jnp.zeros_like(ybuf)
        fetch(0, desc_ref)

    @pl.when(i + 1 < nt)
    def _():
        fetch(1 - cur, descn_ref)

    _for_each_copy(desc_ref[0, DESC_COUNT, 0], desc_ref[0, DESC_COUNT, 1],
                   lambda row, j, rows: copy(cur, 0, 0, rows).wait())

    kpt = jnp.transpose(kpos_ref[0].astype(F32))
    kp = [kpt[:, k:k + 1] for k in range(TOP_K)]
    acc = h_ref[...]
    for c in range(K_ROWS // M_CHUNK):
        cols = (lax.broadcasted_iota(jnp.int32, (tc, M_CHUNK), 1) + c * M_CHUNK).astype(F32)
        pick = _slot_hits(kp, cols).astype(BF16)
        y = ybuf[cur, c * M_CHUNK:(c + 1) * M_CHUNK, :]
        y_hi = y.astype(BF16)
        y_lo = (y - y_hi.astype(F32)).astype(BF16)
        acc = acc + jnp.dot(pick, y_hi, preferred_element_type=F32) + jnp.dot(pick, y_lo, preferred_element_type=F32)
    out_ref[...] = _rms(acc, gf_ref[...])


def _combine_call(desc, kpos, y, h2, g_final, tile0, n_tiles):
    tc = TC_COMB
    last = tile0 + n_tiles - 1
    return pl.pallas_call(
        functools.partial(_combine_kernel, n_tiles), grid=(n_tiles,), name="moe_combine",
        in_specs=[pl.BlockSpec((1, SUBLANES, LANES), lambda i: (tile0 + i, 0, 0), memory_space=pltpu.SMEM),
                  pl.BlockSpec((1, SUBLANES, LANES), lambda i: (jnp.minimum(tile0 + i + 1, last), 0, 0),
                               memory_space=pltpu.SMEM),
                  pl.BlockSpec((1, SUBLANES, tc), lambda i: (tile0 + i, 0, 0)),
                  pl.BlockSpec(memory_space=pl.ANY),
                  pl.BlockSpec((tc, D_MODEL), lambda i: (tile0 + i, 0)),
                  _const_spec((1, D_MODEL))],
        out_specs=pl.BlockSpec((tc, D_MODEL), lambda i: (i, 0)),
        out_shape=jax.ShapeDtypeStruct((n_tiles * tc, D_MODEL), F32),
        scratch_shapes=[pltpu.VMEM((2, K_ROWS, D_MODEL), F32), pltpu.SemaphoreType.DMA((2,))],
        compiler_params=_params())(desc, desc, kpos, y, h2, g_final)


def _forward(xs, mems, g_mix, w_in, sink, conv_w, w_attn_br, w_conv_br, w_mix_out, g_cross, g_mem,
             w_xq, w_xkv, w_xo, g_moe, w_router, b_router, w_gate, b_gate, w_up, b_up, w_down,
             b_down, g_final):
    groups = [(x.shape[0], x.shape[1]) for x in xs]
    for _, s in groups:
        assert s % max(TM_QKV, TQ_ATTN, TM_MIX, TM_CROSS, TC_COMB) == 0
    assert len(xs) == 2, "two request groups"
    xa, xb = (x.reshape(-1, D_MODEL) for x in xs)
    mem = jnp.concatenate(mems, axis=0)
    n = xa.shape[0] + xb.shape[0]
    tables = _rope_tables(max(s for _, s in groups))
    assert w_in.shape[0] == 1, "single-layer trunk: the final norm is fused into the combine kernel"
    l = 0
    row2 = lambda v: v.reshape(1, -1)
    w_in_b = w_in[l].astype(BF16)
    perm = np.array([(kv * GROUP + g) * HEAD_DIM + d
                     for g in range(GROUP) for kv in range(N_KV_HEADS) for d in range(HEAD_DIM)])
    w_qkv = jnp.concatenate([w_in_b[:, :ATTN_W][:, perm], w_in_b[:, ATTN_W:QKV_W]], axis=1)
    q, kv = _qkv_call(xa, xb, row2(g_mix[l]), w_qkv, tables, groups)
    o = _attn_call(q, kv, sink[l], groups)
    h = _mixer_call(xa, xb, o, row2(g_mix[l]), w_in_b[:, QKV_W:], conv_w[l],
                    w_attn_br[l].astype(BF16)[perm, :], w_conv_br[l].astype(BF16),
                    w_mix_out[l].astype(BF16), groups)
    memkv = _memkv_call(mem, row2(g_mem[l]), w_xkv[l].astype(BF16))
    wr = w_router[l]
    wr_hi = wr.astype(BF16)
    wr_lo = (wr - wr_hi.astype(F32)).astype(BF16)
    w_r2t = jnp.concatenate([wr_hi.T, wr_lo.T], axis=0)
    h2, xs_rows, idx = _cross_call(h, memkv, row2(g_cross[l]), w_xq[l].astype(BF16), w_xo[l].astype(BF16),
                                   row2(g_moe[l]), w_r2t, b_router[l].reshape(-1, 1), groups)
    seg_rows = TOP_K * n + (n // TC_COMB) * N_EXPERTS * (SEG_ALIGN - 1)
    n_blocks = -(-seg_rows // BM_MOE) + N_EXPERTS
    kpos, desc, meta = _plan_call(idx, n_blocks)
    xs_sorted = _dispatch_call(meta[1], desc, kpos, xs_rows, n_blocks)
    y = _ffn_call(meta[0], meta[2, 0:1], xs_sorted, w_gate[l], b_gate[l][:, None, :],
                  w_up[l], b_up[l][:, None, :], w_down[l], b_down[l][:, None, :])
    tc = TC_COMB
    outs, t0 = [], 0
    for nb, s in groups:
        nt = nb * s // tc
        outs.append(_combine_call(desc, kpos, y, h2, row2(g_final), t0, nt))
        t0 += nt
    return tuple(o.reshape(x.shape) for o, x in zip(outs, xs))


def kernel(x_prompt, x_sample, mem_prompt, mem_sample, g_mix, w_in, sink, conv_w, w_attn_br, w_conv_br,
           w_mix_out, g_cross, g_mem, w_xq, w_xkv, w_xo, g_moe, w_router, b_router, w_gate, b_gate,
           w_up, b_up, w_down, b_down, g_final):
    return _forward([x_prompt, x_sample], [mem_prompt, mem_sample], g_mix, w_in, sink, conv_w,
                    w_attn_br, w_conv_br, w_mix_out, g_cross, g_mem, w_xq, w_xkv, w_xo, g_moe,
                    w_router, b_router, w_gate, b_gate, w_up, b_up, w_down, b_down, g_final)
```

```python
import functools

import numpy as np
import jax
import jax.numpy as jnp
from jax import lax
from jax.experimental import pallas as pl
from jax.experimental.pallas import tpu as pltpu

F32 = jnp.float32
BF16 = jnp.bfloat16

D_MODEL = 1024
N_HEADS = 8
N_KV_HEADS = 2
HEAD_DIM = 64
GROUP = N_HEADS // N_KV_HEADS
ATTN_W = N_HEADS * HEAD_DIM
KV_W = N_KV_HEADS * HEAD_DIM
QKV_W = ATTN_W + 2 * KV_W
WINDOW = 128
ROT_DIM = HEAD_DIM // 4
ROPE_THETA = 500000.0
CONV_W = D_MODEL // 2
REST_W = 3 * CONV_W + 2 * D_MODEL
N_MEM = 256
X_HEADS = 4
X_HEAD_DIM = D_MODEL // X_HEADS
N_EXPERTS = 32
TOP_K = 4
D_FF = D_MODEL
SWIGLU_ALPHA = 1.702
SWIGLU_LIMIT = 7.0
EPS = 1e-5

LANES = 128
SUBLANES = 8
KEY_BLOCK = WINDOW
TM_QKV = 512
TQ_ATTN = 256
TM_MIX = 256
TM_CROSS = 512
BM_MOE = 512
TC_COMB = 256
NEG_BIG = -1e30
VMEM_LIMIT = 56 * 1024 * 1024


def _rms(x, g):
    var = jnp.mean(x * x, axis=-1, keepdims=True)
    return x * lax.rsqrt(var + EPS) * g


def _tile_meta(groups, tile):
    pos, first, last, bidx = [], [], [], []
    b0 = 0
    for nb, s in groups:
        per = s // tile
        for b in range(nb):
            for j in range(per):
                pos.append(j)
                first.append(int(j == 0))
                last.append(int(j == per - 1))
                bidx.append(b0 + b)
        b0 += nb
    return jnp.asarray(np.array([pos, first, last, bidx], dtype=np.int32))


def _const_spec(shape):
    nd = len(shape)
    return pl.BlockSpec(shape, lambda *_: (0,) * nd)


def _params(vmem=VMEM_LIMIT):
    return pltpu.CompilerParams(dimension_semantics=("arbitrary",), vmem_limit_bytes=vmem)


def _qkv_kernel(na, meta_ref, xa_ref, xb_ref, g_ref, w_ref, cos_ref, sa_ref, sb_ref, q_ref, kv_ref):
    del meta_ref
    x = jnp.where(pl.program_id(0) < na, xa_ref[...], xb_ref[...])
    xn = _rms(x, g_ref[...]).astype(BF16)
    proj = jnp.dot(xn, w_ref[...], preferred_element_type=F32)
    c, sa, sb = cos_ref[...], sa_ref[...], sb_ref[...]
    n_rot = (ATTN_W + KV_W) // LANES
    for gi in range(n_rot):
        p = proj[:, gi * LANES:(gi + 1) * LANES]
        r = p * c + pltpu.roll(p, LANES - ROT_DIM // 2, 1) * sa + pltpu.roll(p, ROT_DIM // 2, 1) * sb
        if gi < ATTN_W // LANES:
            q_ref[:, gi * LANES:(gi + 1) * LANES] = (r * (HEAD_DIM ** -0.5)).astype(BF16)
        else:
            kv_ref[:, 0:KV_W] = r.astype(BF16)
    kv_ref[:, KV_W:2 * KV_W] = proj[:, ATTN_W + KV_W:QKV_W].astype(BF16)


def _rope_tables(s_max):
    half = ROT_DIM // 2
    inv_freq = ROPE_THETA ** (-(jnp.arange(half, dtype=F32) * 2.0) / ROT_DIM)
    ang = jnp.arange(s_max, dtype=F32)[:, None] * inv_freq[None, :]
    cos, sin = jnp.cos(ang), jnp.sin(ang)
    d = np.arange(LANES) % HEAD_DIM
    j = d % half
    cos_l, sin_l = cos[:, j], sin[:, j]
    rot = jnp.asarray(d < ROT_DIM)[None, :]
    lo = jnp.asarray(d < half)[None, :]
    hi = jnp.asarray((d >= half) & (d < ROT_DIM))[None, :]
    c = jnp.where(rot, cos_l, 1.0)
    sa = jnp.where(lo, -sin_l, 0.0)
    sb = jnp.where(hi, sin_l, 0.0)
    return c, sa, sb


def _qkv_call(xa, xb, g_mix, w_qkv, tables, groups):
    tm = TM_QKV
    na = xa.shape[0] // tm
    n = xa.shape[0] + xb.shape[0]
    meta = _tile_meta(groups, tm)
    row = lambda i, m: (i, 0)
    tab = lambda i, m: (m[0, i], 0)
    gs = pltpu.PrefetchScalarGridSpec(
        num_scalar_prefetch=1, grid=(n // tm,),
        in_specs=[pl.BlockSpec((tm, D_MODEL), lambda i, m: (jnp.minimum(i, na - 1), 0)),
                  pl.BlockSpec((tm, D_MODEL), lambda i, m: (jnp.maximum(i - na, 0), 0)),
                  _const_spec((1, D_MODEL)),
                  _const_spec((D_MODEL, QKV_W)),
                  pl.BlockSpec((tm, LANES), tab), pl.BlockSpec((tm, LANES), tab),
                  pl.BlockSpec((tm, LANES), tab)],
        out_specs=[pl.BlockSpec((tm, ATTN_W), row), pl.BlockSpec((tm, 2 * KV_W), row)])
    return pl.pallas_call(
        functools.partial(_qkv_kernel, na), grid_spec=gs, name="qkv_rope",
        out_shape=[jax.ShapeDtypeStruct((n, ATTN_W), BF16), jax.ShapeDtypeStruct((n, 2 * KV_W), BF16)],
        compiler_params=_params())(meta, xa, xb, g_mix, w_qkv, *tables)


def _attn_kernel(meta_ref, sink_ref, q_ref, kvp_ref, kvm_ref, kvn_ref, o_ref, kcat_ref):
    i = pl.program_id(0)
    kb = KEY_BLOCK
    tq = q_ref.shape[0]
    kcat_ref[0:kb, :] = kvp_ref[...]
    kcat_ref[kb:kb + tq, :] = kvm_ref[...]
    kcat_ref[kb + tq:kb + tq + kb, :] = kvn_ref[...]
    r = lax.broadcasted_iota(jnp.int32, (kb, 3 * kb), 0)
    c = lax.broadcasted_iota(jnp.int32, (kb, 3 * kb), 1)
    dlt = c - r
    klane = lax.broadcasted_iota(jnp.int32, (3 * kb, KV_W), 1)
    olane = lax.broadcasted_iota(jnp.int32, (GROUP * kb, KV_W), 1)
    nsub = tq // kb
    for s in range(nsub):
        first = meta_ref[1, i * nsub + s]
        last = meta_ref[2, i * nsub + s]
        lo = jnp.where(first == 1, kb, 0)
        hi = jnp.where(last == 1, 2 * kb, 3 * kb)
        valid = (dlt >= 0) & (dlt <= 2 * WINDOW) & (c >= lo) & (c < hi)
        bias = jnp.where(valid, 0.0, NEG_BIG)
        bias = jnp.concatenate([bias] * GROUP, axis=0)
        kw = kcat_ref[s * kb:(s + 3) * kb, :]
        kc = kw[:, 0:KV_W]
        vc = jnp.concatenate([kw[:, KV_W:2 * KV_W], jnp.ones((3 * kb, KV_W), BF16)], axis=1)
        q4 = jnp.concatenate([q_ref[s * kb:(s + 1) * kb, g * KV_W:(g + 1) * KV_W] for g in range(GROUP)],
                             axis=0)
        res = []
        for kh in range(N_KV_HEADS):
            own = (klane >= kh * HEAD_DIM) & (klane < (kh + 1) * HEAD_DIM)
            kk = jnp.where(own, kc, jnp.zeros_like(kc))
            sc = lax.dot_general(q4, kk, (((1,), (1,)), ((), ())), preferred_element_type=F32) + bias
            snk = jnp.concatenate([jnp.full((kb, 1), sink_ref[kh * GROUP + g], F32) for g in range(GROUP)],
                                  axis=0)
            m = jnp.maximum(jnp.max(sc, axis=-1, keepdims=True), snk)
            p = jnp.exp(sc - m).astype(BF16)
            pv = jnp.dot(p, vc, preferred_element_type=F32)
            denom = pv[:, KV_W:KV_W + 1] + jnp.exp(snk - m)
            res.append(pv[:, 0:KV_W] / denom)
        out = res[0]
        for kh in range(1, N_KV_HEADS):
            out = jnp.where(olane >= kh * HEAD_DIM, res[kh], out)
        out = out.astype(BF16)
        for g in range(GROUP):
            o_ref[s * kb:(s + 1) * kb, g * KV_W:(g + 1) * KV_W] = out[g * kb:(g + 1) * kb, :]


def _attn_call(q, kv, sink, groups):
    n = q.shape[0]
    tq, kb = TQ_ATTN, KEY_BLOCK
    per = tq // kb
    nkb = n // kb
    meta = _tile_meta(groups, kb)
    gs = pltpu.PrefetchScalarGridSpec(
        num_scalar_prefetch=1, grid=(n // tq,),
        in_specs=[pl.BlockSpec(memory_space=pltpu.SMEM),
                  pl.BlockSpec((tq, ATTN_W), lambda i, m: (i, 0)),
                  pl.BlockSpec((kb, 2 * KV_W), lambda i, m: (jnp.maximum(i * per - 1, 0), 0)),
                  pl.BlockSpec((tq, 2 * KV_W), lambda i, m: (i, 0)),
                  pl.BlockSpec((kb, 2 * KV_W), lambda i, m: (jnp.minimum((i + 1) * per, nkb - 1), 0))],
        out_specs=pl.BlockSpec((tq, ATTN_W), lambda i, m: (i, 0)),
        scratch_shapes=[pltpu.VMEM((tq + 2 * kb, 2 * KV_W), BF16)])
    return pl.pallas_call(
        _attn_kernel, grid_spec=gs, name="window_attn",
        out_shape=jax.ShapeDtypeStruct((n, ATTN_W), BF16),
        compiler_params=_params())(meta, sink, q, kv, kv, kv)


def _mixer_kernel(na, meta_ref, xa_ref, xpa_ref, xna_ref, xb_ref, xpb_ref, xnb_ref, o_ref, g_ref, w_ref,
                  cw_ref, wa_ref, wc_ref, wm_ref, h_ref):
    i = pl.program_id(0)
    tm = xa_ref.shape[0]
    g = g_ref[...]
    in_a = i < na
    x = jnp.where(in_a, xa_ref[...], xb_ref[...])
    xb = _rms(x, g).astype(BF16)
    proj = jnp.dot(xb, w_ref[...], preferred_element_type=F32)
    cb = proj[:, 0:CONV_W]
    u = proj[:, CONV_W:2 * CONV_W] * proj[:, 2 * CONV_W:3 * CONV_W]
    xh = jnp.concatenate([jnp.where(in_a, xpa_ref[...], xpb_ref[...]),
                          jnp.where(in_a, xna_ref[...], xnb_ref[...])], axis=0)
    xhb = _rms(xh, g).astype(BF16)
    ph = jnp.dot(xhb, w_ref[:, CONV_W:3 * CONV_W], preferred_element_type=F32)
    uh = ph[:, 0:CONV_W] * ph[:, CONV_W:2 * CONV_W]
    first = meta_ref[1, i]
    last = meta_ref[2, i]
    u_prev = jnp.where(first == 1, 0.0, uh[SUBLANES - 1:SUBLANES, :])
    u_next = jnp.where(last == 1, 0.0, uh[SUBLANES:SUBLANES + 1, :])
    row = lax.broadcasted_iota(jnp.int32, (tm, 1), 0)
    up = jnp.where(row == 0, u_prev, pltpu.roll(u, 1, 0))
    dn = jnp.where(row == tm - 1, u_next, pltpu.roll(u, tm - 1, 0))
    cw = cw_ref[...]
    y = up * cw[0:1, :] + u * cw[1:2, :] + dn * cw[2:3, :]
    conv = (cb * y).astype(BF16)
    conv_br = jnp.dot(conv, wc_ref[...], preferred_element_type=F32)
    attn_br = jnp.dot(o_ref[...], wa_ref[...], preferred_element_type=F32)
    g0 = jax.nn.sigmoid(proj[:, 3 * CONV_W:3 * CONV_W + D_MODEL])
    g1 = jax.nn.sigmoid(proj[:, 3 * CONV_W + D_MODEL:REST_W])
    merged = (g0 * attn_br + g1 * conv_br).astype(BF16)
    h_ref[...] = x + jnp.dot(merged, wm_ref[...], preferred_element_type=F32)


def _mixer_call(xa, xb, o, g_mix, w_rest, conv_w, w_attn_br, w_conv_br, w_mix_out, groups):
    tm = TM_MIX
    per = tm // SUBLANES
    na, nb = xa.shape[0] // tm, xb.shape[0] // tm
    n = xa.shape[0] + xb.shape[0]
    meta = _tile_meta(groups, tm)

    def x_specs(tile_of, n_tiles):
        last8 = n_tiles * per - 1
        return [pl.BlockSpec((tm, D_MODEL), lambda i, m: (jnp.clip(tile_of(i), 0, n_tiles - 1), 0)),
                pl.BlockSpec((SUBLANES, D_MODEL), lambda i, m: (jnp.clip(tile_of(i) * per - 1, 0, last8), 0)),
                pl.BlockSpec((SUBLANES, D_MODEL), lambda i, m: (jnp.clip((tile_of(i) + 1) * per, 0, last8), 0))]

    gs = pltpu.PrefetchScalarGridSpec(
        num_scalar_prefetch=1, grid=(n // tm,),
        in_specs=x_specs(lambda i: i, na) + x_specs(lambda i: i - na, nb) + [
                  pl.BlockSpec((tm, ATTN_W), lambda i, m: (i, 0)),
                  _const_spec((1, D_MODEL)),
                  _const_spec((D_MODEL, REST_W)),
                  _const_spec((3, CONV_W)),
                  _const_spec((ATTN_W, D_MODEL)),
                  _const_spec((CONV_W, D_MODEL)),
                  _const_spec((D_MODEL, D_MODEL))],
        out_specs=pl.BlockSpec((tm, D_MODEL), lambda i, m: (i, 0)))
    return pl.pallas_call(
        functools.partial(_mixer_kernel, na), grid_spec=gs, name="mixer",
        out_shape=jax.ShapeDtypeStruct((n, D_MODEL), F32),
        compiler_params=_params())(meta, xa, xa, xa, xb, xb, xb, o, g_mix, w_rest, conv_w, w_attn_br,
                                   w_conv_br, w_mix_out)


def _memkv_kernel(mem_ref, g_ref, w_ref, kv_ref):
    mn = _rms(mem_ref[0], g_ref[...]).astype(BF16)
    kv_ref[0] = jnp.dot(mn, w_ref[...], preferred_element_type=F32).astype(BF16)


def _memkv_call(mem, g_mem, w_xkv):
    nb = mem.shape[0]
    return pl.pallas_call(
        _memkv_kernel, grid=(nb,), name="mem_kv",
        in_specs=[pl.BlockSpec((1, N_MEM, D_MODEL), lambda b: (b, 0, 0)),
                  _const_spec((1, D_MODEL)),
                  _const_spec((D_MODEL, 2 * D_MODEL))],
        out_specs=pl.BlockSpec((1, N_MEM, 2 * D_MODEL), lambda b: (b, 0, 0)),
        out_shape=jax.ShapeDtypeStruct((nb, N_MEM, 2 * D_MODEL), BF16),
        compiler_params=_params())(mem, g_mem, w_xkv)


def _cross_kernel(meta_ref, h_ref, kv_ref, gc_ref, wq_ref, wo_ref, gm_ref, wr_ref, br_ref,
                  h2_ref, xs_ref, idx_ref):
    del meta_ref
    tm = h_ref.shape[0]
    h = h_ref[...]
    hn = _rms(h, gc_ref[...]).astype(BF16)
    q = (jnp.dot(hn, wq_ref[...], preferred_element_type=F32) * (X_HEAD_DIM ** -0.5)).astype(BF16)
    outs = []
    for hd in range(X_HEADS):
        qh = q[:, hd * X_HEAD_DIM:(hd + 1) * X_HEAD_DIM]
        kh = kv_ref[0, :, hd * X_HEAD_DIM:(hd + 1) * X_HEAD_DIM]
        vh = kv_ref[0, :, D_MODEL + hd * X_HEAD_DIM:D_MODEL + (hd + 1) * X_HEAD_DIM]
        s = lax.dot_general(qh, kh, (((1,), (1,)), ((), ())), preferred_element_type=F32)
        m = jnp.max(s, axis=-1, keepdims=True)
        p = jnp.exp(s - m)
        p = (p / jnp.sum(p, axis=-1, keepdims=True)).astype(BF16)
        outs.append(jnp.dot(p, vh, preferred_element_type=F32).astype(BF16))
    o = jnp.concatenate(outs, axis=1)
    h2 = h + jnp.dot(o, wo_ref[...], preferred_element_type=F32)
    h2_ref[...] = h2

    hn3 = _rms(h2, gm_ref[...])
    hi = hn3.astype(BF16)
    hi32 = hi.astype(F32)
    lo = (hn3 - hi32).astype(BF16)
    xs_ref[:, 0:D_MODEL] = hn3

    nt = (((1,), (1,)), ((), ()))
    r1 = lax.dot_general(wr_ref[...], hi, nt, preferred_element_type=F32)
    r2 = lax.dot_general(wr_ref[...], lo, nt, preferred_element_type=F32)
    e = N_EXPERTS
    logits = ((r2[e:2 * e] + r2[0:e]) + r1[e:2 * e]) + r1[0:e] + br_ref[...]
    eio = lax.broadcasted_iota(jnp.int32, (e, tm), 0)
    cur = logits
    vals, sels = [], []
    for _ in range(TOP_K):
        mx = jnp.max(cur, axis=0, keepdims=True)
        sel = jnp.min(jnp.where(cur == mx, eio, e), axis=0, keepdims=True)
        vals.append(mx)
        sels.append(sel)
        cur = jnp.where(eio == sel, -jnp.inf, cur)
    ex = [jnp.exp(v - vals[0]) for v in vals]
    tot = ex[0] + ex[1] + ex[2] + ex[3]
    idx_ref[...] = jnp.concatenate(sels, axis=0)
    gws = [x / tot for x in ex]
    w_hi = [g.astype(BF16).astype(F32) for g in gws]
    w_lo = [(g - h).astype(BF16).astype(F32) for g, h in zip(gws, w_hi)]
    wt = jnp.transpose(jnp.concatenate(w_hi + w_lo, axis=0))
    et = jnp.transpose(jnp.concatenate([s.astype(F32) for s in sels]
                                       + [jnp.zeros((SUBLANES - TOP_K, tm), F32)], axis=0))
    xs_ref[:, D_MODEL:XS_W] = jnp.concatenate(
        [wt, et, jnp.zeros((tm, LANES - 2 * SUBLANES), F32)], axis=1)


def _cross_call(h1, memkv, g_cross, w_xq, w_xo, g_moe, w_r2t, b_router, groups):
    n = h1.shape[0]
    tm = TM_CROSS
    meta = _tile_meta(groups, tm)
    row = lambda i, m: (i, 0)
    col = lambda i, m: (0, i)
    gs = pltpu.PrefetchScalarGridSpec(
        num_scalar_prefetch=1, grid=(n // tm,),
        in_specs=[pl.BlockSpec((tm, D_MODEL), row),
                  pl.BlockSpec((1, N_MEM, 2 * D_MODEL), lambda i, m: (m[3, i], 0, 0)),
                  _const_spec((1, D_MODEL)),
                  _const_spec((D_MODEL, D_MODEL)),
                  _const_spec((D_MODEL, D_MODEL)),
                  _const_spec((1, D_MODEL)),
                  _const_spec((2 * N_EXPERTS, D_MODEL)),
                  _const_spec((N_EXPERTS, 1))],
        out_specs=[pl.BlockSpec((tm, D_MODEL), row),
                   pl.BlockSpec((tm, XS_W), row),
                   pl.BlockSpec((TOP_K, tm), col)])
    return pl.pallas_call(
        _cross_kernel, grid_spec=gs, name="cross_router",
        out_shape=[jax.ShapeDtypeStruct((n, D_MODEL), F32),
                   jax.ShapeDtypeStruct((n, XS_W), F32),
                   jax.ShapeDtypeStruct((TOP_K, n), jnp.int32)],
        compiler_params=_params())(meta, h1, memkv, g_cross, w_xq, w_xo, g_moe, w_r2t, b_router)


LOG2_BM = BM_MOE.bit_length() - 1
assert 1 << LOG2_BM == BM_MOE
SEG_ALIGN = SUBLANES
LOG2_SEG = SEG_ALIGN.bit_length() - 1
PIECE_ROWS = 2 * SEG_ALIGN
K_ROWS = 1280
M_CHUNK = 640
MAX_FULL = K_ROWS // PIECE_ROWS
assert 1 << LOG2_SEG == SEG_ALIGN and K_ROWS % M_CHUNK == 0 and MAX_FULL <= LANES
assert K_ROWS >= TOP_K * TC_COMB + N_EXPERTS * (SEG_ALIGN - 1)
DESC_FULL_DST, DESC_FULL_SRC, DESC_REM_DST, DESC_REM_SRC, DESC_COUNT = range(5)
AUX_W_HI, AUX_W_LO, AUX_EID = 0, TOP_K, 2 * TOP_K
XS_W = D_MODEL + LANES


def _sublane_scan(x, n):
    row = lax.broadcasted_iota(jnp.int32, x.shape, 0)
    sft = 1
    while sft < n:
        x = x + jnp.where(row >= sft, pltpu.roll(x, sft, 0), 0)
        sft *= 2
    return x


def _owner(ends, lane):
    ebl = lax.broadcasted_iota(jnp.int32, ends.shape, 0)
    owner = jnp.sum(jnp.where(ends <= lane, 1.0, 0.0), axis=0, keepdims=True).astype(jnp.int32)
    return ebl == owner


def _pick(sel, val):
    return jnp.sum(jnp.where(sel, val.astype(F32), 0.0), axis=0, keepdims=True).astype(jnp.int32)


def _plan_kernel(n_tiles, idx_ref, kpos_ref, desc_ref, meta_ref, tri_ref, acc_ref, run_ref):
    i = pl.program_id(0)
    e, tp = N_EXPERTS, TC_COMB
    nb_pad = meta_ref.shape[1]
    eio = lax.broadcasted_iota(jnp.int32, (e, tp), 0)
    idx = idx_ref[...]
    onehots = [eio == idx[k:k + 1, :] for k in range(TOP_K)]
    s = jnp.where(onehots[0], 1.0, 0.0)
    for k in range(1, TOP_K):
        s = s + jnp.where(onehots[k], 1.0, 0.0)
    cnt_t = jnp.sum(s, axis=1, keepdims=True).astype(jnp.int32)
    cnt = jnp.broadcast_to(cnt_t, (e, LANES))
    seg = lax.shift_left(lax.shift_right_logical(cnt + (SEG_ALIGN - 1), LOG2_SEG), LOG2_SEG)

    @pl.when(i == 0)
    def _():
        acc_ref[...] = jnp.zeros_like(acc_ref)
        r = lax.broadcasted_iota(jnp.int32, (tp, tp), 0)
        c = lax.broadcasted_iota(jnp.int32, (tp, tp), 1)
        tri_ref[...] = jnp.where(r < c, 1.0, 0.0).astype(BF16)

    @pl.when(i < n_tiles)
    def _():
        acc_ref[...] += seg

    @pl.when(i == n_tiles)
    def _():
        tot = acc_ref[...]
        pad = lax.shift_left(lax.shift_right_logical(tot + (BM_MOE - 1), LOG2_BM), LOG2_BM)
        pad_end = _sublane_scan(pad, e)
        pad_start = pad_end - pad
        run_ref[...] = pad_start
        bpos = lax.broadcasted_iota(jnp.int32, (e, nb_pad), 1) * BM_MOE
        ebl = lax.broadcasted_iota(jnp.int32, (e, nb_pad), 0)
        done = jnp.where(pad_end[:, 0:1] <= bpos, 1.0, 0.0)
        bexp = jnp.minimum(jnp.sum(done, axis=0, keepdims=True).astype(jnp.int32), e - 1)
        row_end = (pad_start + tot)[:, 0:1].astype(F32)
        rend_b = jnp.sum(jnp.where(ebl == bexp, row_end, 0.0), axis=0, keepdims=True).astype(jnp.int32)
        nvalid = jnp.clip(rend_b - bpos[0:1, :], 0, BM_MOE)
        total = jnp.sum(jnp.where(ebl == e - 1, pad_end[:, 0:1].astype(F32), 0.0), axis=0, keepdims=True)
        nused = lax.shift_right_logical(total.astype(jnp.int32), LOG2_BM)
        meta_ref[...] = jnp.concatenate(
            [bexp, nvalid, nused, jnp.zeros((SUBLANES - 3, nb_pad), jnp.int32)], axis=0)

    @pl.when(i >= n_tiles)
    def _():
        before = jnp.dot(s.astype(BF16), tri_ref[...], preferred_element_type=F32)
        kbase = _sublane_scan(seg, e) - seg
        pos = kbase[:, 0:1].astype(F32) + before
        kp = [jnp.sum(jnp.where(onehots[k], pos, 0.0), axis=0, keepdims=True).astype(jnp.int32)
              for k in range(TOP_K)]
        kpos_ref[0] = jnp.concatenate(kp + [jnp.zeros((SUBLANES - TOP_K, tp), jnp.int32)], axis=0)

        start = run_ref[...]
        lane = lax.broadcasted_iota(jnp.int32, (e, LANES), 1)
        lane1 = lane[0:1, :]
        units = lax.shift_right_logical(seg, LOG2_SEG)
        nfull = lax.shift_right_logical(units, 1)
        fend = _sublane_scan(nfull, e)
        sel = _owner(fend, lane)
        foff = (lane1 - _pick(sel, fend - nfull)) * PIECE_ROWS
        n_full = fend[e - 1:e, :]
        full_ok = lane1 < n_full
        full_dst = jnp.where(full_ok, _pick(sel, start) + foff, 0)
        full_src = jnp.where(full_ok, _pick(sel, kbase) + foff, 0)
        pend = _sublane_scan(units & 1, e)
        sel = _owner(pend, lane)
        n_rem = pend[e - 1:e, :]
        rem_ok = lane1 < n_rem
        rem_dst = jnp.where(rem_ok, _pick(sel, start + nfull * PIECE_ROWS), 0)
        rem_src = jnp.where(rem_ok, _pick(sel, kbase + nfull * PIECE_ROWS), 0)
        counts = jnp.where(lane1 == 0, n_full, jnp.where(lane1 == 1, n_rem, 0))
        desc_ref[0] = jnp.concatenate(
            [full_dst, full_src, rem_dst, rem_src, counts, jnp.zeros((SUBLANES - 5, LANES), jnp.int32)], axis=0)
        run_ref[...] = run_ref[...] + seg


def _plan_call(idx, n_blocks):
    n = idx.shape[1]
    tp = TC_COMB
    n_tiles = n // tp
    nb_pad = -(-n_blocks // LANES) * LANES
    tile = lambda i: (jnp.maximum(i - n_tiles, 0), 0, 0)
    return pl.pallas_call(
        functools.partial(_plan_kernel, n_tiles), grid=(2 * n_tiles,), name="route_plan",
        in_specs=[pl.BlockSpec((TOP_K, tp), lambda i: (0, i % n_tiles))],
        out_specs=[pl.BlockSpec((1, SUBLANES, tp), tile),
                   pl.BlockSpec((1, SUBLANES, LANES), tile),
                   pl.BlockSpec((SUBLANES, nb_pad), lambda i: (0, 0))],
        out_shape=[jax.ShapeDtypeStruct((n_tiles, SUBLANES, tp), jnp.int32),
                   jax.ShapeDtypeStruct((n_tiles, SUBLANES, LANES), jnp.int32),
                   jax.ShapeDtypeStruct((SUBLANES, nb_pad), jnp.int32)],
        scratch_shapes=[pltpu.VMEM((tp, tp), BF16), pltpu.VMEM((N_EXPERTS, LANES), jnp.int32),
                        pltpu.VMEM((N_EXPERTS, LANES), jnp.int32)],
        compiler_params=_params())(idx)


def _slot_hits(kp, iota):
    hit = jnp.where(iota == kp[0], 1.0, 0.0)
    for k in range(1, TOP_K):
        hit = hit + jnp.where(iota == kp[k], 1.0, 0.0)
    return hit


def _for_each_copy(n_full, n_rem, fn):
    def full(j, carry):
        fn(DESC_FULL_DST, j, PIECE_ROWS)
        return carry

    def rem(j, carry):
        fn(DESC_REM_DST, j, SEG_ALIGN)
        return carry
    lax.fori_loop(0, n_full, full, 0)
    lax.fori_loop(0, n_rem, rem, 0)


def _dispatch_kernel(n_tiles, n_blocks, nvalid_ref, desc_ref, kpos_ref, xs_ref, out_hbm, zero_ref, buf,
                     sems, zsem, issued_ref):
    i = pl.program_id(0)
    td = xs_ref.shape[0]
    bm = zero_ref.shape[0]
    cur = i % 2

    def copy(b, dst, src, rows):
        return pltpu.make_async_copy(buf.at[b, pl.ds(pl.multiple_of(src, SEG_ALIGN), rows), :],
                                     out_hbm.at[pl.ds(pl.multiple_of(dst, SEG_ALIGN), rows), :], sems.at[b])

    def drain(b):
        _for_each_copy(issued_ref[b, 0], issued_ref[b, 1], lambda row, j, rows: copy(b, 0, 0, rows).wait())

    @pl.when(i == 0)
    def _():
        zero_ref[...] = jnp.zeros_like(zero_ref)

        def fill(b):
            return pltpu.make_async_copy(zero_ref, out_hbm.at[pl.ds(pl.multiple_of(b * bm, bm), bm), :], zsem)

        def start(b, carry):
            @pl.when(nvalid_ref[b] < bm)
            def _():
                fill(b).start()
            return carry

        def wait(b, carry):
            @pl.when(nvalid_ref[b] < bm)
            def _():
                fill(b).wait()
            return carry
        lax.fori_loop(0, n_blocks, start, 0)
        lax.fori_loop(0, n_blocks, wait, 0)

    @pl.when(i >= 2)
    def _():
        drain(cur)

    kp = kpos_ref[0]
    x = xs_ref[...].astype(BF16)
    for c in range(K_ROWS // M_CHUNK):
        rows = lax.broadcasted_iota(jnp.int32, (M_CHUNK, td), 0) + c * M_CHUNK
        sel = _slot_hits([kp[k:k + 1, :] for k in range(TOP_K)], rows).astype(BF16)
        buf[cur, c * M_CHUNK:(c + 1) * M_CHUNK, :] = jnp.dot(sel, x, preferred_element_type=F32)

    n_full, n_rem = desc_ref[0, DESC_COUNT, 0], desc_ref[0, DESC_COUNT, 1]
    _for_each_copy(n_full, n_rem,
                   lambda row, j, rows: copy(cur, desc_ref[0, row, j], desc_ref[0, row + 1, j], rows).start())
    issued_ref[cur, 0] = n_full
    issued_ref[cur, 1] = n_rem

    @pl.when(i == n_tiles - 1)
    def _():
        drain(cur)

        @pl.when(i >= 1)
        def _():
            drain(1 - cur)


def _dispatch_call(nvalid, desc, kpos, xs, n_blocks):
    n, width = xs.shape
    td, bm = TC_COMB, BM_MOE
    n_tiles = n // td
    gs = pltpu.PrefetchScalarGridSpec(
        num_scalar_prefetch=1, grid=(n_tiles,),
        in_specs=[pl.BlockSpec((1, SUBLANES, LANES), lambda i, nv: (i, 0, 0), memory_space=pltpu.SMEM),
                  pl.BlockSpec((1, SUBLANES, td), lambda i, nv: (i, 0, 0)),
                  pl.BlockSpec((td, width), lambda i, nv: (i, 0))],
        out_specs=pl.BlockSpec(memory_space=pl.ANY),
        scratch_shapes=[pltpu.VMEM((bm, width), xs.dtype), pltpu.VMEM((2, K_ROWS, width), xs.dtype),
                        pltpu.SemaphoreType.DMA((2,)), pltpu.SemaphoreType.DMA(()),
                        pltpu.SMEM((2, 2), jnp.int32)])
    return pl.pallas_call(
        functools.partial(_dispatch_kernel, n_tiles, n_blocks), grid_spec=gs, name="moe_dispatch",
        out_shape=jax.ShapeDtypeStruct((n_blocks * bm, width), xs.dtype),
        compiler_params=_params())(nvalid, desc, kpos, xs)


def _ffn_kernel(bexp_ref, nused_ref, x_ref, wg_ref, bg_ref, wu_ref, bu_ref, wd_ref, bd_ref,
                y_ref, wgb, wub, wdb):
    b = pl.program_id(0)
    nused = nused_ref[0]

    @pl.when((b == 0) | (bexp_ref[b] != bexp_ref[jnp.maximum(b - 1, 0)]))
    def _():
        wgb[...] = wg_ref[0].astype(BF16)
        wub[...] = wu_ref[0].astype(BF16)
        wdb[...] = wd_ref[0].astype(BF16)

    @pl.when(b < nused)
    def _():
        x = x_ref[:, 0:D_MODEL].astype(BF16)
        aux = x_ref[:, D_MODEL:XS_W]
        eid = bexp_ref[b].astype(F32)
        row_w = jnp.zeros((x.shape[0], 1), F32)
        for k in range(TOP_K):
            w_k = aux[:, AUX_W_HI + k:AUX_W_HI + k + 1] + aux[:, AUX_W_LO + k:AUX_W_LO + k + 1]
            row_w = row_w + jnp.where(aux[:, AUX_EID + k:AUX_EID + k + 1] == eid, w_k, 0.0)

        def proj(w_ref, b_ref):
            return jnp.dot(x, w_ref[...], preferred_element_type=F32) + b_ref[0]

        a = jnp.minimum(proj(wgb, bg_ref), SWIGLU_LIMIT)
        u = jnp.clip(proj(wub, bu_ref), -SWIGLU_LIMIT, SWIGLU_LIMIT)
        hid = (a * jax.nn.sigmoid(SWIGLU_ALPHA * a) * (u + 1.0)).astype(BF16)
        y_ref[...] = (jnp.dot(hid, wdb[...], preferred_element_type=F32) + bd_ref[0]) * row_w

    @pl.when(b >= nused)
    def _():
        y_ref[...] = jnp.zeros_like(y_ref)


def _ffn_call(bexp, nused, xs_sorted, w_gate, b_gate, w_up, b_up, w_down, b_down):
    bm = BM_MOE
    n_blocks = xs_sorted.shape[0] // bm
    wspec = lambda r, c: pl.BlockSpec((1, r, c), lambda b, be, nu: (be[b], 0, 0))
    gs = pltpu.PrefetchScalarGridSpec(
        num_scalar_prefetch=2, grid=(n_blocks,),
        in_specs=[pl.BlockSpec((bm, XS_W), lambda b, be, nu: (jnp.minimum(b, nu[0] - 1), 0)),
                  wspec(D_MODEL, D_FF), wspec(1, D_FF), wspec(D_MODEL, D_FF), wspec(1, D_FF),
                  wspec(D_FF, D_MODEL), wspec(1, D_MODEL)],
        out_specs=pl.BlockSpec((bm, D_MODEL), lambda b, be, nu: (b, 0)),
        scratch_shapes=[pltpu.VMEM((D_MODEL, D_FF), BF16), pltpu.VMEM((D_MODEL, D_FF), BF16),
                        pltpu.VMEM((D_FF, D_MODEL), BF16)])
    return pl.pallas_call(
        _ffn_kernel, grid_spec=gs, name="expert_ffn",
        out_shape=jax.ShapeDtypeStruct((n_blocks * bm, D_MODEL), F32),
        compiler_params=_params())(bexp, nused, xs_sorted, w_gate, b_gate, w_up, b_up, w_down, b_down)


def _combine_kernel(nt, desc_ref, descn_ref, kpos_ref, y_hbm, h_ref, gf_ref, out_ref, ybuf, sems):
    i = pl.program_id(0)
    tc = h_ref.shape[0]
    cur = i % 2

    def copy(b, dst, src, rows):
        return pltpu.make_async_copy(y_hbm.at[pl.ds(pl.multiple_of(dst, SEG_ALIGN), rows), :],
                                     ybuf.at[b, pl.ds(pl.multiple_of(src, SEG_ALIGN), rows), :], sems.at[b])

    def fetch(b, ref):
        _for_each_copy(ref[0, DESC_COUNT, 0], ref[0, DESC_COUNT, 1],
                       lambda row, j, rows: copy(b, ref[0, row, j], ref[0, row + 1, j], rows).start())

    @pl.when(i == 0)
    def _():
        ybuf[...] = jnp.zeros_like(ybuf)
        fetch(0, desc_ref)

    @pl.when(i + 1 < nt)
    def _():
        fetch(1 - cur, descn_ref)

    _for_each_copy(desc_ref[0, DESC_COUNT, 0], desc_ref[0, DESC_COUNT, 1],
                   lambda row, j, rows: copy(cur, 0, 0, rows).wait())

    kpt = jnp.transpose(kpos_ref[0].astype(F32))
    kp = [kpt[:, k:k + 1] for k in range(TOP_K)]
    acc = h_ref[...]
    for c in range(K_ROWS // M_CHUNK):
        cols = (lax.broadcasted_iota(jnp.int32, (tc, M_CHUNK), 1) + c * M_CHUNK).astype(F32)
        pick = _slot_hits(kp, cols).astype(BF16)
        y = ybuf[cur, c * M_CHUNK:(c + 1) * M_CHUNK, :]
        y_hi = y.astype(BF16)
        y_lo = (y - y_hi.astype(F32)).astype(BF16)
        acc = acc + jnp.dot(pick, y_hi, preferred_element_type=F32) + jnp.dot(pick, y_lo, preferred_element_type=F32)
    out_ref[...] = _rms(acc, gf_ref[...])


def _combine_call(desc, kpos, y, h2, g_final, tile0, n_tiles):
    tc = TC_COMB
    last = tile0 + n_tiles - 1
    return pl.pallas_call(
        functools.partial(_combine_kernel, n_tiles), grid=(n_tiles,), name="moe_combine",
        in_specs=[pl.BlockSpec((1, SUBLANES, LANES), lambda i: (tile0 + i, 0, 0), memory_space=pltpu.SMEM),
                  pl.BlockSpec((1, SUBLANES, LANES), lambda i: (jnp.minimum(tile0 + i + 1, last), 0, 0),
                               memory_space=pltpu.SMEM),
                  pl.BlockSpec((1, SUBLANES, tc), lambda i: (tile0 + i, 0, 0)),
                  pl.BlockSpec(memory_space=pl.ANY),
                  pl.BlockSpec((tc, D_MODEL), lambda i: (tile0 + i, 0)),
                  _const_spec((1, D_MODEL))],
        out_specs=pl.BlockSpec((tc, D_MODEL), lambda i: (i, 0)),
        out_shape=jax.ShapeDtypeStruct((n_tiles * tc, D_MODEL), F32),
        scratch_shapes=[pltpu.VMEM((2, K_ROWS, D_MODEL), F32), pltpu.SemaphoreType.DMA((2,))],
        compiler_params=_params())(desc, desc, kpos, y, h2, g_final)


def _forward(xs, mems, g_mix, w_in, sink, conv_w, w_attn_br, w_conv_br, w_mix_out, g_cross, g_mem,
             w_xq, w_xkv, w_xo, g_moe, w_router, b_router, w_gate, b_gate, w_up, b_up, w_down,
             b_down, g_final):
    groups = [(x.shape[0], x.shape[1]) for x in xs]
    for _, s in groups:
        assert s % max(TM_QKV, TQ_ATTN, TM_MIX, TM_CROSS, TC_COMB) == 0
    assert len(xs) == 2, "two request groups"
    xa, xb = (x.reshape(-1, D_MODEL) for x in xs)
    mem = jnp.concatenate(mems, axis=0)
    n = xa.shape[0] + xb.shape[0]
    tables = _rope_tables(max(s for _, s in groups))
    assert w_in.shape[0] == 1, "single-layer trunk: the final norm is fused into the combine kernel"
    l = 0
    row2 = lambda v: v.reshape(1, -1)
    w_in_b = w_in[l].astype(BF16)
    perm = np.array([(kv * GROUP + g) * HEAD_DIM + d
                     for g in range(GROUP) for kv in range(N_KV_HEADS) for d in range(HEAD_DIM)])
    w_qkv = jnp.concatenate([w_in_b[:, :ATTN_W][:, perm], w_in_b[:, ATTN_W:QKV_W]], axis=1)
    q, kv = _qkv_call(xa, xb, row2(g_mix[l]), w_qkv, tables, groups)
    o = _attn_call(q, kv, sink[l], groups)
    h = _mixer_call(xa, xb, o, row2(g_mix[l]), w_in_b[:, QKV_W:], conv_w[l],
                    w_attn_br[l].astype(BF16)[perm, :], w_conv_br[l].astype(BF16),
                    w_mix_out[l].astype(BF16), groups)
    memkv = _memkv_call(mem, row2(g_mem[l]), w_xkv[l].astype(BF16))
    wr = w_router[l]
    wr_hi = wr.astype(BF16)
    wr_lo = (wr - wr_hi.astype(F32)).astype(BF16)
    w_r2t = jnp.concatenate([wr_hi.T, wr_lo.T], axis=0)
    h2, xs_rows, idx = _cross_call(h, memkv, row2(g_cross[l]), w_xq[l].astype(BF16), w_xo[l].astype(BF16),
                                   row2(g_moe[l]), w_r2t, b_router[l].reshape(-1, 1), groups)
    seg_rows = TOP_K * n + (n // TC_COMB) * N_EXPERTS * (SEG_ALIGN - 1)
    n_blocks = -(-seg_rows // BM_MOE) + N_EXPERTS
    kpos, desc, meta = _plan_call(idx, n_blocks)
    xs_sorted = _dispatch_call(meta[1], desc, kpos, xs_rows, n_blocks)
    y = _ffn_call(meta[0], meta[2, 0:1], xs_sorted, w_gate[l], b_gate[l][:, None, :],
                  w_up[l], b_up[l][:, None, :], w_down[l], b_down[l][:, None, :])
    tc = TC_COMB
    outs, t0 = [], 0
    for nb, s in groups:
        nt = nb * s // tc
        outs.append(_combine_call(desc, kpos, y, h2, row2(g_final), t0, nt))
        t0 += nt
    return tuple(o.reshape(x.shape) for o, x in zip(outs, xs))


def kernel(x_prompt, x_sample, mem_prompt, mem_sample, g_mix, w_in, sink, conv_w, w_attn_br, w_conv_br,
           w_mix_out, g_cross, g_mem, w_xq, w_xkv, w_xo, g_moe, w_router, b_router, w_gate, b_gate,
           w_up, b_up, w_down, b_down, g_final):
    return _forward([x_prompt, x_sample], [mem_prompt, mem_sample], g_mix, w_in, sink, conv_w,
                    w_attn_br, w_conv_br, w_mix_out, g_cross, g_mem, w_xq, w_xkv, w_xo, g_moe,
                    w_router, b_router, w_gate, b_gate, w_up, b_up, w_down, b_down, g_final)
```

```python
import functools

import numpy as np
import jax
import jax.numpy as jnp
from jax import lax
from jax.experimental import pallas as pl
from jax.experimental.pallas import tpu as pltpu

F32 = jnp.float32
BF16 = jnp.bfloat16

D_MODEL = 1024
N_HEADS = 8
N_KV_HEADS = 2
HEAD_DIM = 64
GROUP = N_HEADS // N_KV_HEADS
ATTN_W = N_HEADS * HEAD_DIM
KV_W = N_KV_HEADS * HEAD_DIM
QKV_W = ATTN_W + 2 * KV_W
WINDOW = 128
ROT_DIM = HEAD_DIM // 4
ROPE_THETA = 500000.0
CONV_W = D_MODEL // 2
REST_W = 3 * CONV_W + 2 * D_MODEL
N_MEM = 256
X_HEADS = 4
X_HEAD_DIM = D_MODEL // X_HEADS
N_EXPERTS = 32
TOP_K = 4
D_FF = D_MODEL
SWIGLU_ALPHA = 1.702
SWIGLU_LIMIT = 7.0
EPS = 1e-5

LANES = 128
SUBLANES = 8
KEY_BLOCK = WINDOW
TM_QKV = 1024
TQ_ATTN = 1024
TM_MIX = 512
TM_CROSS = 1024
BM_MOE = 512
TC_COMB = 256
NEG_BIG = -1e30
VMEM_LIMIT = 56 * 1024 * 1024


def _rms(x, g):
    var = jnp.mean(x * x, axis=-1, keepdims=True)
    return x * lax.rsqrt(var + EPS) * g


def _tile_meta(groups, tile):
    pos, first, last, bidx = [], [], [], []
    b0 = 0
    for nb, s in groups:
        per = s // tile
        for b in range(nb):
            for j in range(per):
                pos.append(j)
                first.append(int(j == 0))
                last.append(int(j == per - 1))
                bidx.append(b0 + b)
        b0 += nb
    return jnp.asarray(np.array([pos, first, last, bidx], dtype=np.int32))


def _const_spec(shape):
    nd = len(shape)
    return pl.BlockSpec(shape, lambda *_: (0,) * nd, pipeline_mode=pl.Buffered(1))


def _params(vmem=VMEM_LIMIT):
    return pltpu.CompilerParams(dimension_semantics=("arbitrary",), vmem_limit_bytes=vmem)


def _qkv_kernel(na, meta_ref, xa_ref, xb_ref, g_ref, w_ref, cos_ref, sa_ref, sb_ref, q_ref, kv_ref):
    del meta_ref
    x = jnp.where(pl.program_id(0) < na, xa_ref[...], xb_ref[...])
    xn = _rms(x, g_ref[...]).astype(BF16)
    proj = jnp.dot(xn, w_ref[...], preferred_element_type=F32)
    c, sa, sb = cos_ref[...], sa_ref[...], sb_ref[...]
    n_rot = (ATTN_W + KV_W) // LANES
    for gi in range(n_rot):
        p = proj[:, gi * LANES:(gi + 1) * LANES]
        r = p * c + pltpu.roll(p, LANES - ROT_DIM // 2, 1) * sa + pltpu.roll(p, ROT_DIM // 2, 1) * sb
        if gi < ATTN_W // LANES:
            q_ref[:, gi * LANES:(gi + 1) * LANES] = (r * (HEAD_DIM ** -0.5)).astype(BF16)
        else:
            kv_ref[:, 0:KV_W] = r.astype(BF16)
    kv_ref[:, KV_W:2 * KV_W] = proj[:, ATTN_W + KV_W:QKV_W].astype(BF16)


def _rope_tables(s_max):
    half = ROT_DIM // 2
    inv_freq = ROPE_THETA ** (-(jnp.arange(half, dtype=F32) * 2.0) / ROT_DIM)
    ang = jnp.arange(s_max, dtype=F32)[:, None] * inv_freq[None, :]
    cos, sin = jnp.cos(ang), jnp.sin(ang)
    d = np.arange(LANES) % HEAD_DIM
    j = d % half
    cos_l, sin_l = cos[:, j], sin[:, j]
    rot = jnp.asarray(d < ROT_DIM)[None, :]
    lo = jnp.asarray(d < half)[None, :]
    hi = jnp.asarray((d >= half) & (d < ROT_DIM))[None, :]
    c = jnp.where(rot, cos_l, 1.0)
    sa = jnp.where(lo, -sin_l, 0.0)
    sb = jnp.where(hi, sin_l, 0.0)
    return c, sa, sb


def _qkv_call(xa, xb, g_mix, w_qkv, tables, groups):
    tm = TM_QKV
    na = xa.shape[0] // tm
    n = xa.shape[0] + xb.shape[0]
    meta = _tile_meta(groups, tm)
    row = lambda i, m: (i, 0)
    tab = lambda i, m: (m[0, i], 0)
    gs = pltpu.PrefetchScalarGridSpec(
        num_scalar_prefetch=1, grid=(n // tm,),
        in_specs=[pl.BlockSpec((tm, D_MODEL), lambda i, m: (jnp.minimum(i, na - 1), 0)),
                  pl.BlockSpec((tm, D_MODEL), lambda i, m: (jnp.maximum(i - na, 0), 0)),
                  _const_spec((1, D_MODEL)),
                  _const_spec((D_MODEL, QKV_W)),
                  pl.BlockSpec((tm, LANES), tab), pl.BlockSpec((tm, LANES), tab),
                  pl.BlockSpec((tm, LANES), tab)],
        out_specs=[pl.BlockSpec((tm, ATTN_W), row), pl.BlockSpec((tm, 2 * KV_W), row)])
    return pl.pallas_call(
        functools.partial(_qkv_kernel, na), grid_spec=gs, name="qkv_rope",
        out_shape=[jax.ShapeDtypeStruct((n, ATTN_W), BF16), jax.ShapeDtypeStruct((n, 2 * KV_W), BF16)],
        compiler_params=_params())(meta, xa, xb, g_mix, w_qkv, *tables)


def _attn_kernel(meta_ref, sink_ref, q_ref, kvp_ref, kvm_ref, kvn_ref, o_ref, kcat_ref):
    i = pl.program_id(0)
    kb = KEY_BLOCK
    tq = q_ref.shape[0]
    kcat_ref[0:kb, :] = kvp_ref[...]
    kcat_ref[kb:kb + tq, :] = kvm_ref[...]
    kcat_ref[kb + tq:kb + tq + kb, :] = kvn_ref[...]
    r = lax.broadcasted_iota(jnp.int32, (kb, 3 * kb), 0)
    c = lax.broadcasted_iota(jnp.int32, (kb, 3 * kb), 1)
    dlt = c - r
    klane = lax.broadcasted_iota(jnp.int32, (3 * kb, KV_W), 1)
    olane = lax.broadcasted_iota(jnp.int32, (GROUP * kb, KV_W), 1)
    nsub = tq // kb
    for s in range(nsub):
        first = meta_ref[1, i * nsub + s]
        last = meta_ref[2, i * nsub + s]
        lo = jnp.where(first == 1, kb, 0)
        hi = jnp.where(last == 1, 2 * kb, 3 * kb)
        valid = (dlt >= 0) & (dlt <= 2 * WINDOW) & (c >= lo) & (c < hi)
        bias = jnp.where(valid, 0.0, NEG_BIG)
        bias = jnp.concatenate([bias] * GROUP, axis=0)
        kw = kcat_ref[s * kb:(s + 3) * kb, :]
        kc = kw[:, 0:KV_W]
        vc = jnp.concatenate([kw[:, KV_W:2 * KV_W], jnp.ones((3 * kb, KV_W), BF16)], axis=1)
        q4 = jnp.concatenate([q_ref[s * kb:(s + 1) * kb, g * KV_W:(g + 1) * KV_W] for g in range(GROUP)],
                             axis=0)
        res = []
        for kh in range(N_KV_HEADS):
            own = (klane >= kh * HEAD_DIM) & (klane < (kh + 1) * HEAD_DIM)
            kk = jnp.where(own, kc, jnp.zeros_like(kc))
            sc = lax.dot_general(q4, kk, (((1,), (1,)), ((), ())), preferred_element_type=F32) + bias
            snk = jnp.concatenate([jnp.full((kb, 1), sink_ref[kh * GROUP + g], F32) for g in range(GROUP)],
                                  axis=0)
            m = jnp.maximum(jnp.max(sc, axis=-1, keepdims=True), snk)
            p = jnp.exp(sc - m).astype(BF16)
            pv = jnp.dot(p, vc, preferred_element_type=F32)
            denom = pv[:, KV_W:KV_W + 1] + jnp.exp(snk - m)
            res.append(pv[:, 0:KV_W] / denom)
        out = res[0]
        for kh in range(1, N_KV_HEADS):
            out = jnp.where(olane >= kh * HEAD_DIM, res[kh], out)
        out = out.astype(BF16)
        for g in range(GROUP):
            o_ref[s * kb:(s + 1) * kb, g * KV_W:(g + 1) * KV_W] = out[g * kb:(g + 1) * kb, :]


def _attn_call(q, kv, sink, groups):
    n = q.shape[0]
    tq, kb = TQ_ATTN, KEY_BLOCK
    per = tq // kb
    nkb = n // kb
    meta = _tile_meta(groups, kb)
    gs = pltpu.PrefetchScalarGridSpec(
        num_scalar_prefetch=1, grid=(n // tq,),
        in_specs=[pl.BlockSpec(memory_space=pltpu.SMEM),
                  pl.BlockSpec((tq, ATTN_W), lambda i, m: (i, 0)),
                  pl.BlockSpec((kb, 2 * KV_W), lambda i, m: (jnp.maximum(i * per - 1, 0), 0)),
                  pl.BlockSpec((tq, 2 * KV_W), lambda i, m: (i, 0)),
                  pl.BlockSpec((kb, 2 * KV_W), lambda i, m: (jnp.minimum((i + 1) * per, nkb - 1), 0))],
        out_specs=pl.BlockSpec((tq, ATTN_W), lambda i, m: (i, 0)),
        scratch_shapes=[pltpu.VMEM((tq + 2 * kb, 2 * KV_W), BF16)])
    return pl.pallas_call(
        _attn_kernel, grid_spec=gs, name="window_attn",
        out_shape=jax.ShapeDtypeStruct((n, ATTN_W), BF16),
        compiler_params=_params())(meta, sink, q, kv, kv, kv)


def _mixer_kernel(na, meta_ref, xa_ref, xpa_ref, xna_ref, xb_ref, xpb_ref, xnb_ref, o_ref, g_ref, w_ref,
                  cw_ref, wa_ref, wc_ref, wm_ref, h_ref):
    i = pl.program_id(0)
    tm = xa_ref.shape[0]
    g = g_ref[...]
    in_a = i < na
    x = jnp.where(in_a, xa_ref[...], xb_ref[...])
    xb = _rms(x, g).astype(BF16)
    proj = jnp.dot(xb, w_ref[...], preferred_element_type=F32)
    cb = proj[:, 0:CONV_W]
    u = proj[:, CONV_W:2 * CONV_W] * proj[:, 2 * CONV_W:3 * CONV_W]
    xh = jnp.concatenate([jnp.where(in_a, xpa_ref[...], xpb_ref[...]),
                          jnp.where(in_a, xna_ref[...], xnb_ref[...])], axis=0)
    xhb = _rms(xh, g).astype(BF16)
    ph = jnp.dot(xhb, w_ref[:, CONV_W:3 * CONV_W], preferred_element_type=F32)
    uh = ph[:, 0:CONV_W] * ph[:, CONV_W:2 * CONV_W]
    first = meta_ref[1, i]
    last = meta_ref[2, i]
    u_prev = jnp.where(first == 1, 0.0, uh[SUBLANES - 1:SUBLANES, :])
    u_next = jnp.where(last == 1, 0.0, uh[SUBLANES:SUBLANES + 1, :])
    row = lax.broadcasted_iota(jnp.int32, (tm, 1), 0)
    up = jnp.where(row == 0, u_prev, pltpu.roll(u, 1, 0))
    dn = jnp.where(row == tm - 1, u_next, pltpu.roll(u, tm - 1, 0))
    cw = cw_ref[...]
    y = up * cw[0:1, :] + u * cw[1:2, :] + dn * cw[2:3, :]
    conv = (cb * y).astype(BF16)
    conv_br = jnp.dot(conv, wc_ref[...], preferred_element_type=F32)
    attn_br = jnp.dot(o_ref[...], wa_ref[...], preferred_element_type=F32)
    g0 = jax.nn.sigmoid(proj[:, 3 * CONV_W:3 * CONV_W + D_MODEL])
    g1 = jax.nn.sigmoid(proj[:, 3 * CONV_W + D_MODEL:REST_W])
    merged = (g0 * attn_br + g1 * conv_br).astype(BF16)
    h_ref[...] = x + jnp.dot(merged, wm_ref[...], preferred_element_type=F32)


def _mixer_call(xa, xb, o, g_mix, w_rest, conv_w, w_attn_br, w_conv_br, w_mix_out, groups):
    tm = TM_MIX
    per = tm // SUBLANES
    na, nb = xa.shape[0] // tm, xb.shape[0] // tm
    n = xa.shape[0] + xb.shape[0]
    meta = _tile_meta(groups, tm)

    def x_specs(tile_of, n_tiles):
        last8 = n_tiles * per - 1
        return [pl.BlockSpec((tm, D_MODEL), lambda i, m: (jnp.clip(tile_of(i), 0, n_tiles - 1), 0)),
                pl.BlockSpec((SUBLANES, D_MODEL), lambda i, m: (jnp.clip(tile_of(i) * per - 1, 0, last8), 0)),
                pl.BlockSpec((SUBLANES, D_MODEL), lambda i, m: (jnp.clip((tile_of(i) + 1) * per, 0, last8), 0))]

    gs = pltpu.PrefetchScalarGridSpec(
        num_scalar_prefetch=1, grid=(n // tm,),
        in_specs=x_specs(lambda i: i, na) + x_specs(lambda i: i - na, nb) + [
                  pl.BlockSpec((tm, ATTN_W), lambda i, m: (i, 0)),
                  _const_spec((1, D_MODEL)),
                  _const_spec((D_MODEL, REST_W)),
                  _const_spec((3, CONV_W)),
                  _const_spec((ATTN_W, D_MODEL)),
                  _const_spec((CONV_W, D_MODEL)),
                  _const_spec((D_MODEL, D_MODEL))],
        out_specs=pl.BlockSpec((tm, D_MODEL), lambda i, m: (i, 0)))
    return pl.pallas_call(
        functools.partial(_mixer_kernel, na), grid_spec=gs, name="mixer",
        out_shape=jax.ShapeDtypeStruct((n, D_MODEL), F32),
        compiler_params=_params())(meta, xa, xa, xa, xb, xb, xb, o, g_mix, w_rest, conv_w, w_attn_br,
                                   w_conv_br, w_mix_out)


def _memkv_kernel(mem_ref, g_ref, w_ref, kv_ref):
    mn = _rms(mem_ref[0], g_ref[...]).astype(BF16)
    kv_ref[0] = jnp.dot(mn, w_ref[...], preferred_element_type=F32).astype(BF16)


def _memkv_call(mem, g_mem, w_xkv):
    nb = mem.shape[0]
    return pl.pallas_call(
        _memkv_kernel, grid=(nb,), name="mem_kv",
        in_specs=[pl.BlockSpec((1, N_MEM, D_MODEL), lambda b: (b, 0, 0)),
                  _const_spec((1, D_MODEL)),
                  _const_spec((D_MODEL, 2 * D_MODEL))],
        out_specs=pl.BlockSpec((1, N_MEM, 2 * D_MODEL), lambda b: (b, 0, 0)),
        out_shape=jax.ShapeDtypeStruct((nb, N_MEM, 2 * D_MODEL), BF16),
        compiler_params=_params())(mem, g_mem, w_xkv)


def _cross_kernel(meta_ref, h_ref, kv_ref, gc_ref, wq_ref, wo_ref, gm_ref, wr_ref, br_ref,
                  h2_ref, xs_ref, idx_ref):
    del meta_ref
    tm = h_ref.shape[0]
    h = h_ref[...]
    hn = _rms(h, gc_ref[...]).astype(BF16)
    q = (jnp.dot(hn, wq_ref[...], preferred_element_type=F32) * (X_HEAD_DIM ** -0.5)).astype(BF16)
    outs = []
    for hd in range(X_HEADS):
        qh = q[:, hd * X_HEAD_DIM:(hd + 1) * X_HEAD_DIM]
        kh = kv_ref[0, :, hd * X_HEAD_DIM:(hd + 1) * X_HEAD_DIM]
        vh = kv_ref[0, :, D_MODEL + hd * X_HEAD_DIM:D_MODEL + (hd + 1) * X_HEAD_DIM]
        s = lax.dot_general(qh, kh, (((1,), (1,)), ((), ())), preferred_element_type=F32)
        m = jnp.max(s, axis=-1, keepdims=True)
        p = jnp.exp(s - m)
        p = (p / jnp.sum(p, axis=-1, keepdims=True)).astype(BF16)
        outs.append(jnp.dot(p, vh, preferred_element_type=F32).astype(BF16))
    o = jnp.concatenate(outs, axis=1)
    h2 = h + jnp.dot(o, wo_ref[...], preferred_element_type=F32)
    h2_ref[...] = h2

    hn3 = _rms(h2, gm_ref[...])
    hi = hn3.astype(BF16)
    hi32 = hi.astype(F32)
    lo = (hn3 - hi32).astype(BF16)
    xs_ref[:, 0:D_MODEL] = hn3

    nt = (((1,), (1,)), ((), ()))
    r1 = lax.dot_general(wr_ref[...], hi, nt, preferred_element_type=F32)
    r2 = lax.dot_general(wr_ref[...], lo, nt, preferred_element_type=F32)
    e = N_EXPERTS
    logits = ((r2[e:2 * e] + r2[0:e]) + r1[e:2 * e]) + r1[0:e] + br_ref[...]
    eio = lax.broadcasted_iota(jnp.int32, (e, tm), 0)
    cur = logits
    vals, sels = [], []
    for _ in range(TOP_K):
        mx = jnp.max(cur, axis=0, keepdims=True)
        sel = jnp.min(jnp.where(cur == mx, eio, e), axis=0, keepdims=True)
        vals.append(mx)
        sels.append(sel)
        cur = jnp.where(eio == sel, -jnp.inf, cur)
    ex = [jnp.exp(v - vals[0]) for v in vals]
    tot = ex[0] + ex[1] + ex[2] + ex[3]
    idx_ref[...] = jnp.concatenate(sels, axis=0)
    gws = [x / tot for x in ex]
    w_hi = [g.astype(BF16).astype(F32) for g in gws]
    w_lo = [(g - h).astype(BF16).astype(F32) for g, h in zip(gws, w_hi)]
    wt = jnp.transpose(jnp.concatenate(w_hi + w_lo, axis=0))
    et = jnp.transpose(jnp.concatenate([s.astype(F32) for s in sels]
                                       + [jnp.zeros((SUBLANES - TOP_K, tm), F32)], axis=0))
    xs_ref[:, D_MODEL:XS_W] = jnp.concatenate(
        [wt, et, jnp.zeros((tm, LANES - 2 * SUBLANES), F32)], axis=1)


def _cross_call(h1, memkv, g_cross, w_xq, w_xo, g_moe, w_r2t, b_router, groups):
    n = h1.shape[0]
    tm = TM_CROSS
    meta = _tile_meta(groups, tm)
    row = lambda i, m: (i, 0)
    col = lambda i, m: (0, i)
    gs = pltpu.PrefetchScalarGridSpec(
        num_scalar_prefetch=1, grid=(n // tm,),
        in_specs=[pl.BlockSpec((tm, D_MODEL), row),
                  pl.BlockSpec((1, N_MEM, 2 * D_MODEL), lambda i, m: (m[3, i], 0, 0)),
                  _const_spec((1, D_MODEL)),
                  _const_spec((D_MODEL, D_MODEL)),
                  _const_spec((D_MODEL, D_MODEL)),
                  _const_spec((1, D_MODEL)),
                  _const_spec((2 * N_EXPERTS, D_MODEL)),
                  _const_spec((N_EXPERTS, 1))],
        out_specs=[pl.BlockSpec((tm, D_MODEL), row),
                   pl.BlockSpec((tm, XS_W), row),
                   pl.BlockSpec((TOP_K, tm), col)])
    return pl.pallas_call(
        _cross_kernel, grid_spec=gs, name="cross_router",
        out_shape=[jax.ShapeDtypeStruct((n, D_MODEL), F32),
                   jax.ShapeDtypeStruct((n, XS_W), F32),
                   jax.ShapeDtypeStruct((TOP_K, n), jnp.int32)],
        compiler_params=_params())(meta, h1, memkv, g_cross, w_xq, w_xo, g_moe, w_r2t, b_router)


LOG2_BM = BM_MOE.bit_length() - 1
assert 1 << LOG2_BM == BM_MOE
SEG_ALIGN = SUBLANES
LOG2_SEG = SEG_ALIGN.bit_length() - 1
COPY_ROWS = (4 * SEG_ALIGN, 2 * SEG_ALIGN, SEG_ALIGN)
K_ROWS = 1280
M_CHUNK = 640
assert 1 << LOG2_SEG == SEG_ALIGN and K_ROWS % M_CHUNK == 0 and K_ROWS // COPY_ROWS[0] <= LANES
assert K_ROWS >= TOP_K * TC_COMB + N_EXPERTS * (SEG_ALIGN - 1)
DESC_COUNT = 2 * len(COPY_ROWS)
assert DESC_COUNT < SUBLANES and all(a == 2 * b for a, b in zip(COPY_ROWS, COPY_ROWS[1:]))
AUX_W_HI, AUX_W_LO, AUX_EID = 0, TOP_K, 2 * TOP_K
XS_W = D_MODEL + LANES


def _sublane_scan(x, n):
    row = lax.broadcasted_iota(jnp.int32, x.shape, 0)
    sft = 1
    while sft < n:
        x = x + jnp.where(row >= sft, pltpu.roll(x, sft, 0), 0)
        sft *= 2
    return x


def _owner(ends, lane):
    ebl = lax.broadcasted_iota(jnp.int32, ends.shape, 0)
    owner = jnp.sum(jnp.where(ends <= lane, 1.0, 0.0), axis=0, keepdims=True).astype(jnp.int32)
    return ebl == owner


def _pick(sel, val):
    return jnp.sum(jnp.where(sel, val.astype(F32), 0.0), axis=0, keepdims=True).astype(jnp.int32)


def _plan_kernel(n_tiles, idx_ref, kpos_ref, desc_ref, meta_ref, tri_ref, acc_ref, run_ref):
    i = pl.program_id(0)
    e, tp = N_EXPERTS, TC_COMB
    nb_pad = meta_ref.shape[1]
    eio = lax.broadcasted_iota(jnp.int32, (e, tp), 0)
    idx = idx_ref[...]
    onehots = [eio == idx[k:k + 1, :] for k in range(TOP_K)]
    s = jnp.where(onehots[0], 1.0, 0.0)
    for k in range(1, TOP_K):
        s = s + jnp.where(onehots[k], 1.0, 0.0)
    cnt_t = jnp.sum(s, axis=1, keepdims=True).astype(jnp.int32)
    cnt = jnp.broadcast_to(cnt_t, (e, LANES))
    seg = lax.shift_left(lax.shift_right_logical(cnt + (SEG_ALIGN - 1), LOG2_SEG), LOG2_SEG)

    @pl.when(i == 0)
    def _():
        acc_ref[...] = jnp.zeros_like(acc_ref)
        r = lax.broadcasted_iota(jnp.int32, (tp, tp), 0)
        c = lax.broadcasted_iota(jnp.int32, (tp, tp), 1)
        tri_ref[...] = jnp.where(r < c, 1.0, 0.0).astype(BF16)

    @pl.when(i < n_tiles)
    def _():
        acc_ref[...] += seg

    @pl.when(i == n_tiles)
    def _():
        tot = acc_ref[...]
        pad = lax.shift_left(lax.shift_right_logical(tot + (BM_MOE - 1), LOG2_BM), LOG2_BM)
        pad_end = _sublane_scan(pad, e)
        pad_start = pad_end - pad
        run_ref[...] = pad_start
        bpos = lax.broadcasted_iota(jnp.int32, (e, nb_pad), 1) * BM_MOE
        ebl = lax.broadcasted_iota(jnp.int32, (e, nb_pad), 0)
        done = jnp.where(pad_end[:, 0:1] <= bpos, 1.0, 0.0)
        bexp = jnp.minimum(jnp.sum(done, axis=0, keepdims=True).astype(jnp.int32), e - 1)
        row_end = (pad_start + tot)[:, 0:1].astype(F32)
        rend_b = jnp.sum(jnp.where(ebl == bexp, row_end, 0.0), axis=0, keepdims=True).astype(jnp.int32)
        nvalid = jnp.clip(rend_b - bpos[0:1, :], 0, BM_MOE)
        total = jnp.sum(jnp.where(ebl == e - 1, pad_end[:, 0:1].astype(F32), 0.0), axis=0, keepdims=True)
        nused = lax.shift_right_logical(total.astype(jnp.int32), LOG2_BM)
        meta_ref[...] = jnp.concatenate(
            [bexp, nvalid, nused, jnp.zeros((SUBLANES - 3, nb_pad), jnp.int32)], axis=0)

    @pl.when(i >= n_tiles)
    def _():
        before = jnp.dot(s.astype(BF16), tri_ref[...], preferred_element_type=F32)
        kbase = _sublane_scan(seg, e) - seg
        pos = kbase[:, 0:1].astype(F32) + before
        kp = [jnp.sum(jnp.where(onehots[k], pos, 0.0), axis=0, keepdims=True).astype(jnp.int32)
              for k in range(TOP_K)]
        kpos_ref[0] = jnp.concatenate(kp + [jnp.zeros((SUBLANES - TOP_K, tp), jnp.int32)], axis=0)

        start = run_ref[...]
        lane = lax.broadcasted_iota(jnp.int32, (e, LANES), 1)
        lane1 = lane[0:1, :]
        lists = []
        counts = jnp.zeros((1, LANES), jnp.int32)
        done = jnp.zeros_like(seg)
        for ci, rows in enumerate(COPY_ROWS):
            n = lax.shift_right_logical(seg - done, rows.bit_length() - 1)
            end = _sublane_scan(n, e)
            sel = _owner(end, lane)
            off = _pick(sel, done) + (lane1 - _pick(sel, end - n)) * rows
            n_c = end[e - 1:e, :]
            ok = lane1 < n_c
            lists += [jnp.where(ok, _pick(sel, start) + off, 0), jnp.where(ok, _pick(sel, kbase) + off, 0)]
            counts = jnp.where(lane1 == ci, n_c, counts)
            done = done + n * rows
        desc_ref[0] = jnp.concatenate(
            lists + [counts, jnp.zeros((SUBLANES - DESC_COUNT - 1, LANES), jnp.int32)], axis=0)
        run_ref[...] = run_ref[...] + seg


def _plan_call(idx, n_blocks):
    n = idx.shape[1]
    tp = TC_COMB
    n_tiles = n // tp
    nb_pad = -(-n_blocks // LANES) * LANES
    tile = lambda i: (jnp.maximum(i - n_tiles, 0), 0, 0)
    return pl.pallas_call(
        functools.partial(_plan_kernel, n_tiles), grid=(2 * n_tiles,), name="route_plan",
        in_specs=[pl.BlockSpec((TOP_K, tp), lambda i: (0, i % n_tiles))],
        out_specs=[pl.BlockSpec((1, SUBLANES, tp), tile),
                   pl.BlockSpec((1, SUBLANES, LANES), tile),
                   pl.BlockSpec((SUBLANES, nb_pad), lambda i: (0, 0))],
        out_shape=[jax.ShapeDtypeStruct((n_tiles, SUBLANES, tp), jnp.int32),
                   jax.ShapeDtypeStruct((n_tiles, SUBLANES, LANES), jnp.int32),
                   jax.ShapeDtypeStruct((SUBLANES, nb_pad), jnp.int32)],
        scratch_shapes=[pltpu.VMEM((tp, tp), BF16), pltpu.VMEM((N_EXPERTS, LANES), jnp.int32),
                        pltpu.VMEM((N_EXPERTS, LANES), jnp.int32)],
        compiler_params=_params())(idx)


def _slot_hits(kp, iota):
    hit = jnp.where(iota == kp[0], 1.0, 0.0)
    for k in range(1, TOP_K):
        hit = hit + jnp.where(iota == kp[k], 1.0, 0.0)
    return hit


def _for_each_copy(count, fn):
    for ci, rows in enumerate(COPY_ROWS):
        def body(j, carry, ci=ci, rows=rows):
            fn(2 * ci, j, rows)
            return carry
        lax.fori_loop(0, count(ci), body, 0)


def _dispatch_kernel(n_tiles, n_blocks, nvalid_ref, desc_ref, kpos_ref, xs_ref, out_hbm, zero_ref, buf,
                     sems, zsem, issued_ref):
    i = pl.program_id(0)
    td = xs_ref.shape[0]
    bm = zero_ref.shape[0]
    cur = i % 2

    def copy(b, dst, src, rows):
        return pltpu.make_async_copy(buf.at[b, pl.ds(pl.multiple_of(src, SEG_ALIGN), rows), :],
                                     out_hbm.at[pl.ds(pl.multiple_of(dst, SEG_ALIGN), rows), :], sems.at[b])

    def drain(b):
        _for_each_copy(lambda ci: issued_ref[b, ci], lambda row, j, rows: copy(b, 0, 0, rows).wait())

    @pl.when(i == 0)
    def _():
        zero_ref[...] = jnp.zeros_like(zero_ref)

        def fill(b):
            return pltpu.make_async_copy(zero_ref, out_hbm.at[pl.ds(pl.multiple_of(b * bm, bm), bm), :], zsem)

        def start(b, carry):
            @pl.when(nvalid_ref[b] < bm)
            def _():
                fill(b).start()
            return carry

        def wait(b, carry):
            @pl.when(nvalid_ref[b] < bm)
            def _():
                fill(b).wait()
            return carry
        lax.fori_loop(0, n_blocks, start, 0)
        lax.fori_loop(0, n_blocks, wait, 0)

    @pl.when(i >= 2)
    def _():
        drain(cur)

    kp = kpos_ref[0]
    x = xs_ref[...].astype(BF16)
    for c in range(K_ROWS // M_CHUNK):
        rows = lax.broadcasted_iota(jnp.int32, (M_CHUNK, td), 0) + c * M_CHUNK
        sel = _slot_hits([kp[k:k + 1, :] for k in range(TOP_K)], rows).astype(BF16)
        buf[cur, c * M_CHUNK:(c + 1) * M_CHUNK, :] = jnp.dot(sel, x, preferred_element_type=F32)

    _for_each_copy(lambda ci: desc_ref[0, DESC_COUNT, ci],
                   lambda row, j, rows: copy(cur, desc_ref[0, row, j], desc_ref[0, row + 1, j], rows).start())
    for ci in range(len(COPY_ROWS)):
        issued_ref[cur, ci] = desc_ref[0, DESC_COUNT, ci]

    @pl.when(i == n_tiles - 1)
    def _():
        drain(cur)

        @pl.when(i >= 1)
        def _():
            drain(1 - cur)


def _dispatch_call(nvalid, desc, kpos, xs, n_blocks):
    n, width = xs.shape
    td, bm = TC_COMB, BM_MOE
    n_tiles = n // td
    gs = pltpu.PrefetchScalarGridSpec(
        num_scalar_prefetch=1, grid=(n_tiles,),
        in_specs=[pl.BlockSpec((1, SUBLANES, LANES), lambda i, nv: (i, 0, 0), memory_space=pltpu.SMEM),
                  pl.BlockSpec((1, SUBLANES, td), lambda i, nv: (i, 0, 0)),
                  pl.BlockSpec((td, width), lambda i, nv: (i, 0))],
        out_specs=pl.BlockSpec(memory_space=pl.ANY),
        scratch_shapes=[pltpu.VMEM((bm, width), xs.dtype), pltpu.VMEM((2, K_ROWS, width), xs.dtype),
                        pltpu.SemaphoreType.DMA((2,)), pltpu.SemaphoreType.DMA(()),
                        pltpu.SMEM((2, len(COPY_ROWS)), jnp.int32)])
    return pl.pallas_call(
        functools.partial(_dispatch_kernel, n_tiles, n_blocks), grid_spec=gs, name="moe_dispatch",
        out_shape=jax.ShapeDtypeStruct((n_blocks * bm, width), xs.dtype),
        compiler_params=_params())(nvalid, desc, kpos, xs)


def _ffn_kernel(bexp_ref, nused_ref, x_ref, wg_ref, bg_ref, wu_ref, bu_ref, wd_ref, bd_ref,
                y_ref, wgb, wub, wdb):
    b = pl.program_id(0)
    nused = nused_ref[0]

    @pl.when((b == 0) | (bexp_ref[b] != bexp_ref[jnp.maximum(b - 1, 0)]))
    def _():
        wgb[...] = wg_ref[0].astype(BF16)
        wub[...] = wu_ref[0].astype(BF16)
        wdb[...] = wd_ref[0].astype(BF16)

    @pl.when(b < nused)
    def _():
        x = x_ref[:, 0:D_MODEL].astype(BF16)
        aux = x_ref[:, D_MODEL:XS_W]
        eid = bexp_ref[b].astype(F32)
        row_w = jnp.zeros((x.shape[0], 1), F32)
        for k in range(TOP_K):
            w_k = aux[:, AUX_W_HI + k:AUX_W_HI + k + 1] + aux[:, AUX_W_LO + k:AUX_W_LO + k + 1]
            row_w = row_w + jnp.where(aux[:, AUX_EID + k:AUX_EID + k + 1] == eid, w_k, 0.0)

        def proj(w_ref, b_ref):
            return jnp.dot(x, w_ref[...], preferred_element_type=F32) + b_ref[0]

        a = jnp.minimum(proj(wgb, bg_ref), SWIGLU_LIMIT)
        u = jnp.clip(proj(wub, bu_ref), -SWIGLU_LIMIT, SWIGLU_LIMIT)
        hid = (a * jax.nn.sigmoid(SWIGLU_ALPHA * a) * (u + 1.0)).astype(BF16)
        y_ref[...] = (jnp.dot(hid, wdb[...], preferred_element_type=F32) + bd_ref[0]) * row_w

    @pl.when(b >= nused)
    def _():
        y_ref[...] = jnp.zeros_like(y_ref)


def _ffn_call(bexp, nused, xs_sorted, w_gate, b_gate, w_up, b_up, w_down, b_down):
    bm = BM_MOE
    n_blocks = xs_sorted.shape[0] // bm
    wspec = lambda r, c: pl.BlockSpec((1, r, c), lambda b, be, nu: (be[b], 0, 0))
    gs = pltpu.PrefetchScalarGridSpec(
        num_scalar_prefetch=2, grid=(n_blocks,),
        in_specs=[pl.BlockSpec((bm, XS_W), lambda b, be, nu: (jnp.minimum(b, nu[0] - 1), 0)),
                  wspec(D_MODEL, D_FF), wspec(1, D_FF), wspec(D_MODEL, D_FF), wspec(1, D_FF),
                  wspec(D_FF, D_MODEL), wspec(1, D_MODEL)],
        out_specs=pl.BlockSpec((bm, D_MODEL), lambda b, be, nu: (b, 0)),
        scratch_shapes=[pltpu.VMEM((D_MODEL, D_FF), BF16), pltpu.VMEM((D_MODEL, D_FF), BF16),
                        pltpu.VMEM((D_FF, D_MODEL), BF16)])
    return pl.pallas_call(
        _ffn_kernel, grid_spec=gs, name="expert_ffn",
        out_shape=jax.ShapeDtypeStruct((n_blocks * bm, D_MODEL), F32),
        compiler_params=_params())(bexp, nused, xs_sorted, w_gate, b_gate, w_up, b_up, w_down, b_down)


def _combine_kernel(nt, desc_ref, descn_ref, kpos_ref, y_hbm, h_ref, gf_ref, out_ref, ybuf, sems):
    i = pl.program_id(0)
    tc = h_ref.shape[0]
    cur = i % 2

    def copy(b, dst, src, rows):
        return pltpu.make_async_copy(y_hbm.at[pl.ds(pl.multiple_of(dst, SEG_ALIGN), rows), :],
                                     ybuf.at[b, pl.ds(pl.multiple_of(src, SEG_ALIGN), rows), :], sems.at[b])

    def fetch(b, ref):
        _for_each_copy(lambda ci: ref[0, DESC_COUNT, ci],
                       lambda row, j, rows: copy(b, ref[0, row, j], ref[0, row + 1, j], rows).start())

    @pl.when(i == 0)
    def _():
        ybuf[...] = jnp.zeros_like(ybuf)
        fetch(0, desc_ref)

    @pl.when(i + 1 < nt)
    def _():
        fetch(1 - cur, descn_ref)

    _for_each_copy(lambda ci: desc_ref[0, DESC_COUNT, ci], lambda row, j, rows: copy(cur, 0, 0, rows).wait())

    kpt = jnp.transpose(kpos_ref[0].astype(F32))
    kp = [kpt[:, k:k + 1] for k in range(TOP_K)]
    acc = h_ref[...]
    for c in range(K_ROWS // M_CHUNK):
        cols = (lax.broadcasted_iota(jnp.int32, (tc, M_CHUNK), 1) + c * M_CHUNK).astype(F32)
        pick = _slot_hits(kp, cols).astype(BF16)
        y = ybuf[cur, c * M_CHUNK:(c + 1) * M_CHUNK, :]
        y_hi = y.astype(BF16)
        y_lo = (y - y_hi.astype(F32)).astype(BF16)
        acc = acc + jnp.dot(pick, y_hi, preferred_element_type=F32) + jnp.dot(pick, y_lo, preferred_element_type=F32)
    out_ref[...] = _rms(acc, gf_ref[...])


def _combine_call(desc, kpos, y, h2, g_final, tile0, n_tiles):
    tc = TC_COMB
    last = tile0 + n_tiles - 1
    return pl.pallas_call(
        functools.partial(_combine_kernel, n_tiles), grid=(n_tiles,), name="moe_combine",
        in_specs=[pl.BlockSpec((1, SUBLANES, LANES), lambda i: (tile0 + i, 0, 0), memory_space=pltpu.SMEM),
                  pl.BlockSpec((1, SUBLANES, LANES), lambda i: (jnp.minimum(tile0 + i + 1, last), 0, 0),
                               memory_space=pltpu.SMEM),
                  pl.BlockSpec((1, SUBLANES, tc), lambda i: (tile0 + i, 0, 0)),
                  pl.BlockSpec(memory_space=pl.ANY),
                  pl.BlockSpec((tc, D_MODEL), lambda i: (tile0 + i, 0)),
                  _const_spec((1, D_MODEL))],
        out_specs=pl.BlockSpec((tc, D_MODEL), lambda i: (i, 0)),
        out_shape=jax.ShapeDtypeStruct((n_tiles * tc, D_MODEL), F32),
        scratch_shapes=[pltpu.VMEM((2, K_ROWS, D_MODEL), F32), pltpu.SemaphoreType.DMA((2,))],
        compiler_params=_params())(desc, desc, kpos, y, h2, g_final)


def _forward(xs, mems, g_mix, w_in, sink, conv_w, w_attn_br, w_conv_br, w_mix_out, g_cross, g_mem,
             w_xq, w_xkv, w_xo, g_moe, w_router, b_router, w_gate, b_gate, w_up, b_up, w_down,
             b_down, g_final):
    groups = [(x.shape[0], x.shape[1]) for x in xs]
    for _, s in groups:
        assert s % max(TM_QKV, TQ_ATTN, TM_MIX, TM_CROSS, TC_COMB) == 0
    assert len(xs) == 2, "two request groups"
    xa, xb = (x.reshape(-1, D_MODEL) for x in xs)
    mem = jnp.concatenate(mems, axis=0)
    n = xa.shape[0] + xb.shape[0]
    tables = _rope_tables(max(s for _, s in groups))
    assert w_in.shape[0] == 1, "single-layer trunk: the final norm is fused into the combine kernel"
    l = 0
    row2 = lambda v: v.reshape(1, -1)
    w_in_b = w_in[l].astype(BF16)
    perm = np.array([(kv * GROUP + g) * HEAD_DIM + d
                     for g in range(GROUP) for kv in range(N_KV_HEADS) for d in range(HEAD_DIM)])
    w_qkv = jnp.concatenate([w_in_b[:, :ATTN_W][:, perm], w_in_b[:, ATTN_W:QKV_W]], axis=1)
    q, kv = _qkv_call(xa, xb, row2(g_mix[l]), w_qkv, tables, groups)
    o = _attn_call(q, kv, sink[l], groups)
    h = _mixer_call(xa, xb, o, row2(g_mix[l]), w_in_b[:, QKV_W:], conv_w[l],
                    w_attn_br[l].astype(BF16)[perm, :], w_conv_br[l].astype(BF16),
                    w_mix_out[l].astype(BF16), groups)
    memkv = _memkv_call(mem, row2(g_mem[l]), w_xkv[l].astype(BF16))
    wr = w_router[l]
    wr_hi = wr.astype(BF16)
    wr_lo = (wr - wr_hi.astype(F32)).astype(BF16)
    w_r2t = jnp.concatenate([wr_hi.T, wr_lo.T], axis=0)
    h2, xs_rows, idx = _cross_call(h, memkv, row2(g_cross[l]), w_xq[l].astype(BF16), w_xo[l].astype(BF16),
                                   row2(g_moe[l]), w_r2t, b_router[l].reshape(-1, 1), groups)
    seg_rows = TOP_K * n + (n // TC_COMB) * N_EXPERTS * (SEG_ALIGN - 1)
    n_blocks = -(-seg_rows // BM_MOE) + N_EXPERTS
    kpos, desc, meta = _plan_call(idx, n_blocks)
    xs_sorted = _dispatch_call(meta[1], desc, kpos, xs_rows, n_blocks)
    y = _ffn_call(meta[0], meta[2, 0:1], xs_sorted, w_gate[l], b_gate[l][:, None, :],
                  w_up[l], b_up[l][:, None, :], w_down[l], b_down[l][:, None, :])
    tc = TC_COMB
    outs, t0 = [], 0
    for nb, s in groups:
        nt = nb * s // tc
        outs.append(_combine_call(desc, kpos, y, h2, row2(g_final), t0, nt))
        t0 += nt
    return tuple(o.reshape(x.shape) for o, x in zip(outs, xs))


def kernel(x_prompt, x_sample, mem_prompt, mem_sample, g_mix, w_in, sink, conv_w, w_attn_br, w_conv_br,
           w_mix_out, g_cross, g_mem, w_xq, w_xkv, w_xo, g_moe, w_router, b_router, w_gate, b_gate,
           w_up, b_up, w_down, b_down, g_final):
    return _forward([x_prompt, x_sample], [mem_prompt, mem_sample], g_mix, w_in, sink, conv_w,
                    w_attn_br, w_conv_br, w_mix_out, g_cross, g_mem, w_xq, w_xkv, w_xo, g_moe,
                    w_router, b_router, w_gate, b_gate, w_up, b_up, w_down, b_down, g_final)
```

```python
import functools

import numpy as np
import jax
import jax.numpy as jnp
from jax import lax
from jax.experimental import pallas as pl
from jax.experimental.pallas import tpu as pltpu

F32 = jnp.float32
BF16 = jnp.bfloat16

D_MODEL = 1024
N_HEADS = 8
N_KV_HEADS = 2
HEAD_DIM = 64
GROUP = N_HEADS // N_KV_HEADS
ATTN_W = N_HEADS * HEAD_DIM
KV_W = N_KV_HEADS * HEAD_DIM
QKV_W = ATTN_W + 2 * KV_W
WINDOW = 128
ROT_DIM = HEAD_DIM // 4
ROPE_THETA = 500000.0
CONV_W = D_MODEL // 2
REST_W = 3 * CONV_W + 2 * D_MODEL
N_MEM = 256
X_HEADS = 4
X_HEAD_DIM = D_MODEL // X_HEADS
N_EXPERTS = 32
TOP_K = 4
D_FF = D_MODEL
SWIGLU_ALPHA = 1.702
SWIGLU_LIMIT = 7.0
EPS = 1e-5

LANES = 128
SUBLANES = 8
KEY_BLOCK = WINDOW
TM_QKV = 1024
TQ_ATTN = 1024
TM_MIX = 512
TM_CROSS = 1024
BM_MOE = 512
TC_COMB = 256
NEG_BIG = -1e30
VMEM_LIMIT = 56 * 1024 * 1024


def _rms(x, g):
    var = jnp.mean(x * x, axis=-1, keepdims=True)
    return x * lax.rsqrt(var + EPS) * g


def _tile_meta(groups, tile):
    pos, first, last, bidx = [], [], [], []
    b0 = 0
    for nb, s in groups:
        per = s // tile
        for b in range(nb):
            for j in range(per):
                pos.append(j)
                first.append(int(j == 0))
                last.append(int(j == per - 1))
                bidx.append(b0 + b)
        b0 += nb
    return jnp.asarray(np.array([pos, first, last, bidx], dtype=np.int32))


def _const_spec(shape):
    nd = len(shape)
    return pl.BlockSpec(shape, lambda *_: (0,) * nd, pipeline_mode=pl.Buffered(1))


def _params(vmem=VMEM_LIMIT):
    return pltpu.CompilerParams(dimension_semantics=("arbitrary",), vmem_limit_bytes=vmem)


def _qkv_kernel(na, meta_ref, xa_ref, xb_ref, g_ref, w_ref, cos_ref, sa_ref, sb_ref, q_ref, kv_ref):
    del meta_ref
    x = jnp.where(pl.program_id(0) < na, xa_ref[...], xb_ref[...])
    xn = _rms(x, g_ref[...]).astype(BF16)
    proj = jnp.dot(xn, w_ref[...], preferred_element_type=F32)
    c, sa, sb = cos_ref[...], sa_ref[...], sb_ref[...]
    n_rot = (ATTN_W + KV_W) // LANES
    for gi in range(n_rot):
        p = proj[:, gi * LANES:(gi + 1) * LANES]
        r = p * c + pltpu.roll(p, LANES - ROT_DIM // 2, 1) * sa + pltpu.roll(p, ROT_DIM // 2, 1) * sb
        if gi < ATTN_W // LANES:
            q_ref[:, gi * LANES:(gi + 1) * LANES] = (r * (HEAD_DIM ** -0.5)).astype(BF16)
        else:
            kv_ref[:, 0:KV_W] = r.astype(BF16)
    kv_ref[:, KV_W:2 * KV_W] = proj[:, ATTN_W + KV_W:QKV_W].astype(BF16)


def _rope_tables(s_max):
    half = ROT_DIM // 2
    inv_freq = ROPE_THETA ** (-(jnp.arange(half, dtype=F32) * 2.0) / ROT_DIM)
    ang = jnp.arange(s_max, dtype=F32)[:, None] * inv_freq[None, :]
    cos, sin = jnp.cos(ang), jnp.sin(ang)
    d = np.arange(LANES) % HEAD_DIM
    j = d % half
    cos_l, sin_l = cos[:, j], sin[:, j]
    rot = jnp.asarray(d < ROT_DIM)[None, :]
    lo = jnp.asarray(d < half)[None, :]
    hi = jnp.asarray((d >= half) & (d < ROT_DIM))[None, :]
    c = jnp.where(rot, cos_l, 1.0)
    sa = jnp.where(lo, -sin_l, 0.0)
    sb = jnp.where(hi, sin_l, 0.0)
    return c, sa, sb


def _qkv_call(xa, xb, g_mix, w_qkv, tables, groups):
    tm = TM_QKV
    na = xa.shape[0] // tm
    n = xa.shape[0] + xb.shape[0]
    meta = _tile_meta(groups, tm)
    row = lambda i, m: (i, 0)
    tab = lambda i, m: (m[0, i], 0)
    gs = pltpu.PrefetchScalarGridSpec(
        num_scalar_prefetch=1, grid=(n // tm,),
        in_specs=[pl.BlockSpec((tm, D_MODEL), lambda i, m: (jnp.minimum(i, na - 1), 0)),
                  pl.BlockSpec((tm, D_MODEL), lambda i, m: (jnp.maximum(i - na, 0), 0)),
                  _const_spec((1, D_MODEL)),
                  _const_spec((D_MODEL, QKV_W)),
                  pl.BlockSpec((tm, LANES), tab), pl.BlockSpec((tm, LANES), tab),
                  pl.BlockSpec((tm, LANES), tab)],
        out_specs=[pl.BlockSpec((tm, ATTN_W), row), pl.BlockSpec((tm, 2 * KV_W), row)])
    return pl.pallas_call(
        functools.partial(_qkv_kernel, na), grid_spec=gs, name="qkv_rope",
        out_shape=[jax.ShapeDtypeStruct((n, ATTN_W), BF16), jax.ShapeDtypeStruct((n, 2 * KV_W), BF16)],
        compiler_params=_params())(meta, xa, xb, g_mix, w_qkv, *tables)


def _attn_kernel(meta_ref, sink_ref, q_ref, kvp_ref, kvm_ref, kvn_ref, o_ref, kcat_ref):
    i = pl.program_id(0)
    kb = KEY_BLOCK
    tq = q_ref.shape[0]
    kcat_ref[0:kb, :] = kvp_ref[...]
    kcat_ref[kb:kb + tq, :] = kvm_ref[...]
    kcat_ref[kb + tq:kb + tq + kb, :] = kvn_ref[...]
    r = lax.broadcasted_iota(jnp.int32, (kb, 3 * kb), 0)
    c = lax.broadcasted_iota(jnp.int32, (kb, 3 * kb), 1)
    dlt = c - r
    klane = lax.broadcasted_iota(jnp.int32, (3 * kb, KV_W), 1)
    olane = lax.broadcasted_iota(jnp.int32, (GROUP * kb, KV_W), 1)
    nsub = tq // kb
    for s in range(nsub):
        first = meta_ref[1, i * nsub + s]
        last = meta_ref[2, i * nsub + s]
        lo = jnp.where(first == 1, kb, 0)
        hi = jnp.where(last == 1, 2 * kb, 3 * kb)
        valid = (dlt >= 0) & (dlt <= 2 * WINDOW) & (c >= lo) & (c < hi)
        bias = jnp.where(valid, 0.0, NEG_BIG)
        bias = jnp.concatenate([bias] * GROUP, axis=0)
        kw = kcat_ref[s * kb:(s + 3) * kb, :]
        kc = kw[:, 0:KV_W]
        vc = jnp.concatenate([kw[:, KV_W:2 * KV_W], jnp.ones((3 * kb, KV_W), BF16)], axis=1)
        q4 = jnp.concatenate([q_ref[s * kb:(s + 1) * kb, g * KV_W:(g + 1) * KV_W] for g in range(GROUP)],
                             axis=0)
        res = []
        for kh in range(N_KV_HEADS):
            own = (klane >= kh * HEAD_DIM) & (klane < (kh + 1) * HEAD_DIM)
            kk = jnp.where(own, kc, jnp.zeros_like(kc))
            sc = lax.dot_general(q4, kk, (((1,), (1,)), ((), ())), preferred_element_type=F32) + bias
            snk = jnp.concatenate([jnp.full((kb, 1), sink_ref[kh * GROUP + g], F32) for g in range(GROUP)],
                                  axis=0)
            m = jnp.maximum(jnp.max(sc, axis=-1, keepdims=True), snk)
            p = jnp.exp(sc - m).astype(BF16)
            pv = jnp.dot(p, vc, preferred_element_type=F32)
            denom = pv[:, KV_W:KV_W + 1] + jnp.exp(snk - m)
            res.append(pv[:, 0:KV_W] / denom)
        out = res[0]
        for kh in range(1, N_KV_HEADS):
            out = jnp.where(olane >= kh * HEAD_DIM, res[kh], out)
        out = out.astype(BF16)
        for g in range(GROUP):
            o_ref[s * kb:(s + 1) * kb, g * KV_W:(g + 1) * KV_W] = out[g * kb:(g + 1) * kb, :]


def _attn_call(q, kv, sink, groups):
    n = q.shape[0]
    tq, kb = TQ_ATTN, KEY_BLOCK
    per = tq // kb
    nkb = n // kb
    meta = _tile_meta(groups, kb)
    gs = pltpu.PrefetchScalarGridSpec(
        num_scalar_prefetch=1, grid=(n // tq,),
        in_specs=[pl.BlockSpec(memory_space=pltpu.SMEM),
                  pl.BlockSpec((tq, ATTN_W), lambda i, m: (i, 0)),
                  pl.BlockSpec((kb, 2 * KV_W), lambda i, m: (jnp.maximum(i * per - 1, 0), 0)),
                  pl.BlockSpec((tq, 2 * KV_W), lambda i, m: (i, 0)),
                  pl.BlockSpec((kb, 2 * KV_W), lambda i, m: (jnp.minimum((i + 1) * per, nkb - 1), 0))],
        out_specs=pl.BlockSpec((tq, ATTN_W), lambda i, m: (i, 0)),
        scratch_shapes=[pltpu.VMEM((tq + 2 * kb, 2 * KV_W), BF16)])
    return pl.pallas_call(
        _attn_kernel, grid_spec=gs, name="window_attn",
        out_shape=jax.ShapeDtypeStruct((n, ATTN_W), BF16),
        compiler_params=_params())(meta, sink, q, kv, kv, kv)


def _mixer_kernel(na, meta_ref, xa_ref, xpa_ref, xna_ref, xb_ref, xpb_ref, xnb_ref, o_ref, g_ref, w_ref,
                  cw_ref, wa_ref, wc_ref, wm_ref, h_ref):
    i = pl.program_id(0)
    tm = xa_ref.shape[0]
    g = g_ref[...]
    in_a = i < na
    x = jnp.where(in_a, xa_ref[...], xb_ref[...])
    xb = _rms(x, g).astype(BF16)
    proj = jnp.dot(xb, w_ref[...], preferred_element_type=F32)
    cb = proj[:, 0:CONV_W]
    u = proj[:, CONV_W:2 * CONV_W] * proj[:, 2 * CONV_W:3 * CONV_W]
    xh = jnp.concatenate([jnp.where(in_a, xpa_ref[...], xpb_ref[...]),
                          jnp.where(in_a, xna_ref[...], xnb_ref[...])], axis=0)
    xhb = _rms(xh, g).astype(BF16)
    ph = jnp.dot(xhb, w_ref[:, CONV_W:3 * CONV_W], preferred_element_type=F32)
    uh = ph[:, 0:CONV_W] * ph[:, CONV_W:2 * CONV_W]
    first = meta_ref[1, i]
    last = meta_ref[2, i]
    u_prev = jnp.where(first == 1, 0.0, uh[SUBLANES - 1:SUBLANES, :])
    u_next = jnp.where(last == 1, 0.0, uh[SUBLANES:SUBLANES + 1, :])
    row = lax.broadcasted_iota(jnp.int32, (tm, 1), 0)
    up = jnp.where(row == 0, u_prev, pltpu.roll(u, 1, 0))
    dn = jnp.where(row == tm - 1, u_next, pltpu.roll(u, tm - 1, 0))
    cw = cw_ref[...]
    y = up * cw[0:1, :] + u * cw[1:2, :] + dn * cw[2:3, :]
    conv = (cb * y).astype(BF16)
    conv_br = jnp.dot(conv, wc_ref[...], preferred_element_type=F32)
    attn_br = jnp.dot(o_ref[...], wa_ref[...], preferred_element_type=F32)
    g0 = jax.nn.sigmoid(proj[:, 3 * CONV_W:3 * CONV_W + D_MODEL])
    g1 = jax.nn.sigmoid(proj[:, 3 * CONV_W + D_MODEL:REST_W])
    merged = (g0 * attn_br + g1 * conv_br).astype(BF16)
    h_ref[...] = x + jnp.dot(merged, wm_ref[...], preferred_element_type=F32)


def _mixer_call(xa, xb, o, g_mix, w_rest, conv_w, w_attn_br, w_conv_br, w_mix_out, groups):
    tm = TM_MIX
    per = tm // SUBLANES
    na, nb = xa.shape[0] // tm, xb.shape[0] // tm
    n = xa.shape[0] + xb.shape[0]
    meta = _tile_meta(groups, tm)

    def x_specs(tile_of, n_tiles):
        last8 = n_tiles * per - 1
        return [pl.BlockSpec((tm, D_MODEL), lambda i, m: (jnp.clip(tile_of(i), 0, n_tiles - 1), 0)),
                pl.BlockSpec((SUBLANES, D_MODEL), lambda i, m: (jnp.clip(tile_of(i) * per - 1, 0, last8), 0)),
                pl.BlockSpec((SUBLANES, D_MODEL), lambda i, m: (jnp.clip((tile_of(i) + 1) * per, 0, last8), 0))]

    gs = pltpu.PrefetchScalarGridSpec(
        num_scalar_prefetch=1, grid=(n // tm,),
        in_specs=x_specs(lambda i: i, na) + x_specs(lambda i: i - na, nb) + [
                  pl.BlockSpec((tm, ATTN_W), lambda i, m: (i, 0)),
                  _const_spec((1, D_MODEL)),
                  _const_spec((D_MODEL, REST_W)),
                  _const_spec((3, CONV_W)),
                  _const_spec((ATTN_W, D_MODEL)),
                  _const_spec((CONV_W, D_MODEL)),
                  _const_spec((D_MODEL, D_MODEL))],
        out_specs=pl.BlockSpec((tm, D_MODEL), lambda i, m: (i, 0)))
    return pl.pallas_call(
        functools.partial(_mixer_kernel, na), grid_spec=gs, name="mixer",
        out_shape=jax.ShapeDtypeStruct((n, D_MODEL), F32),
        compiler_params=_params())(meta, xa, xa, xa, xb, xb, xb, o, g_mix, w_rest, conv_w, w_attn_br,
                                   w_conv_br, w_mix_out)


def _memkv_kernel(mem_ref, g_ref, w_ref, kv_ref):
    mn = _rms(mem_ref[0], g_ref[...]).astype(BF16)
    kv_ref[0] = jnp.dot(mn, w_ref[...], preferred_element_type=F32).astype(BF16)


def _memkv_call(mem, g_mem, w_xkv):
    nb = mem.shape[0]
    return pl.pallas_call(
        _memkv_kernel, grid=(nb,), name="mem_kv",
        in_specs=[pl.BlockSpec((1, N_MEM, D_MODEL), lambda b: (b, 0, 0)),
                  _const_spec((1, D_MODEL)),
                  _const_spec((D_MODEL, 2 * D_MODEL))],
        out_specs=pl.BlockSpec((1, N_MEM, 2 * D_MODEL), lambda b: (b, 0, 0)),
        out_shape=jax.ShapeDtypeStruct((nb, N_MEM, 2 * D_MODEL), BF16),
        compiler_params=_params())(mem, g_mem, w_xkv)


def _cross_kernel(meta_ref, h_ref, kv_ref, gc_ref, wq_ref, wo_ref, gm_ref, wr_ref, br_ref,
                  h2_ref, xs_ref, idx_ref):
    del meta_ref
    tm = h_ref.shape[0]
    h = h_ref[...]
    hn = _rms(h, gc_ref[...]).astype(BF16)
    q = (jnp.dot(hn, wq_ref[...], preferred_element_type=F32) * (X_HEAD_DIM ** -0.5)).astype(BF16)
    outs = []
    for hd in range(X_HEADS):
        qh = q[:, hd * X_HEAD_DIM:(hd + 1) * X_HEAD_DIM]
        kh = kv_ref[0, :, hd * X_HEAD_DIM:(hd + 1) * X_HEAD_DIM]
        vh = kv_ref[0, :, D_MODEL + hd * X_HEAD_DIM:D_MODEL + (hd + 1) * X_HEAD_DIM]
        s = lax.dot_general(qh, kh, (((1,), (1,)), ((), ())), preferred_element_type=F32)
        m = jnp.max(s, axis=-1, keepdims=True)
        p = jnp.exp(s - m)
        p = (p / jnp.sum(p, axis=-1, keepdims=True)).astype(BF16)
        outs.append(jnp.dot(p, vh, preferred_element_type=F32).astype(BF16))
    o = jnp.concatenate(outs, axis=1)
    h2 = h + jnp.dot(o, wo_ref[...], preferred_element_type=F32)
    h2_ref[...] = h2

    hn3 = _rms(h2, gm_ref[...])
    hi = hn3.astype(BF16)
    hi32 = hi.astype(F32)
    lo = (hn3 - hi32).astype(BF16)
    xs_ref[:, 0:D_MODEL] = hn3

    nt = (((1,), (1,)), ((), ()))
    r1 = lax.dot_general(wr_ref[...], hi, nt, preferred_element_type=F32)
    r2 = lax.dot_general(wr_ref[...], lo, nt, preferred_element_type=F32)
    e = N_EXPERTS
    logits = ((r2[e:2 * e] + r2[0:e]) + r1[e:2 * e]) + r1[0:e] + br_ref[...]
    eio = lax.broadcasted_iota(jnp.int32, (e, tm), 0)
    cur = logits
    vals, sels = [], []
    for _ in range(TOP_K):
        mx = jnp.max(cur, axis=0, keepdims=True)
        sel = jnp.min(jnp.where(cur == mx, eio, e), axis=0, keepdims=True)
        vals.append(mx)
        sels.append(sel)
        cur = jnp.where(eio == sel, -jnp.inf, cur)
    ex = [jnp.exp(v - vals[0]) for v in vals]
    tot = ex[0] + ex[1] + ex[2] + ex[3]
    idx_ref[...] = jnp.concatenate(sels, axis=0)
    gws = [x / tot for x in ex]
    w_hi = [g.astype(BF16).astype(F32) for g in gws]
    w_lo = [(g - h).astype(BF16).astype(F32) for g, h in zip(gws, w_hi)]
    wt = jnp.transpose(jnp.concatenate(w_hi + w_lo, axis=0))
    et = jnp.transpose(jnp.concatenate([s.astype(F32) for s in sels]
                                       + [jnp.zeros((SUBLANES - TOP_K, tm), F32)], axis=0))
    xs_ref[:, D_MODEL:XS_W] = jnp.concatenate(
        [wt, et, jnp.zeros((tm, LANES - 2 * SUBLANES), F32)], axis=1)


def _cross_call(h1, memkv, g_cross, w_xq, w_xo, g_moe, w_r2t, b_router, groups):
    n = h1.shape[0]
    tm = TM_CROSS
    meta = _tile_meta(groups, tm)
    row = lambda i, m: (i, 0)
    col = lambda i, m: (0, i)
    gs = pltpu.PrefetchScalarGridSpec(
        num_scalar_prefetch=1, grid=(n // tm,),
        in_specs=[pl.BlockSpec((tm, D_MODEL), row),
                  pl.BlockSpec((1, N_MEM, 2 * D_MODEL), lambda i, m: (m[3, i], 0, 0)),
                  _const_spec((1, D_MODEL)),
                  _const_spec((D_MODEL, D_MODEL)),
                  _const_spec((D_MODEL, D_MODEL)),
                  _const_spec((1, D_MODEL)),
                  _const_spec((2 * N_EXPERTS, D_MODEL)),
                  _const_spec((N_EXPERTS, 1))],
        out_specs=[pl.BlockSpec((tm, D_MODEL), row),
                   pl.BlockSpec((tm, XS_W), row),
                   pl.BlockSpec((TOP_K, tm), col)])
    return pl.pallas_call(
        _cross_kernel, grid_spec=gs, name="cross_router",
        out_shape=[jax.ShapeDtypeStruct((n, D_MODEL), F32),
                   jax.ShapeDtypeStruct((n, XS_W), F32),
                   jax.ShapeDtypeStruct((TOP_K, n), jnp.int32)],
        compiler_params=_params())(meta, h1, memkv, g_cross, w_xq, w_xo, g_moe, w_r2t, b_router)


LOG2_BM = BM_MOE.bit_length() - 1
assert 1 << LOG2_BM == BM_MOE
SEG_ALIGN = SUBLANES
LOG2_SEG = SEG_ALIGN.bit_length() - 1
COPY_ROWS = (4 * SEG_ALIGN, 2 * SEG_ALIGN, SEG_ALIGN)
K_ROWS = 1280
M_CHUNK = 640
assert 1 << LOG2_SEG == SEG_ALIGN and K_ROWS % M_CHUNK == 0 and K_ROWS // COPY_ROWS[0] <= LANES
assert K_ROWS >= TOP_K * TC_COMB + N_EXPERTS * (SEG_ALIGN - 1)
DESC_COUNT = 2 * len(COPY_ROWS)
assert DESC_COUNT < SUBLANES and all(a == 2 * b for a, b in zip(COPY_ROWS, COPY_ROWS[1:]))
AUX_W_HI, AUX_W_LO, AUX_EID = 0, TOP_K, 2 * TOP_K
XS_W = D_MODEL + LANES


def _sublane_scan(x, n):
    row = lax.broadcasted_iota(jnp.int32, x.shape, 0)
    sft = 1
    while sft < n:
        x = x + jnp.where(row >= sft, pltpu.roll(x, sft, 0), 0)
        sft *= 2
    return x


def _owner(ends, lane):
    ebl = lax.broadcasted_iota(jnp.int32, ends.shape, 0)
    owner = jnp.sum(jnp.where(ends <= lane, 1.0, 0.0), axis=0, keepdims=True).astype(jnp.int32)
    return ebl == owner


def _pick(sel, val):
    return jnp.sum(jnp.where(sel, val.astype(F32), 0.0), axis=0, keepdims=True).astype(jnp.int32)


def _plan_kernel(n_tiles, idx_ref, kpos_ref, desc_ref, meta_ref, tri_ref, acc_ref, run_ref):
    i = pl.program_id(0)
    e, tp = N_EXPERTS, TC_COMB
    nb_pad = meta_ref.shape[1]
    eio = lax.broadcasted_iota(jnp.int32, (e, tp), 0)
    idx = idx_ref[...]
    onehots = [eio == idx[k:k + 1, :] for k in range(TOP_K)]
    s = jnp.where(onehots[0], 1.0, 0.0)
    for k in range(1, TOP_K):
        s = s + jnp.where(onehots[k], 1.0, 0.0)
    cnt_t = jnp.sum(s, axis=1, keepdims=True).astype(jnp.int32)
    cnt = jnp.broadcast_to(cnt_t, (e, LANES))
    seg = lax.shift_left(lax.shift_right_logical(cnt + (SEG_ALIGN - 1), LOG2_SEG), LOG2_SEG)

    @pl.when(i == 0)
    def _():
        acc_ref[...] = jnp.zeros_like(acc_ref)
        r = lax.broadcasted_iota(jnp.int32, (tp, tp), 0)
        c = lax.broadcasted_iota(jnp.int32, (tp, tp), 1)
        tri_ref[...] = jnp.where(r < c, 1.0, 0.0).astype(BF16)

    @pl.when(i < n_tiles)
    def _():
        acc_ref[...] += seg

    @pl.when(i == n_tiles)
    def _():
        tot = acc_ref[...]
        pad = lax.shift_left(lax.shift_right_logical(tot + (BM_MOE - 1), LOG2_BM), LOG2_BM)
        pad_end = _sublane_scan(pad, e)
        pad_start = pad_end - pad
        run_ref[...] = pad_start
        bpos = lax.broadcasted_iota(jnp.int32, (e, nb_pad), 1) * BM_MOE
        ebl = lax.broadcasted_iota(jnp.int32, (e, nb_pad), 0)
        done = jnp.where(pad_end[:, 0:1] <= bpos, 1.0, 0.0)
        bexp = jnp.minimum(jnp.sum(done, axis=0, keepdims=True).astype(jnp.int32), e - 1)
        row_end = (pad_start + tot)[:, 0:1].astype(F32)
        rend_b = jnp.sum(jnp.where(ebl == bexp, row_end, 0.0), axis=0, keepdims=True).astype(jnp.int32)
        nvalid = jnp.clip(rend_b - bpos[0:1, :], 0, BM_MOE)
        total = jnp.sum(jnp.where(ebl == e - 1, pad_end[:, 0:1].astype(F32), 0.0), axis=0, keepdims=True)
        nused = lax.shift_right_logical(total.astype(jnp.int32), LOG2_BM)
        meta_ref[...] = jnp.concatenate(
            [bexp, nvalid, nused, jnp.zeros((SUBLANES - 3, nb_pad), jnp.int32)], axis=0)

    @pl.when(i >= n_tiles)
    def _():
        before = jnp.dot(s.astype(BF16), tri_ref[...], preferred_element_type=F32)
        kbase = _sublane_scan(seg, e) - seg
        pos = kbase[:, 0:1].astype(F32) + before
        kp = [jnp.sum(jnp.where(onehots[k], pos, 0.0), axis=0, keepdims=True).astype(jnp.int32)
              for k in range(TOP_K)]
        kpos_ref[0] = jnp.concatenate(kp + [jnp.zeros((SUBLANES - TOP_K, tp), jnp.int32)], axis=0)

        start = run_ref[...]
        lane = lax.broadcasted_iota(jnp.int32, (e, LANES), 1)
        lane1 = lane[0:1, :]
        lists = []
        counts = jnp.zeros((1, LANES), jnp.int32)
        done = jnp.zeros_like(seg)
        for ci, rows in enumerate(COPY_ROWS):
            n = lax.shift_right_logical(seg - done, rows.bit_length() - 1)
            end = _sublane_scan(n, e)
            sel = _owner(end, lane)
            off = _pick(sel, done) + (lane1 - _pick(sel, end - n)) * rows
            n_c = end[e - 1:e, :]
            ok = lane1 < n_c
            lists += [jnp.where(ok, _pick(sel, start) + off, 0), jnp.where(ok, _pick(sel, kbase) + off, 0)]
            counts = jnp.where(lane1 == ci, n_c, counts)
            done = done + n * rows
        desc_ref[0] = jnp.concatenate(
            lists + [counts, jnp.zeros((SUBLANES - DESC_COUNT - 1, LANES), jnp.int32)], axis=0)
        run_ref[...] = run_ref[...] + seg


def _plan_call(idx, n_blocks):
    n = idx.shape[1]
    tp = TC_COMB
    n_tiles = n // tp
    nb_pad = -(-n_blocks // LANES) * LANES
    tile = lambda i: (jnp.maximum(i - n_tiles, 0), 0, 0)
    return pl.pallas_call(
        functools.partial(_plan_kernel, n_tiles), grid=(2 * n_tiles,), name="route_plan",
        in_specs=[pl.BlockSpec((TOP_K, tp), lambda i: (0, i % n_tiles))],
        out_specs=[pl.BlockSpec((1, SUBLANES, tp), tile),
                   pl.BlockSpec((1, SUBLANES, LANES), tile),
                   pl.BlockSpec((SUBLANES, nb_pad), lambda i: (0, 0))],
        out_shape=[jax.ShapeDtypeStruct((n_tiles, SUBLANES, tp), jnp.int32),
                   jax.ShapeDtypeStruct((n_tiles, SUBLANES, LANES), jnp.int32),
                   jax.ShapeDtypeStruct((SUBLANES, nb_pad), jnp.int32)],
        scratch_shapes=[pltpu.VMEM((tp, tp), BF16), pltpu.VMEM((N_EXPERTS, LANES), jnp.int32),
                        pltpu.VMEM((N_EXPERTS, LANES), jnp.int32)],
        compiler_params=_params())(idx)


def _slot_hits(kp, iota):
    hit = jnp.where(iota == kp[0], 1.0, 0.0)
    for k in range(1, TOP_K):
        hit = hit + jnp.where(iota == kp[k], 1.0, 0.0)
    return hit


def _for_each_copy(count, fn):
    for ci, rows in enumerate(COPY_ROWS):
        def body(j, carry, ci=ci, rows=rows):
            fn(2 * ci, j, rows)
            return carry
        lax.fori_loop(0, count(ci), body, 0)


def _dispatch_kernel(n_tiles, n_blocks, nvalid_ref, desc_ref, kpos_ref, xs_ref, out_hbm, zero_ref, buf,
                     sems, zsem, issued_ref):
    i = pl.program_id(0)
    td = xs_ref.shape[0]
    bm = zero_ref.shape[0]
    cur = i % 2

    def copy(b, dst, src, rows):
        return pltpu.make_async_copy(buf.at[b, pl.ds(pl.multiple_of(src, SEG_ALIGN), rows), :],
                                     out_hbm.at[pl.ds(pl.multiple_of(dst, SEG_ALIGN), rows), :], sems.at[b])

    def drain(b):
        _for_each_copy(lambda ci: issued_ref[b, ci], lambda row, j, rows: copy(b, 0, 0, rows).wait())

    @pl.when(i == 0)
    def _():
        zero_ref[...] = jnp.zeros_like(zero_ref)

        def fill(b):
            return pltpu.make_async_copy(zero_ref, out_hbm.at[pl.ds(pl.multiple_of(b * bm, bm), bm), :], zsem)

        def start(b, carry):
            @pl.when(nvalid_ref[b] < bm)
            def _():
                fill(b).start()
            return carry

        def wait(b, carry):
            @pl.when(nvalid_ref[b] < bm)
            def _():
                fill(b).wait()
            return carry
        lax.fori_loop(0, n_blocks, start, 0)
        lax.fori_loop(0, n_blocks, wait, 0)

    @pl.when(i >= 2)
    def _():
        drain(cur)

    kp = kpos_ref[0]
    x = xs_ref[...].astype(BF16)
    for c in range(K_ROWS // M_CHUNK):
        rows = lax.broadcasted_iota(jnp.int32, (M_CHUNK, td), 0) + c * M_CHUNK
        sel = _slot_hits([kp[k:k + 1, :] for k in range(TOP_K)], rows).astype(BF16)
        buf[cur, c * M_CHUNK:(c + 1) * M_CHUNK, :] = jnp.dot(sel, x, preferred_element_type=F32)

    _for_each_copy(lambda ci: desc_ref[0, DESC_COUNT, ci],
                   lambda row, j, rows: copy(cur, desc_ref[0, row, j], desc_ref[0, row + 1, j], rows).start())
    for ci in range(len(COPY_ROWS)):
        issued_ref[cur, ci] = desc_ref[0, DESC_COUNT, ci]

    @pl.when(i == n_tiles - 1)
    def _():
        drain(cur)

        @pl.when(i >= 1)
        def _():
            drain(1 - cur)


def _dispatch_call(nvalid, desc, kpos, xs, n_blocks):
    n, width = xs.shape
    td, bm = TC_COMB, BM_MOE
    n_tiles = n // td
    gs = pltpu.PrefetchScalarGridSpec(
        num_scalar_prefetch=1, grid=(n_tiles,),
        in_specs=[pl.BlockSpec((1, SUBLANES, LANES), lambda i, nv: (i, 0, 0), memory_space=pltpu.SMEM),
                  pl.BlockSpec((1, SUBLANES, td), lambda i, nv: (i, 0, 0)),
                  pl.BlockSpec((td, width), lambda i, nv: (i, 0))],
        out_specs=pl.BlockSpec(memory_space=pl.ANY),
        scratch_shapes=[pltpu.VMEM((bm, width), xs.dtype), pltpu.VMEM((2, K_ROWS, width), xs.dtype),
                        pltpu.SemaphoreType.DMA((2,)), pltpu.SemaphoreType.DMA(()),
                        pltpu.SMEM((2, len(COPY_ROWS)), jnp.int32)])
    return pl.pallas_call(
        functools.partial(_dispatch_kernel, n_tiles, n_blocks), grid_spec=gs, name="moe_dispatch",
        out_shape=jax.ShapeDtypeStruct((n_blocks * bm, width), xs.dtype),
        compiler_params=_params())(nvalid, desc, kpos, xs)


def _ffn_kernel(bexp_ref, nused_ref, x_ref, wg_hbm, bg_ref, wu_hbm, bu_ref, wd_hbm, bd_ref,
                y_ref, wf32, wbf, wsems, slot_ref):
    b = pl.program_id(0)
    nused = nused_ref[0]
    e = bexp_ref[b]
    wgb, wub, wdb = wbf.at[0], wbf.at[1], wbf.at[2]

    def fetch(expert, slot):
        return [pltpu.make_async_copy(src.at[expert], wf32.at[slot, j], wsems.at[slot])
                for j, src in enumerate((wg_hbm, wu_hbm, wd_hbm))]

    @pl.when(b == 0)
    def _():
        slot_ref[0] = 0
        for d in fetch(e, 0):
            d.start()

    @pl.when((b < nused) & ((b == 0) | (e != bexp_ref[jnp.maximum(b - 1, 0)])))
    def _():
        slot = slot_ref[0]
        for d in fetch(e, slot):
            d.wait()
        for j in range(3):
            wbf[j] = wf32[slot, j].astype(BF16)
        nxt = lax.while_loop(lambda j: (j < nused) & (bexp_ref[jnp.minimum(j, nused - 1)] == e),
                             lambda j: j + 1, b + 1)

        @pl.when(nxt < nused)
        def _():
            for d in fetch(bexp_ref[nxt], 1 - slot):
                d.start()
        slot_ref[0] = 1 - slot

    @pl.when(b < nused)
    def _():
        x = x_ref[:, 0:D_MODEL].astype(BF16)
        aux = x_ref[:, D_MODEL:XS_W]
        eid = bexp_ref[b].astype(F32)
        row_w = jnp.zeros((x.shape[0], 1), F32)
        for k in range(TOP_K):
            w_k = aux[:, AUX_W_HI + k:AUX_W_HI + k + 1] + aux[:, AUX_W_LO + k:AUX_W_LO + k + 1]
            row_w = row_w + jnp.where(aux[:, AUX_EID + k:AUX_EID + k + 1] == eid, w_k, 0.0)

        def proj(w_ref, b_ref):
            return jnp.dot(x, w_ref[...], preferred_element_type=F32) + b_ref[0]

        a = jnp.minimum(proj(wgb, bg_ref), SWIGLU_LIMIT)
        u = jnp.clip(proj(wub, bu_ref), -SWIGLU_LIMIT, SWIGLU_LIMIT)
        hid = (a * jax.nn.sigmoid(SWIGLU_ALPHA * a) * (u + 1.0)).astype(BF16)
        y_ref[...] = (jnp.dot(hid, wdb[...], preferred_element_type=F32) + bd_ref[0]) * row_w

    @pl.when(b >= nused)
    def _():
        y_ref[...] = jnp.zeros_like(y_ref)


def _ffn_call(bexp, nused, xs_sorted, w_gate, b_gate, w_up, b_up, w_down, b_down):
    bm = BM_MOE
    n_blocks = xs_sorted.shape[0] // bm
    assert D_FF == D_MODEL, "the three expert matrices share one staging shape"
    bspec = lambda c: pl.BlockSpec((1, 1, c), lambda b, be, nu: (be[b], 0, 0))
    whole = pl.BlockSpec(memory_space=pl.ANY)
    gs = pltpu.PrefetchScalarGridSpec(
        num_scalar_prefetch=2, grid=(n_blocks,),
        in_specs=[pl.BlockSpec((bm, XS_W), lambda b, be, nu: (jnp.minimum(b, nu[0] - 1), 0)),
                  whole, bspec(D_FF), whole, bspec(D_FF), whole, bspec(D_MODEL)],
        out_specs=pl.BlockSpec((bm, D_MODEL), lambda b, be, nu: (b, 0)),
        scratch_shapes=[pltpu.VMEM((2, 3, D_MODEL, D_FF), F32), pltpu.VMEM((3, D_MODEL, D_FF), BF16),
                        pltpu.SemaphoreType.DMA((2,)), pltpu.SMEM((1,), jnp.int32)])
    return pl.pallas_call(
        _ffn_kernel, grid_spec=gs, name="expert_ffn",
        out_shape=jax.ShapeDtypeStruct((n_blocks * bm, D_MODEL), F32),
        compiler_params=_params())(bexp, nused, xs_sorted, w_gate, b_gate, w_up, b_up, w_down, b_down)


def _combine_kernel(nt, desc_ref, descn_ref, kpos_ref, y_hbm, h_ref, gf_ref, out_ref, ybuf, sems):
    i = pl.program_id(0)
    tc = h_ref.shape[0]
    cur = i % 2

    def copy(b, dst, src, rows):
        return pltpu.make_async_copy(y_hbm.at[pl.ds(pl.multiple_of(dst, SEG_ALIGN), rows), :],
                                     ybuf.at[b, pl.ds(pl.multiple_of(src, SEG_ALIGN), rows), :], sems.at[b])

    def fetch(b, ref):
        _for_each_copy(lambda ci: ref[0, DESC_COUNT, ci],
                       lambda row, j, rows: copy(b, ref[0, row, j], ref[0, row + 1, j], rows).start())

    @pl.when(i == 0)
    def _():
        ybuf[...] = jnp.zeros_like(ybuf)
        fetch(0, desc_ref)

    @pl.when(i + 1 < nt)
    def _():
        fetch(1 - cur, descn_ref)

    _for_each_copy(lambda ci: desc_ref[0, DESC_COUNT, ci], lambda row, j, rows: copy(cur, 0, 0, rows).wait())

    kpt = jnp.transpose(kpos_ref[0].astype(F32))
    kp = [kpt[:, k:k + 1] for k in range(TOP_K)]
    acc = h_ref[...]
    for c in range(K_ROWS // M_CHUNK):
        cols = (lax.broadcasted_iota(jnp.int32, (tc, M_CHUNK), 1) + c * M_CHUNK).astype(F32)
        pick = _slot_hits(kp, cols).astype(BF16)
        y = ybuf[cur, c * M_CHUNK:(c + 1) * M_CHUNK, :].astype(BF16)
        acc = acc + jnp.dot(pick, y, preferred_element_type=F32)
    out_ref[...] = _rms(acc, gf_ref[...])


def _combine_call(desc, kpos, y, h2, g_final, tile0, n_tiles):
    tc = TC_COMB
    last = tile0 + n_tiles - 1
    return pl.pallas_call(
        functools.partial(_combine_kernel, n_tiles), grid=(n_tiles,), name="moe_combine",
        in_specs=[pl.BlockSpec((1, SUBLANES, LANES), lambda i: (tile0 + i, 0, 0), memory_space=pltpu.SMEM),
                  pl.BlockSpec((1, SUBLANES, LANES), lambda i: (jnp.minimum(tile0 + i + 1, last), 0, 0),
                               memory_space=pltpu.SMEM),
                  pl.BlockSpec((1, SUBLANES, tc), lambda i: (tile0 + i, 0, 0)),
                  pl.BlockSpec(memory_space=pl.ANY),
                  pl.BlockSpec((tc, D_MODEL), lambda i: (tile0 + i, 0)),
                  _const_spec((1, D_MODEL))],
        out_specs=pl.BlockSpec((tc, D_MODEL), lambda i: (i, 0)),
        out_shape=jax.ShapeDtypeStruct((n_tiles * tc, D_MODEL), F32),
        scratch_shapes=[pltpu.VMEM((2, K_ROWS, D_MODEL), F32), pltpu.SemaphoreType.DMA((2,))],
        compiler_params=_params())(desc, desc, kpos, y, h2, g_final)


def _forward(xs, mems, g_mix, w_in, sink, conv_w, w_attn_br, w_conv_br, w_mix_out, g_cross, g_mem,
             w_xq, w_xkv, w_xo, g_moe, w_router, b_router, w_gate, b_gate, w_up, b_up, w_down,
             b_down, g_final):
    groups = [(x.shape[0], x.shape[1]) for x in xs]
    for _, s in groups:
        assert s % max(TM_QKV, TQ_ATTN, TM_MIX, TM_CROSS, TC_COMB) == 0
    assert len(xs) == 2, "two request groups"
    xa, xb = (x.reshape(-1, D_MODEL) for x in xs)
    mem = jnp.concatenate(mems, axis=0)
    n = xa.shape[0] + xb.shape[0]
    tables = _rope_tables(max(s for _, s in groups))
    assert w_in.shape[0] == 1, "single-layer trunk: the final norm is fused into the combine kernel"
    l = 0
    row2 = lambda v: v.reshape(1, -1)
    w_in_b = w_in[l].astype(BF16)
    perm = np.array([(kv * GROUP + g) * HEAD_DIM + d
                     for g in range(GROUP) for kv in range(N_KV_HEADS) for d in range(HEAD_DIM)])
    w_qkv = jnp.concatenate([w_in_b[:, :ATTN_W][:, perm], w_in_b[:, ATTN_W:QKV_W]], axis=1)
    q, kv = _qkv_call(xa, xb, row2(g_mix[l]), w_qkv, tables, groups)
    o = _attn_call(q, kv, sink[l], groups)
    h = _mixer_call(xa, xb, o, row2(g_mix[l]), w_in_b[:, QKV_W:], conv_w[l],
                    w_attn_br[l].astype(BF16)[perm, :], w_conv_br[l].astype(BF16),
                    w_mix_out[l].astype(BF16), groups)
    memkv = _memkv_call(mem, row2(g_mem[l]), w_xkv[l].astype(BF16))
    wr = w_router[l]
    wr_hi = wr.astype(BF16)
    wr_lo = (wr - wr_hi.astype(F32)).astype(BF16)
    w_r2t = jnp.concatenate([wr_hi.T, wr_lo.T], axis=0)
    h2, xs_rows, idx = _cross_call(h, memkv, row2(g_cross[l]), w_xq[l].astype(BF16), w_xo[l].astype(BF16),
                                   row2(g_moe[l]), w_r2t, b_router[l].reshape(-1, 1), groups)
    seg_rows = TOP_K * n + (n // TC_COMB) * N_EXPERTS * (SEG_ALIGN - 1)
    n_blocks = -(-seg_rows // BM_MOE) + N_EXPERTS
    kpos, desc, meta = _plan_call(idx, n_blocks)
    xs_sorted = _dispatch_call(meta[1], desc, kpos, xs_rows, n_blocks)
    y = _ffn_call(meta[0], meta[2, 0:1], xs_sorted, w_gate[l], b_gate[l][:, None, :],
                  w_up[l], b_up[l][:, None, :], w_down[l], b_down[l][:, None, :])
    tc = TC_COMB
    outs, t0 = [], 0
    for nb, s in groups:
        nt = nb * s // tc
        outs.append(_combine_call(desc, kpos, y, h2, row2(g_final), t0, nt))
        t0 += nt
    return tuple(o.reshape(x.shape) for o, x in zip(outs, xs))


def kernel(x_prompt, x_sample, mem_prompt, mem_sample, g_mix, w_in, sink, conv_w, w_attn_br, w_conv_br,
           w_mix_out, g_cross, g_mem, w_xq, w_xkv, w_xo, g_moe, w_router, b_router, w_gate, b_gate,
           w_up, b_up, w_down, b_down, g_final):
    return _forward([x_prompt, x_sample], [mem_prompt, mem_sample], g_mix, w_in, sink, conv_w,
                    w_attn_br, w_conv_br, w_mix_out, g_cross, g_mem, w_xq, w_xkv, w_xo, g_moe,
                    w_router, b_router, w_gate, b_gate, w_up, b_up, w_down, b_down, g_final)
```

```python
import functools

import numpy as np
import jax
import jax.numpy as jnp
from jax import lax
from jax.experimental import pallas as pl
from jax.experimental.pallas import tpu as pltpu

F32 = jnp.float32
BF16 = jnp.bfloat16

D_MODEL = 1024
N_HEADS = 8
N_KV_HEADS = 2
HEAD_DIM = 64
GROUP = N_HEADS // N_KV_HEADS
ATTN_W = N_HEADS * HEAD_DIM
KV_W = N_KV_HEADS * HEAD_DIM
QKV_W = ATTN_W + 2 * KV_W
WINDOW = 128
ROT_DIM = HEAD_DIM // 4
ROPE_THETA = 500000.0
CONV_W = D_MODEL // 2
REST_W = 3 * CONV_W + 2 * D_MODEL
N_MEM = 256
X_HEADS = 4
X_HEAD_DIM = D_MODEL // X_HEADS
N_EXPERTS = 32
TOP_K = 4
D_FF = D_MODEL
SWIGLU_ALPHA = 1.702
SWIGLU_LIMIT = 7.0
EPS = 1e-5

LANES = 128
SUBLANES = 8
KEY_BLOCK = WINDOW
TM_QKV = 1024
TQ_ATTN = 1024
TM_MIX = 512
TM_CROSS = 1024
BM_MOE = 512
TC_COMB = 256
NEG_BIG = -1e30
VMEM_LIMIT = 56 * 1024 * 1024


def _rms(x, g):
    var = jnp.mean(x * x, axis=-1, keepdims=True)
    return x * lax.rsqrt(var + EPS) * g


def _tile_meta(groups, tile):
    pos, first, last, bidx = [], [], [], []
    b0 = 0
    for nb, s in groups:
        per = s // tile
        for b in range(nb):
            for j in range(per):
                pos.append(j)
                first.append(int(j == 0))
                last.append(int(j == per - 1))
                bidx.append(b0 + b)
        b0 += nb
    return jnp.asarray(np.array([pos, first, last, bidx], dtype=np.int32))


def _const_spec(shape):
    nd = len(shape)
    return pl.BlockSpec(shape, lambda *_: (0,) * nd, pipeline_mode=pl.Buffered(1))


def _params(vmem=VMEM_LIMIT):
    return pltpu.CompilerParams(dimension_semantics=("arbitrary",), vmem_limit_bytes=vmem)


def _qkv_kernel(na, meta_ref, xa_ref, xb_ref, g_ref, w_ref, cos_ref, sa_ref, sb_ref, q_ref, kv_ref):
    del meta_ref
    x = jnp.where(pl.program_id(0) < na, xa_ref[...], xb_ref[...])
    xn = _rms(x, g_ref[...]).astype(BF16)
    proj = jnp.dot(xn, w_ref[...], preferred_element_type=F32)
    c, sa, sb = cos_ref[...], sa_ref[...], sb_ref[...]
    n_rot = (ATTN_W + KV_W) // LANES
    for gi in range(n_rot):
        p = proj[:, gi * LANES:(gi + 1) * LANES]
        r = p * c + pltpu.roll(p, LANES - ROT_DIM // 2, 1) * sa + pltpu.roll(p, ROT_DIM // 2, 1) * sb
        if gi < ATTN_W // LANES:
            q_ref[:, gi * LANES:(gi + 1) * LANES] = (r * (HEAD_DIM ** -0.5)).astype(BF16)
        else:
            kv_ref[:, 0:KV_W] = r.astype(BF16)
    kv_ref[:, KV_W:2 * KV_W] = proj[:, ATTN_W + KV_W:QKV_W].astype(BF16)


def _rope_tables(s_max):
    half = ROT_DIM // 2
    inv_freq = ROPE_THETA ** (-(jnp.arange(half, dtype=F32) * 2.0) / ROT_DIM)
    ang = jnp.arange(s_max, dtype=F32)[:, None] * inv_freq[None, :]
    cos, sin = jnp.cos(ang), jnp.sin(ang)
    d = np.arange(LANES) % HEAD_DIM
    j = d % half
    cos_l, sin_l = cos[:, j], sin[:, j]
    rot = jnp.asarray(d < ROT_DIM)[None, :]
    lo = jnp.asarray(d < half)[None, :]
    hi = jnp.asarray((d >= half) & (d < ROT_DIM))[None, :]
    c = jnp.where(rot, cos_l, 1.0)
    sa = jnp.where(lo, -sin_l, 0.0)
    sb = jnp.where(hi, sin_l, 0.0)
    return c, sa, sb


def _qkv_call(xa, xb, g_mix, w_qkv, tables, groups):
    tm = TM_QKV
    na = xa.shape[0] // tm
    n = xa.shape[0] + xb.shape[0]
    meta = _tile_meta(groups, tm)
    row = lambda i, m: (i, 0)
    tab = lambda i, m: (m[0, i], 0)
    gs = pltpu.PrefetchScalarGridSpec(
        num_scalar_prefetch=1, grid=(n // tm,),
        in_specs=[pl.BlockSpec((tm, D_MODEL), lambda i, m: (jnp.minimum(i, na - 1), 0)),
                  pl.BlockSpec((tm, D_MODEL), lambda i, m: (jnp.maximum(i - na, 0), 0)),
                  _const_spec((1, D_MODEL)),
                  _const_spec((D_MODEL, QKV_W)),
                  pl.BlockSpec((tm, LANES), tab), pl.BlockSpec((tm, LANES), tab),
                  pl.BlockSpec((tm, LANES), tab)],
        out_specs=[pl.BlockSpec((tm, ATTN_W), row), pl.BlockSpec((tm, 2 * KV_W), row)])
    return pl.pallas_call(
        functools.partial(_qkv_kernel, na), grid_spec=gs, name="qkv_rope",
        out_shape=[jax.ShapeDtypeStruct((n, ATTN_W), BF16), jax.ShapeDtypeStruct((n, 2 * KV_W), BF16)],
        compiler_params=_params())(meta, xa, xb, g_mix, w_qkv, *tables)


def _attn_kernel(meta_ref, sink_ref, q_ref, kvp_ref, kvm_ref, kvn_ref, o_ref, kcat_ref):
    i = pl.program_id(0)
    kb = KEY_BLOCK
    tq = q_ref.shape[0]
    kcat_ref[0:kb, :] = kvp_ref[...]
    kcat_ref[kb:kb + tq, :] = kvm_ref[...]
    kcat_ref[kb + tq:kb + tq + kb, :] = kvn_ref[...]
    r = lax.broadcasted_iota(jnp.int32, (kb, 3 * kb), 0)
    c = lax.broadcasted_iota(jnp.int32, (kb, 3 * kb), 1)
    dlt = c - r
    klane = lax.broadcasted_iota(jnp.int32, (3 * kb, KV_W), 1)
    olane = lax.broadcasted_iota(jnp.int32, (GROUP * kb, KV_W), 1)
    nsub = tq // kb
    for s in range(nsub):
        first = meta_ref[1, i * nsub + s]
        last = meta_ref[2, i * nsub + s]
        lo = jnp.where(first == 1, kb, 0)
        hi = jnp.where(last == 1, 2 * kb, 3 * kb)
        valid = (dlt >= 0) & (dlt <= 2 * WINDOW) & (c >= lo) & (c < hi)
        bias = jnp.where(valid, 0.0, NEG_BIG)
        bias = jnp.concatenate([bias] * GROUP, axis=0)
        kw = kcat_ref[s * kb:(s + 3) * kb, :]
        kc = kw[:, 0:KV_W]
        vc = jnp.concatenate([kw[:, KV_W:2 * KV_W], jnp.ones((3 * kb, KV_W), BF16)], axis=1)
        q4 = jnp.concatenate([q_ref[s * kb:(s + 1) * kb, g * KV_W:(g + 1) * KV_W] for g in range(GROUP)],
                             axis=0)
        res = []
        for kh in range(N_KV_HEADS):
            own = (klane >= kh * HEAD_DIM) & (klane < (kh + 1) * HEAD_DIM)
            kk = jnp.where(own, kc, jnp.zeros_like(kc))
            sc = lax.dot_general(q4, kk, (((1,), (1,)), ((), ())), preferred_element_type=F32) + bias
            snk = jnp.concatenate([jnp.full((kb, 1), sink_ref[kh * GROUP + g], F32) for g in range(GROUP)],
                                  axis=0)
            m = jnp.maximum(jnp.max(sc, axis=-1, keepdims=True), snk)
            p = jnp.exp(sc - m).astype(BF16)
            pv = jnp.dot(p, vc, preferred_element_type=F32)
            denom = pv[:, KV_W:KV_W + 1] + jnp.exp(snk - m)
            res.append(pv[:, 0:KV_W] / denom)
        out = res[0]
        for kh in range(1, N_KV_HEADS):
            out = jnp.where(olane >= kh * HEAD_DIM, res[kh], out)
        out = out.astype(BF16)
        for g in range(GROUP):
            o_ref[s * kb:(s + 1) * kb, g * KV_W:(g + 1) * KV_W] = out[g * kb:(g + 1) * kb, :]


def _attn_call(q, kv, sink, groups):
    n = q.shape[0]
    tq, kb = TQ_ATTN, KEY_BLOCK
    per = tq // kb
    nkb = n // kb
    meta = _tile_meta(groups, kb)
    gs = pltpu.PrefetchScalarGridSpec(
        num_scalar_prefetch=1, grid=(n // tq,),
        in_specs=[pl.BlockSpec(memory_space=pltpu.SMEM),
                  pl.BlockSpec((tq, ATTN_W), lambda i, m: (i, 0)),
                  pl.BlockSpec((kb, 2 * KV_W), lambda i, m: (jnp.maximum(i * per - 1, 0), 0)),
                  pl.BlockSpec((tq, 2 * KV_W), lambda i, m: (i, 0)),
                  pl.BlockSpec((kb, 2 * KV_W), lambda i, m: (jnp.minimum((i + 1) * per, nkb - 1), 0))],
        out_specs=pl.BlockSpec((tq, ATTN_W), lambda i, m: (i, 0)),
        scratch_shapes=[pltpu.VMEM((tq + 2 * kb, 2 * KV_W), BF16)])
    return pl.pallas_call(
        _attn_kernel, grid_spec=gs, name="window_attn",
        out_shape=jax.ShapeDtypeStruct((n, ATTN_W), BF16),
        compiler_params=_params())(meta, sink, q, kv, kv, kv)


def _mixer_kernel(na, meta_ref, xa_ref, xpa_ref, xna_ref, xb_ref, xpb_ref, xnb_ref, o_ref, g_ref, w_ref,
                  cw_ref, wa_ref, wc_ref, wm_ref, h_ref):
    i = pl.program_id(0)
    tm = xa_ref.shape[0]
    g = g_ref[...]
    in_a = i < na
    x = jnp.where(in_a, xa_ref[...], xb_ref[...])
    xb = _rms(x, g).astype(BF16)
    proj = jnp.dot(xb, w_ref[...], preferred_element_type=F32)
    cb = proj[:, 0:CONV_W]
    u = proj[:, CONV_W:2 * CONV_W] * proj[:, 2 * CONV_W:3 * CONV_W]
    xh = jnp.concatenate([jnp.where(in_a, xpa_ref[...], xpb_ref[...]),
                          jnp.where(in_a, xna_ref[...], xnb_ref[...])], axis=0)
    xhb = _rms(xh, g).astype(BF16)
    ph = jnp.dot(xhb, w_ref[:, CONV_W:3 * CONV_W], preferred_element_type=F32)
    uh = ph[:, 0:CONV_W] * ph[:, CONV_W:2 * CONV_W]
    first = meta_ref[1, i]
    last = meta_ref[2, i]
    u_prev = jnp.where(first == 1, 0.0, uh[SUBLANES - 1:SUBLANES, :])
    u_next = jnp.where(last == 1, 0.0, uh[SUBLANES:SUBLANES + 1, :])
    row = lax.broadcasted_iota(jnp.int32, (tm, 1), 0)
    up = jnp.where(row == 0, u_prev, pltpu.roll(u, 1, 0))
    dn = jnp.where(row == tm - 1, u_next, pltpu.roll(u, tm - 1, 0))
    cw = cw_ref[...]
    y = up * cw[0:1, :] + u * cw[1:2, :] + dn * cw[2:3, :]
    conv = (cb * y).astype(BF16)
    conv_br = jnp.dot(conv, wc_ref[...], preferred_element_type=F32)
    attn_br = jnp.dot(o_ref[...], wa_ref[...], preferred_element_type=F32)
    g0 = jax.nn.sigmoid(proj[:, 3 * CONV_W:3 * CONV_W + D_MODEL])
    g1 = jax.nn.sigmoid(proj[:, 3 * CONV_W + D_MODEL:REST_W])
    merged = (g0 * attn_br + g1 * conv_br).astype(BF16)
    h_ref[...] = x + jnp.dot(merged, wm_ref[...], preferred_element_type=F32)


def _mixer_call(xa, xb, o, g_mix, w_rest, conv_w, w_attn_br, w_conv_br, w_mix_out, groups):
    tm = TM_MIX
    per = tm // SUBLANES
    na, nb = xa.shape[0] // tm, xb.shape[0] // tm
    n = xa.shape[0] + xb.shape[0]
    meta = _tile_meta(groups, tm)

    def x_specs(tile_of, n_tiles):
        last8 = n_tiles * per - 1
        return [pl.BlockSpec((tm, D_MODEL), lambda i, m: (jnp.clip(tile_of(i), 0, n_tiles - 1), 0)),
                pl.BlockSpec((SUBLANES, D_MODEL), lambda i, m: (jnp.clip(tile_of(i) * per - 1, 0, last8), 0)),
                pl.BlockSpec((SUBLANES, D_MODEL), lambda i, m: (jnp.clip((tile_of(i) + 1) * per, 0, last8), 0))]

    gs = pltpu.PrefetchScalarGridSpec(
        num_scalar_prefetch=1, grid=(n // tm,),
        in_specs=x_specs(lambda i: i, na) + x_specs(lambda i: i - na, nb) + [
                  pl.BlockSpec((tm, ATTN_W), lambda i, m: (i, 0)),
                  _const_spec((1, D_MODEL)),
                  _const_spec((D_MODEL, REST_W)),
                  _const_spec((3, CONV_W)),
                  _const_spec((ATTN_W, D_MODEL)),
                  _const_spec((CONV_W, D_MODEL)),
                  _const_spec((D_MODEL, D_MODEL))],
        out_specs=pl.BlockSpec((tm, D_MODEL), lambda i, m: (i, 0)))
    return pl.pallas_call(
        functools.partial(_mixer_kernel, na), grid_spec=gs, name="mixer",
        out_shape=jax.ShapeDtypeStruct((n, D_MODEL), F32),
        compiler_params=_params())(meta, xa, xa, xa, xb, xb, xb, o, g_mix, w_rest, conv_w, w_attn_br,
                                   w_conv_br, w_mix_out)


def _memkv_kernel(na, mema_ref, memb_ref, g_ref, w_ref, kv_ref):
    mem = jnp.where(pl.program_id(0) < na, mema_ref[0], memb_ref[0])
    mn = _rms(mem, g_ref[...]).astype(BF16)
    kv_ref[0] = jnp.dot(mn, w_ref[...], preferred_element_type=F32).astype(BF16)


def _memkv_call(mem_a, mem_b, g_mem, w_xkv):
    na, nb = mem_a.shape[0], mem_b.shape[0]
    return pl.pallas_call(
        functools.partial(_memkv_kernel, na), grid=(na + nb,), name="mem_kv",
        in_specs=[pl.BlockSpec((1, N_MEM, D_MODEL), lambda b: (jnp.minimum(b, na - 1), 0, 0)),
                  pl.BlockSpec((1, N_MEM, D_MODEL), lambda b: (jnp.maximum(b - na, 0), 0, 0)),
                  _const_spec((1, D_MODEL)),
                  _const_spec((D_MODEL, 2 * D_MODEL))],
        out_specs=pl.BlockSpec((1, N_MEM, 2 * D_MODEL), lambda b: (b, 0, 0)),
        out_shape=jax.ShapeDtypeStruct((na + nb, N_MEM, 2 * D_MODEL), BF16),
        compiler_params=_params())(mem_a, mem_b, g_mem, w_xkv)


def _cross_kernel(meta_ref, h_ref, kv_ref, gc_ref, wq_ref, wo_ref, gm_ref, wr_ref, br_ref,
                  h2_ref, xs_ref, idx_ref, tot_ref):
    del meta_ref
    tm = h_ref.shape[0]
    h = h_ref[...]
    hn = _rms(h, gc_ref[...]).astype(BF16)
    q = (jnp.dot(hn, wq_ref[...], preferred_element_type=F32) * (X_HEAD_DIM ** -0.5)).astype(BF16)
    outs = []
    for hd in range(X_HEADS):
        qh = q[:, hd * X_HEAD_DIM:(hd + 1) * X_HEAD_DIM]
        kh = kv_ref[0, :, hd * X_HEAD_DIM:(hd + 1) * X_HEAD_DIM]
        vh = kv_ref[0, :, D_MODEL + hd * X_HEAD_DIM:D_MODEL + (hd + 1) * X_HEAD_DIM]
        s = lax.dot_general(qh, kh, (((1,), (1,)), ((), ())), preferred_element_type=F32)
        m = jnp.max(s, axis=-1, keepdims=True)
        p = jnp.exp(s - m)
        p = (p / jnp.sum(p, axis=-1, keepdims=True)).astype(BF16)
        outs.append(jnp.dot(p, vh, preferred_element_type=F32).astype(BF16))
    o = jnp.concatenate(outs, axis=1)
    h2 = h + jnp.dot(o, wo_ref[...], preferred_element_type=F32)
    h2_ref[...] = h2

    hn3 = _rms(h2, gm_ref[...])
    hi = hn3.astype(BF16)
    hi32 = hi.astype(F32)
    lo = (hn3 - hi32).astype(BF16)
    xs_ref[:, 0:D_MODEL] = hn3

    nt = (((1,), (1,)), ((), ()))
    r1 = lax.dot_general(wr_ref[...], hi, nt, preferred_element_type=F32)
    r2 = lax.dot_general(wr_ref[...], lo, nt, preferred_element_type=F32)
    e = N_EXPERTS
    logits = ((r2[e:2 * e] + r2[0:e]) + r1[e:2 * e]) + r1[0:e] + br_ref[...]
    eio = lax.broadcasted_iota(jnp.int32, (e, tm), 0)
    cur = logits
    vals, sels = [], []
    for _ in range(TOP_K):
        mx = jnp.max(cur, axis=0, keepdims=True)
        sel = jnp.min(jnp.where(cur == mx, eio, e), axis=0, keepdims=True)
        vals.append(mx)
        sels.append(sel)
        cur = jnp.where(eio == sel, -jnp.inf, cur)
    ex = [jnp.exp(v - vals[0]) for v in vals]
    tot = ex[0] + ex[1] + ex[2] + ex[3]
    idx_ref[...] = jnp.concatenate(sels, axis=0)
    assign = jnp.where(eio == sels[0], 1.0, 0.0)
    for k in range(1, TOP_K):
        assign = assign + jnp.where(eio == sels[k], 1.0, 0.0)
    segs = jnp.zeros((e, 1), jnp.int32)
    for j in range(tm // TC_COMB):
        cnt = jnp.sum(assign[:, j * TC_COMB:(j + 1) * TC_COMB], axis=1, keepdims=True).astype(jnp.int32)
        segs = segs + lax.shift_left(lax.shift_right_logical(cnt + (SEG_ALIGN - 1), LOG2_SEG), LOG2_SEG)

    @pl.when(pl.program_id(0) == 0)
    def _():
        tot_ref[...] = jnp.zeros_like(tot_ref)
    tot_ref[...] += jnp.broadcast_to(segs, tot_ref.shape)
    gws = [x / tot for x in ex]
    w_hi = [g.astype(BF16).astype(F32) for g in gws]
    w_lo = [(g - h).astype(BF16).astype(F32) for g, h in zip(gws, w_hi)]
    wt = jnp.transpose(jnp.concatenate(w_hi + w_lo, axis=0))
    et = jnp.transpose(jnp.concatenate([s.astype(F32) for s in sels]
                                       + [jnp.zeros((SUBLANES - TOP_K, tm), F32)], axis=0))
    xs_ref[:, D_MODEL:XS_W] = jnp.concatenate(
        [wt, et, jnp.zeros((tm, LANES - 2 * SUBLANES), F32)], axis=1)


def _cross_call(h1, memkv, g_cross, w_xq, w_xo, g_moe, w_r2t, b_router, groups):
    n = h1.shape[0]
    tm = TM_CROSS
    meta = _tile_meta(groups, tm)
    row = lambda i, m: (i, 0)
    col = lambda i, m: (0, i)
    gs = pltpu.PrefetchScalarGridSpec(
        num_scalar_prefetch=1, grid=(n // tm,),
        in_specs=[pl.BlockSpec((tm, D_MODEL), row),
                  pl.BlockSpec((1, N_MEM, 2 * D_MODEL), lambda i, m: (m[3, i], 0, 0)),
                  _const_spec((1, D_MODEL)),
                  _const_spec((D_MODEL, D_MODEL)),
                  _const_spec((D_MODEL, D_MODEL)),
                  _const_spec((1, D_MODEL)),
                  _const_spec((2 * N_EXPERTS, D_MODEL)),
                  _const_spec((N_EXPERTS, 1))],
        out_specs=[pl.BlockSpec((tm, D_MODEL), row),
                   pl.BlockSpec((tm, XS_W), row),
                   pl.BlockSpec((TOP_K, tm), col),
                   pl.BlockSpec((N_EXPERTS, LANES), lambda i, m: (0, 0))])
    return pl.pallas_call(
        _cross_kernel, grid_spec=gs, name="cross_router",
        out_shape=[jax.ShapeDtypeStruct((n, D_MODEL), F32),
                   jax.ShapeDtypeStruct((n, XS_W), F32),
                   jax.ShapeDtypeStruct((TOP_K, n), jnp.int32),
                   jax.ShapeDtypeStruct((N_EXPERTS, LANES), jnp.int32)],
        compiler_params=_params())(meta, h1, memkv, g_cross, w_xq, w_xo, g_moe, w_r2t, b_router)


LOG2_BM = BM_MOE.bit_length() - 1
assert 1 << LOG2_BM == BM_MOE
SEG_ALIGN = SUBLANES
LOG2_SEG = SEG_ALIGN.bit_length() - 1
COPY_ROWS = (4 * SEG_ALIGN, 2 * SEG_ALIGN, SEG_ALIGN)
K_ROWS = 1280
M_CHUNK = 640
assert 1 << LOG2_SEG == SEG_ALIGN and K_ROWS % M_CHUNK == 0 and K_ROWS // COPY_ROWS[0] <= LANES
assert K_ROWS >= TOP_K * TC_COMB + N_EXPERTS * (SEG_ALIGN - 1)
DESC_COUNT = 2 * len(COPY_ROWS)
assert DESC_COUNT < SUBLANES and all(a == 2 * b for a, b in zip(COPY_ROWS, COPY_ROWS[1:]))
AUX_W_HI, AUX_W_LO, AUX_EID = 0, TOP_K, 2 * TOP_K
XS_W = D_MODEL + LANES


def _sublane_scan(x, n):
    row = lax.broadcasted_iota(jnp.int32, x.shape, 0)
    sft = 1
    while sft < n:
        x = x + jnp.where(row >= sft, pltpu.roll(x, sft, 0), 0)
        sft *= 2
    return x


def _owner(ends, lane):
    ebl = lax.broadcasted_iota(jnp.int32, ends.shape, 0)
    owner = jnp.sum(jnp.where(ends <= lane, 1.0, 0.0), axis=0, keepdims=True).astype(jnp.int32)
    return ebl == owner


def _pick(sel, val):
    return jnp.sum(jnp.where(sel, val.astype(F32), 0.0), axis=0, keepdims=True).astype(jnp.int32)


def _plan_kernel(idx_ref, tot_ref, kpos_ref, desc_ref, meta_ref, tri_ref, run_ref):
    i = pl.program_id(0)
    e, tp = N_EXPERTS, TC_COMB
    nb_pad = meta_ref.shape[1]
    eio = lax.broadcasted_iota(jnp.int32, (e, tp), 0)
    idx = idx_ref[...]
    onehots = [eio == idx[k:k + 1, :] for k in range(TOP_K)]
    s = jnp.where(onehots[0], 1.0, 0.0)
    for k in range(1, TOP_K):
        s = s + jnp.where(onehots[k], 1.0, 0.0)
    cnt_t = jnp.sum(s, axis=1, keepdims=True).astype(jnp.int32)
    cnt = jnp.broadcast_to(cnt_t, (e, LANES))
    seg = lax.shift_left(lax.shift_right_logical(cnt + (SEG_ALIGN - 1), LOG2_SEG), LOG2_SEG)

    @pl.when(i == 0)
    def _():
        r = lax.broadcasted_iota(jnp.int32, (tp, tp), 0)
        c = lax.broadcasted_iota(jnp.int32, (tp, tp), 1)
        tri_ref[...] = jnp.where(r < c, 1.0, 0.0).astype(BF16)
        tot = tot_ref[...]
        pad = lax.shift_left(lax.shift_right_logical(tot + (BM_MOE - 1), LOG2_BM), LOG2_BM)
        pad_end = _sublane_scan(pad, e)
        pad_start = pad_end - pad
        run_ref[...] = pad_start
        bpos = lax.broadcasted_iota(jnp.int32, (e, nb_pad), 1) * BM_MOE
        ebl = lax.broadcasted_iota(jnp.int32, (e, nb_pad), 0)
        done = jnp.where(pad_end[:, 0:1] <= bpos, 1.0, 0.0)
        bexp = jnp.minimum(jnp.sum(done, axis=0, keepdims=True).astype(jnp.int32), e - 1)
        row_end = (pad_start + tot)[:, 0:1].astype(F32)
        rend_b = jnp.sum(jnp.where(ebl == bexp, row_end, 0.0), axis=0, keepdims=True).astype(jnp.int32)
        nvalid = jnp.clip(rend_b - bpos[0:1, :], 0, BM_MOE)
        total = jnp.sum(jnp.where(ebl == e - 1, pad_end[:, 0:1].astype(F32), 0.0), axis=0, keepdims=True)
        nused = lax.shift_right_logical(total.astype(jnp.int32), LOG2_BM)
        meta_ref[...] = jnp.concatenate(
            [bexp, nvalid, nused, jnp.zeros((SUBLANES - 3, nb_pad), jnp.int32)], axis=0)

    before = jnp.dot(s.astype(BF16), tri_ref[...], preferred_element_type=F32)
    kbase = _sublane_scan(seg, e) - seg
    pos = kbase[:, 0:1].astype(F32) + before
    kp = [jnp.sum(jnp.where(onehots[k], pos, 0.0), axis=0, keepdims=True).astype(jnp.int32)
          for k in range(TOP_K)]
    kpos_ref[0] = jnp.concatenate(kp + [jnp.zeros((SUBLANES - TOP_K, tp), jnp.int32)], axis=0)

    start = run_ref[...]
    lane = lax.broadcasted_iota(jnp.int32, (e, LANES), 1)
    lane1 = lane[0:1, :]
    lists = []
    counts = jnp.zeros((1, LANES), jnp.int32)
    done = jnp.zeros_like(seg)
    for ci, rows in enumerate(COPY_ROWS):
        n = lax.shift_right_logical(seg - done, rows.bit_length() - 1)
        end = _sublane_scan(n, e)
        sel = _owner(end, lane)
        off = _pick(sel, done) + (lane1 - _pick(sel, end - n)) * rows
        n_c = end[e - 1:e, :]
        ok = lane1 < n_c
        lists += [jnp.where(ok, _pick(sel, start) + off, 0), jnp.where(ok, _pick(sel, kbase) + off, 0)]
        counts = jnp.where(lane1 == ci, n_c, counts)
        done = done + n * rows
    desc_ref[0] = jnp.concatenate(
        lists + [counts, jnp.zeros((SUBLANES - DESC_COUNT - 1, LANES), jnp.int32)], axis=0)
    run_ref[...] = run_ref[...] + seg


def _plan_call(idx, tot, n_blocks):
    n = idx.shape[1]
    tp = TC_COMB
    n_tiles = n // tp
    nb_pad = -(-n_blocks // LANES) * LANES
    tile = lambda i: (i, 0, 0)
    return pl.pallas_call(
        _plan_kernel, grid=(n_tiles,), name="route_plan",
        in_specs=[pl.BlockSpec((TOP_K, tp), lambda i: (0, i)), _const_spec((N_EXPERTS, LANES))],
        out_specs=[pl.BlockSpec((1, SUBLANES, tp), tile),
                   pl.BlockSpec((1, SUBLANES, LANES), tile),
                   pl.BlockSpec((SUBLANES, nb_pad), lambda i: (0, 0))],
        out_shape=[jax.ShapeDtypeStruct((n_tiles, SUBLANES, tp), jnp.int32),
                   jax.ShapeDtypeStruct((n_tiles, SUBLANES, LANES), jnp.int32),
                   jax.ShapeDtypeStruct((SUBLANES, nb_pad), jnp.int32)],
        scratch_shapes=[pltpu.VMEM((tp, tp), BF16), pltpu.VMEM((N_EXPERTS, LANES), jnp.int32)],
        compiler_params=_params())(idx, tot)


def _slot_hits(kp, iota):
    hit = jnp.where(iota == kp[0], 1.0, 0.0)
    for k in range(1, TOP_K):
        hit = hit + jnp.where(iota == kp[k], 1.0, 0.0)
    return hit


def _for_each_copy(count, fn):
    for ci, rows in enumerate(COPY_ROWS):
        def body(j, carry, ci=ci, rows=rows):
            fn(2 * ci, j, rows)
            return carry
        lax.fori_loop(0, count(ci), body, 0)


def _dispatch_kernel(n_tiles, n_blocks, nvalid_ref, desc_ref, kpos_ref, xs_ref, out_hbm, zero_ref, buf,
                     sems, zsem, issued_ref):
    i = pl.program_id(0)
    td = xs_ref.shape[0]
    bm = zero_ref.shape[0]
    cur = i % 2

    def copy(b, dst, src, rows):
        return pltpu.make_async_copy(buf.at[b, pl.ds(pl.multiple_of(src, SEG_ALIGN), rows), :],
                                     out_hbm.at[pl.ds(pl.multiple_of(dst, SEG_ALIGN), rows), :], sems.at[b])

    def drain(b):
        _for_each_copy(lambda ci: issued_ref[b, ci], lambda row, j, rows: copy(b, 0, 0, rows).wait())

    @pl.when(i == 0)
    def _():
        zero_ref[...] = jnp.zeros_like(zero_ref)

        def fill(b):
            return pltpu.make_async_copy(zero_ref, out_hbm.at[pl.ds(pl.multiple_of(b * bm, bm), bm), :], zsem)

        def start(b, carry):
            @pl.when(nvalid_ref[b] < bm)
            def _():
                fill(b).start()
            return carry

        def wait(b, carry):
            @pl.when(nvalid_ref[b] < bm)
            def _():
                fill(b).wait()
            return carry
        lax.fori_loop(0, n_blocks, start, 0)
        lax.fori_loop(0, n_blocks, wait, 0)

    @pl.when(i >= 2)
    def _():
        drain(cur)

    kp = kpos_ref[0]
    x = xs_ref[...].astype(BF16)
    for c in range(K_ROWS // M_CHUNK):
        rows = lax.broadcasted_iota(jnp.int32, (M_CHUNK, td), 0) + c * M_CHUNK
        sel = _slot_hits([kp[k:k + 1, :] for k in range(TOP_K)], rows).astype(BF16)
        buf[cur, c * M_CHUNK:(c + 1) * M_CHUNK, :] = jnp.dot(sel, x, preferred_element_type=F32)

    _for_each_copy(lambda ci: desc_ref[0, DESC_COUNT, ci],
                   lambda row, j, rows: copy(cur, desc_ref[0, row, j], desc_ref[0, row + 1, j], rows).start())
    for ci in range(len(COPY_ROWS)):
        issued_ref[cur, ci] = desc_ref[0, DESC_COUNT, ci]

    @pl.when(i == n_tiles - 1)
    def _():
        drain(cur)

        @pl.when(i >= 1)
        def _():
            drain(1 - cur)


def _dispatch_call(nvalid, desc, kpos, xs, n_blocks):
    n, width = xs.shape
    td, bm = TC_COMB, BM_MOE
    n_tiles = n // td
    gs = pltpu.PrefetchScalarGridSpec(
        num_scalar_prefetch=1, grid=(n_tiles,),
        in_specs=[pl.BlockSpec((1, SUBLANES, LANES), lambda i, nv: (i, 0, 0), memory_space=pltpu.SMEM),
                  pl.BlockSpec((1, SUBLANES, td), lambda i, nv: (i, 0, 0)),
                  pl.BlockSpec((td, width), lambda i, nv: (i, 0))],
        out_specs=pl.BlockSpec(memory_space=pl.ANY),
        scratch_shapes=[pltpu.VMEM((bm, width), xs.dtype), pltpu.VMEM((2, K_ROWS, width), xs.dtype),
                        pltpu.SemaphoreType.DMA((2,)), pltpu.SemaphoreType.DMA(()),
                        pltpu.SMEM((2, len(COPY_ROWS)), jnp.int32)])
    return pl.pallas_call(
        functools.partial(_dispatch_kernel, n_tiles, n_blocks), grid_spec=gs, name="moe_dispatch",
        out_shape=jax.ShapeDtypeStruct((n_blocks * bm, width), xs.dtype),
        compiler_params=_params())(nvalid, desc, kpos, xs)


def _ffn_kernel(bexp_ref, nused_ref, x_ref, wg_hbm, bg_ref, wu_hbm, bu_ref, wd_hbm, bd_ref,
                y_ref, wf32, wbf, wsems, slot_ref):
    b = pl.program_id(0)
    nused = nused_ref[0]
    e = bexp_ref[b]
    wgb, wub, wdb = wbf.at[0], wbf.at[1], wbf.at[2]

    def fetch(expert, slot):
        return [pltpu.make_async_copy(src.at[expert], wf32.at[slot, j], wsems.at[slot])
                for j, src in enumerate((wg_hbm, wu_hbm, wd_hbm))]

    @pl.when(b == 0)
    def _():
        slot_ref[0] = 0
        for d in fetch(e, 0):
            d.start()

    @pl.when((b < nused) & ((b == 0) | (e != bexp_ref[jnp.maximum(b - 1, 0)])))
    def _():
        slot = slot_ref[0]
        for d in fetch(e, slot):
            d.wait()
        for j in range(3):
            wbf[j] = wf32[slot, j].astype(BF16)
        nxt = lax.while_loop(lambda j: (j < nused) & (bexp_ref[jnp.minimum(j, nused - 1)] == e),
                             lambda j: j + 1, b + 1)

        @pl.when(nxt < nused)
        def _():
            for d in fetch(bexp_ref[nxt], 1 - slot):
                d.start()
        slot_ref[0] = 1 - slot

    @pl.when(b < nused)
    def _():
        x = x_ref[:, 0:D_MODEL].astype(BF16)
        aux = x_ref[:, D_MODEL:XS_W]
        eid = bexp_ref[b].astype(F32)
        row_w = jnp.zeros((x.shape[0], 1), F32)
        for k in range(TOP_K):
            w_k = aux[:, AUX_W_HI + k:AUX_W_HI + k + 1] + aux[:, AUX_W_LO + k:AUX_W_LO + k + 1]
            row_w = row_w + jnp.where(aux[:, AUX_EID + k:AUX_EID + k + 1] == eid, w_k, 0.0)

        def proj(w_ref, b_ref):
            return jnp.dot(x, w_ref[...], preferred_element_type=F32) + b_ref[0]

        a = jnp.minimum(proj(wgb, bg_ref), SWIGLU_LIMIT)
        u = jnp.clip(proj(wub, bu_ref), -SWIGLU_LIMIT, SWIGLU_LIMIT)
        hid = (a * jax.nn.sigmoid(SWIGLU_ALPHA * a) * (u + 1.0)).astype(BF16)
        y_ref[...] = (jnp.dot(hid, wdb[...], preferred_element_type=F32) + bd_ref[0]) * row_w

    @pl.when(b >= nused)
    def _():
        y_ref[...] = jnp.zeros_like(y_ref)


def _ffn_call(bexp, nused, xs_sorted, w_gate, b_gate, w_up, b_up, w_down, b_down):
    bm = BM_MOE
    n_blocks = xs_sorted.shape[0] // bm
    assert D_FF == D_MODEL, "the three expert matrices share one staging shape"
    bspec = lambda c: pl.BlockSpec((1, 1, c), lambda b, be, nu: (be[b], 0, 0))
    whole = pl.BlockSpec(memory_space=pl.ANY)
    gs = pltpu.PrefetchScalarGridSpec(
        num_scalar_prefetch=2, grid=(n_blocks,),
        in_specs=[pl.BlockSpec((bm, XS_W), lambda b, be, nu: (jnp.minimum(b, nu[0] - 1), 0)),
                  whole, bspec(D_FF), whole, bspec(D_FF), whole, bspec(D_MODEL)],
        out_specs=pl.BlockSpec((bm, D_MODEL), lambda b, be, nu: (b, 0)),
        scratch_shapes=[pltpu.VMEM((2, 3, D_MODEL, D_FF), F32), pltpu.VMEM((3, D_MODEL, D_FF), BF16),
                        pltpu.SemaphoreType.DMA((2,)), pltpu.SMEM((1,), jnp.int32)])
    return pl.pallas_call(
        _ffn_kernel, grid_spec=gs, name="expert_ffn",
        out_shape=jax.ShapeDtypeStruct((n_blocks * bm, D_MODEL), F32),
        compiler_params=_params())(bexp, nused, xs_sorted, w_gate, b_gate, w_up, b_up, w_down, b_down)


def _combine_kernel(nt, desc_ref, descn_ref, kpos_ref, y_hbm, h_ref, gf_ref, out_ref, ybuf, sems):
    i = pl.program_id(0)
    tc = h_ref.shape[0]
    cur = i % 2

    def copy(b, dst, src, rows):
        return pltpu.make_async_copy(y_hbm.at[pl.ds(pl.multiple_of(dst, SEG_ALIGN), rows), :],
                                     ybuf.at[b, pl.ds(pl.multiple_of(src, SEG_ALIGN), rows), :], sems.at[b])

    def fetch(b, ref):
        _for_each_copy(lambda ci: ref[0, DESC_COUNT, ci],
                       lambda row, j, rows: copy(b, ref[0, row, j], ref[0, row + 1, j], rows).start())

    @pl.when(i == 0)
    def _():
        ybuf[...] = jnp.zeros_like(ybuf)
        fetch(0, desc_ref)

    @pl.when(i + 1 < nt)
    def _():
        fetch(1 - cur, descn_ref)

    _for_each_copy(lambda ci: desc_ref[0, DESC_COUNT, ci], lambda row, j, rows: copy(cur, 0, 0, rows).wait())

    kpt = jnp.transpose(kpos_ref[0].astype(F32))
    kp = [kpt[:, k:k + 1] for k in range(TOP_K)]
    acc = h_ref[...]
    for c in range(K_ROWS // M_CHUNK):
        cols = (lax.broadcasted_iota(jnp.int32, (tc, M_CHUNK), 1) + c * M_CHUNK).astype(F32)
        pick = _slot_hits(kp, cols).astype(BF16)
        y = ybuf[cur, c * M_CHUNK:(c + 1) * M_CHUNK, :].astype(BF16)
        acc = acc + jnp.dot(pick, y, preferred_element_type=F32)
    out_ref[...] = _rms(acc, gf_ref[...])


def _combine_call(desc, kpos, y, h2, g_final, tile0, n_tiles):
    tc = TC_COMB
    last = tile0 + n_tiles - 1
    return pl.pallas_call(
        functools.partial(_combine_kernel, n_tiles), grid=(n_tiles,), name="moe_combine",
        in_specs=[pl.BlockSpec((1, SUBLANES, LANES), lambda i: (tile0 + i, 0, 0), memory_space=pltpu.SMEM),
                  pl.BlockSpec((1, SUBLANES, LANES), lambda i: (jnp.minimum(tile0 + i + 1, last), 0, 0),
                               memory_space=pltpu.SMEM),
                  pl.BlockSpec((1, SUBLANES, tc), lambda i: (tile0 + i, 0, 0)),
                  pl.BlockSpec(memory_space=pl.ANY),
                  pl.BlockSpec((tc, D_MODEL), lambda i: (tile0 + i, 0)),
                  _const_spec((1, D_MODEL))],
        out_specs=pl.BlockSpec((tc, D_MODEL), lambda i: (i, 0)),
        out_shape=jax.ShapeDtypeStruct((n_tiles * tc, D_MODEL), F32),
        scratch_shapes=[pltpu.VMEM((2, K_ROWS, D_MODEL), F32), pltpu.SemaphoreType.DMA((2,))],
        compiler_params=_params())(desc, desc, kpos, y, h2, g_final)


def _forward(xs, mems, g_mix, w_in, sink, conv_w, w_attn_br, w_conv_br, w_mix_out, g_cross, g_mem,
             w_xq, w_xkv, w_xo, g_moe, w_router, b_router, w_gate, b_gate, w_up, b_up, w_down,
             b_down, g_final):
    groups = [(x.shape[0], x.shape[1]) for x in xs]
    for _, s in groups:
        assert s % max(TM_QKV, TQ_ATTN, TM_MIX, TM_CROSS, TC_COMB) == 0
    assert len(xs) == 2, "two request groups"
    xa, xb = (x.reshape(-1, D_MODEL) for x in xs)
    n = xa.shape[0] + xb.shape[0]
    tables = _rope_tables(max(s for _, s in groups))
    assert w_in.shape[0] == 1, "single-layer trunk: the final norm is fused into the combine kernel"
    l = 0
    row2 = lambda v: v.reshape(1, -1)
    w_in_b = w_in[l].astype(BF16)
    perm = np.array([(kv * GROUP + g) * HEAD_DIM + d
                     for g in range(GROUP) for kv in range(N_KV_HEADS) for d in range(HEAD_DIM)])
    w_qkv = jnp.concatenate([w_in_b[:, :ATTN_W][:, perm], w_in_b[:, ATTN_W:QKV_W]], axis=1)
    q, kv = _qkv_call(xa, xb, row2(g_mix[l]), w_qkv, tables, groups)
    o = _attn_call(q, kv, sink[l], groups)
    h = _mixer_call(xa, xb, o, row2(g_mix[l]), w_in_b[:, QKV_W:], conv_w[l],
                    w_attn_br[l].astype(BF16)[perm, :], w_conv_br[l].astype(BF16),
                    w_mix_out[l].astype(BF16), groups)
    memkv = _memkv_call(mems[0], mems[1], row2(g_mem[l]), w_xkv[l].astype(BF16))
    wr = w_router[l]
    wr_hi = wr.astype(BF16)
    wr_lo = (wr - wr_hi.astype(F32)).astype(BF16)
    w_r2t = jnp.concatenate([wr_hi.T, wr_lo.T], axis=0)
    h2, xs_rows, idx, tot = _cross_call(h, memkv, row2(g_cross[l]), w_xq[l].astype(BF16),
                                        w_xo[l].astype(BF16), row2(g_moe[l]), w_r2t,
                                        b_router[l].reshape(-1, 1), groups)
    seg_rows = TOP_K * n + (n // TC_COMB) * N_EXPERTS * (SEG_ALIGN - 1)
    n_blocks = -(-seg_rows // BM_MOE) + N_EXPERTS
    kpos, desc, meta = _plan_call(idx, tot, n_blocks)
    xs_sorted = _dispatch_call(meta[1], desc, kpos, xs_rows, n_blocks)
    y = _ffn_call(meta[0], meta[2, 0:1], xs_sorted, w_gate[l], b_gate[l][:, None, :],
                  w_up[l], b_up[l][:, None, :], w_down[l], b_down[l][:, None, :])
    tc = TC_COMB
    outs, t0 = [], 0
    for nb, s in groups:
        nt = nb * s // tc
        outs.append(_combine_call(desc, kpos, y, h2, row2(g_final), t0, nt))
        t0 += nt
    return tuple(o.reshape(x.shape) for o, x in zip(outs, xs))


def kernel(x_prompt, x_sample, mem_prompt, mem_sample, g_mix, w_in, sink, conv_w, w_attn_br, w_conv_br,
           w_mix_out, g_cross, g_mem, w_xq, w_xkv, w_xo, g_moe, w_router, b_router, w_gate, b_gate,
           w_up, b_up, w_down, b_down, g_final):
    return _forward([x_prompt, x_sample], [mem_prompt, mem_sample], g_mix, w_in, sink, conv_w,
                    w_attn_br, w_conv_br, w_mix_out, g_cross, g_mem, w_xq, w_xkv, w_xo, g_moe,
                    w_router, b_router, w_gate, b_gate, w_up, b_up, w_down, b_down, g_final)
```

```python
import functools

import numpy as np
import jax
import jax.numpy as jnp
from jax import lax
from jax.experimental import pallas as pl
from jax.experimental.pallas import tpu as pltpu

F32 = jnp.float32
BF16 = jnp.bfloat16

D_MODEL = 1024
N_HEADS = 8
N_KV_HEADS = 2
HEAD_DIM = 64
GROUP = N_HEADS // N_KV_HEADS
ATTN_W = N_HEADS * HEAD_DIM
KV_W = N_KV_HEADS * HEAD_DIM
QKV_W = ATTN_W + 2 * KV_W
WINDOW = 128
ROT_DIM = HEAD_DIM // 4
ROPE_THETA = 500000.0
CONV_W = D_MODEL // 2
REST_W = 3 * CONV_W + 2 * D_MODEL
N_MEM = 256
X_HEADS = 4
X_HEAD_DIM = D_MODEL // X_HEADS
N_EXPERTS = 32
TOP_K = 4
D_FF = D_MODEL
SWIGLU_ALPHA = 1.702
SWIGLU_LIMIT = 7.0
EPS = 1e-5

LANES = 128
SUBLANES = 8
KEY_BLOCK = WINDOW
TM_QKV = 1024
TQ_ATTN = 1024
TM_MIX = 512
TM_CROSS = 1024
BM_MOE = 512
TC_COMB = 256
NEG_BIG = -1e30
VMEM_LIMIT = 56 * 1024 * 1024


def _rms(x, g):
    var = jnp.mean(x * x, axis=-1, keepdims=True)
    return x * lax.rsqrt(var + EPS) * g


def _tile_meta(groups, tile):
    pos, first, last, bidx = [], [], [], []
    b0 = 0
    for nb, s in groups:
        per = s // tile
        for b in range(nb):
            for j in range(per):
                pos.append(j)
                first.append(int(j == 0))
                last.append(int(j == per - 1))
                bidx.append(b0 + b)
        b0 += nb
    return jnp.asarray(np.array([pos, first, last, bidx], dtype=np.int32))


def _const_spec(shape):
    nd = len(shape)
    return pl.BlockSpec(shape, lambda *_: (0,) * nd, pipeline_mode=pl.Buffered(1))


def _params(vmem=VMEM_LIMIT):
    return pltpu.CompilerParams(dimension_semantics=("arbitrary",), vmem_limit_bytes=vmem)


def _qkv_kernel(na, meta_ref, xa_ref, xb_ref, g_ref, w_ref, cos_ref, sa_ref, sb_ref, q_ref, kv_ref):
    del meta_ref
    x = jnp.where(pl.program_id(0) < na, xa_ref[...], xb_ref[...])
    xn = _rms(x, g_ref[...]).astype(BF16)
    proj = jnp.dot(xn, w_ref[...], preferred_element_type=F32)
    c, sa, sb = cos_ref[...], sa_ref[...], sb_ref[...]
    n_rot = (ATTN_W + KV_W) // LANES
    for gi in range(n_rot):
        p = proj[:, gi * LANES:(gi + 1) * LANES]
        r = p * c + pltpu.roll(p, LANES - ROT_DIM // 2, 1) * sa + pltpu.roll(p, ROT_DIM // 2, 1) * sb
        if gi < ATTN_W // LANES:
            q_ref[:, gi * LANES:(gi + 1) * LANES] = (r * (HEAD_DIM ** -0.5)).astype(BF16)
        else:
            kv_ref[:, 0:KV_W] = r.astype(BF16)
    kv_ref[:, KV_W:2 * KV_W] = proj[:, ATTN_W + KV_W:QKV_W].astype(BF16)


def _rope_tables(s_max):
    half = ROT_DIM // 2
    inv_freq = ROPE_THETA ** (-(jnp.arange(half, dtype=F32) * 2.0) / ROT_DIM)
    ang = jnp.arange(s_max, dtype=F32)[:, None] * inv_freq[None, :]
    cos, sin = jnp.cos(ang), jnp.sin(ang)
    d = np.arange(LANES) % HEAD_DIM
    j = d % half
    cos_l, sin_l = cos[:, j], sin[:, j]
    rot = jnp.asarray(d < ROT_DIM)[None, :]
    lo = jnp.asarray(d < half)[None, :]
    hi = jnp.asarray((d >= half) & (d < ROT_DIM))[None, :]
    c = jnp.where(rot, cos_l, 1.0)
    sa = jnp.where(lo, -sin_l, 0.0)
    sb = jnp.where(hi, sin_l, 0.0)
    return c, sa, sb


def _qkv_call(xa, xb, g_mix, w_qkv, tables, groups):
    tm = TM_QKV
    na = xa.shape[0] // tm
    n = xa.shape[0] + xb.shape[0]
    meta = _tile_meta(groups, tm)
    row = lambda i, m: (i, 0)
    tab = lambda i, m: (m[0, i], 0)
    gs = pltpu.PrefetchScalarGridSpec(
        num_scalar_prefetch=1, grid=(n // tm,),
        in_specs=[pl.BlockSpec((tm, D_MODEL), lambda i, m: (jnp.minimum(i, na - 1), 0)),
                  pl.BlockSpec((tm, D_MODEL), lambda i, m: (jnp.maximum(i - na, 0), 0)),
                  _const_spec((1, D_MODEL)),
                  _const_spec((D_MODEL, QKV_W)),
                  pl.BlockSpec((tm, LANES), tab), pl.BlockSpec((tm, LANES), tab),
                  pl.BlockSpec((tm, LANES), tab)],
        out_specs=[pl.BlockSpec((tm, ATTN_W), row), pl.BlockSpec((tm, 2 * KV_W), row)])
    return pl.pallas_call(
        functools.partial(_qkv_kernel, na), grid_spec=gs, name="qkv_rope",
        out_shape=[jax.ShapeDtypeStruct((n, ATTN_W), BF16), jax.ShapeDtypeStruct((n, 2 * KV_W), BF16)],
        compiler_params=_params())(meta, xa, xb, g_mix, w_qkv, *tables)


def _attn_kernel(meta_ref, sink_ref, q_ref, kvp_ref, kvm_ref, kvn_ref, o_ref, kcat_ref):
    i = pl.program_id(0)
    kb = KEY_BLOCK
    tq = q_ref.shape[0]
    kcat_ref[0:kb, :] = kvp_ref[...]
    kcat_ref[kb:kb + tq, :] = kvm_ref[...]
    kcat_ref[kb + tq:kb + tq + kb, :] = kvn_ref[...]
    r = lax.broadcasted_iota(jnp.int32, (kb, 3 * kb), 0)
    c = lax.broadcasted_iota(jnp.int32, (kb, 3 * kb), 1)
    dlt = c - r
    klane = lax.broadcasted_iota(jnp.int32, (3 * kb, KV_W), 1)
    olane = lax.broadcasted_iota(jnp.int32, (GROUP * kb, KV_W), 1)
    nsub = tq // kb
    for s in range(nsub):
        first = meta_ref[1, i * nsub + s]
        last = meta_ref[2, i * nsub + s]
        lo = jnp.where(first == 1, kb, 0)
        hi = jnp.where(last == 1, 2 * kb, 3 * kb)
        valid = (dlt >= 0) & (dlt <= 2 * WINDOW) & (c >= lo) & (c < hi)
        bias = jnp.where(valid, 0.0, NEG_BIG)
        bias = jnp.concatenate([bias] * GROUP, axis=0)
        kw = kcat_ref[s * kb:(s + 3) * kb, :]
        kc = kw[:, 0:KV_W]
        vc = jnp.concatenate([kw[:, KV_W:2 * KV_W], jnp.ones((3 * kb, KV_W), BF16)], axis=1)
        q4 = jnp.concatenate([q_ref[s * kb:(s + 1) * kb, g * KV_W:(g + 1) * KV_W] for g in range(GROUP)],
                             axis=0)
        res = []
        for kh in range(N_KV_HEADS):
            own = (klane >= kh * HEAD_DIM) & (klane < (kh + 1) * HEAD_DIM)
            kk = jnp.where(own, kc, jnp.zeros_like(kc))
            sc = lax.dot_general(q4, kk, (((1,), (1,)), ((), ())), preferred_element_type=F32) + bias
            snk = jnp.concatenate([jnp.full((kb, 1), sink_ref[kh * GROUP + g], F32) for g in range(GROUP)],
                                  axis=0)
            m = jnp.maximum(jnp.max(sc, axis=-1, keepdims=True), snk)
            p = jnp.exp(sc - m).astype(BF16)
            pv = jnp.dot(p, vc, preferred_element_type=F32)
            denom = pv[:, KV_W:KV_W + 1] + jnp.exp(snk - m)
            res.append(pv[:, 0:KV_W] / denom)
        out = res[0]
        for kh in range(1, N_KV_HEADS):
            out = jnp.where(olane >= kh * HEAD_DIM, res[kh], out)
        out = out.astype(BF16)
        for g in range(GROUP):
            o_ref[s * kb:(s + 1) * kb, g * KV_W:(g + 1) * KV_W] = out[g * kb:(g + 1) * kb, :]


def _attn_call(q, kv, sink, groups):
    n = q.shape[0]
    tq, kb = TQ_ATTN, KEY_BLOCK
    per = tq // kb
    nkb = n // kb
    meta = _tile_meta(groups, kb)
    gs = pltpu.PrefetchScalarGridSpec(
        num_scalar_prefetch=1, grid=(n // tq,),
        in_specs=[pl.BlockSpec(memory_space=pltpu.SMEM),
                  pl.BlockSpec((tq, ATTN_W), lambda i, m: (i, 0)),
                  pl.BlockSpec((kb, 2 * KV_W), lambda i, m: (jnp.maximum(i * per - 1, 0), 0)),
                  pl.BlockSpec((tq, 2 * KV_W), lambda i, m: (i, 0)),
                  pl.BlockSpec((kb, 2 * KV_W), lambda i, m: (jnp.minimum((i + 1) * per, nkb - 1), 0))],
        out_specs=pl.BlockSpec((tq, ATTN_W), lambda i, m: (i, 0)),
        scratch_shapes=[pltpu.VMEM((tq + 2 * kb, 2 * KV_W), BF16)])
    return pl.pallas_call(
        _attn_kernel, grid_spec=gs, name="window_attn",
        out_shape=jax.ShapeDtypeStruct((n, ATTN_W), BF16),
        compiler_params=_params())(meta, sink, q, kv, kv, kv)


def _mixer_kernel(na, meta_ref, xa_ref, xpa_ref, xna_ref, xb_ref, xpb_ref, xnb_ref, o_ref, g_ref, w_ref,
                  cw_ref, wa_ref, wc_ref, wm_ref, h_ref):
    i = pl.program_id(0)
    tm = xa_ref.shape[0]
    g = g_ref[...]
    in_a = i < na
    x = jnp.where(in_a, xa_ref[...], xb_ref[...])
    xb = _rms(x, g).astype(BF16)
    proj = jnp.dot(xb, w_ref[...], preferred_element_type=F32)
    cb = proj[:, 0:CONV_W]
    u = proj[:, CONV_W:2 * CONV_W] * proj[:, 2 * CONV_W:3 * CONV_W]
    xh = jnp.concatenate([jnp.where(in_a, xpa_ref[...], xpb_ref[...]),
                          jnp.where(in_a, xna_ref[...], xnb_ref[...])], axis=0)
    xhb = _rms(xh, g).astype(BF16)
    ph = jnp.dot(xhb, w_ref[:, CONV_W:3 * CONV_W], preferred_element_type=F32)
    uh = ph[:, 0:CONV_W] * ph[:, CONV_W:2 * CONV_W]
    first = meta_ref[1, i]
    last = meta_ref[2, i]
    u_prev = jnp.where(first == 1, 0.0, uh[SUBLANES - 1:SUBLANES, :])
    u_next = jnp.where(last == 1, 0.0, uh[SUBLANES:SUBLANES + 1, :])
    row = lax.broadcasted_iota(jnp.int32, (tm, 1), 0)
    up = jnp.where(row == 0, u_prev, pltpu.roll(u, 1, 0))
    dn = jnp.where(row == tm - 1, u_next, pltpu.roll(u, tm - 1, 0))
    cw = cw_ref[...]
    y = up * cw[0:1, :] + u * cw[1:2, :] + dn * cw[2:3, :]
    conv = (cb * y).astype(BF16)
    conv_br = jnp.dot(conv, wc_ref[...], preferred_element_type=F32)
    attn_br = jnp.dot(o_ref[...], wa_ref[...], preferred_element_type=F32)
    g0 = jax.nn.sigmoid(proj[:, 3 * CONV_W:3 * CONV_W + D_MODEL])
    g1 = jax.nn.sigmoid(proj[:, 3 * CONV_W + D_MODEL:REST_W])
    merged = (g0 * attn_br + g1 * conv_br).astype(BF16)
    h_ref[...] = x + jnp.dot(merged, wm_ref[...], preferred_element_type=F32)


def _mixer_call(xa, xb, o, g_mix, w_rest, conv_w, w_attn_br, w_conv_br, w_mix_out, groups):
    tm = TM_MIX
    per = tm // SUBLANES
    na, nb = xa.shape[0] // tm, xb.shape[0] // tm
    n = xa.shape[0] + xb.shape[0]
    meta = _tile_meta(groups, tm)

    def x_specs(tile_of, n_tiles):
        last8 = n_tiles * per - 1
        return [pl.BlockSpec((tm, D_MODEL), lambda i, m: (jnp.clip(tile_of(i), 0, n_tiles - 1), 0)),
                pl.BlockSpec((SUBLANES, D_MODEL), lambda i, m: (jnp.clip(tile_of(i) * per - 1, 0, last8), 0)),
                pl.BlockSpec((SUBLANES, D_MODEL), lambda i, m: (jnp.clip((tile_of(i) + 1) * per, 0, last8), 0))]

    gs = pltpu.PrefetchScalarGridSpec(
        num_scalar_prefetch=1, grid=(n // tm,),
        in_specs=x_specs(lambda i: i, na) + x_specs(lambda i: i - na, nb) + [
                  pl.BlockSpec((tm, ATTN_W), lambda i, m: (i, 0)),
                  _const_spec((1, D_MODEL)),
                  _const_spec((D_MODEL, REST_W)),
                  _const_spec((3, CONV_W)),
                  _const_spec((ATTN_W, D_MODEL)),
                  _const_spec((CONV_W, D_MODEL)),
                  _const_spec((D_MODEL, D_MODEL))],
        out_specs=pl.BlockSpec((tm, D_MODEL), lambda i, m: (i, 0)))
    return pl.pallas_call(
        functools.partial(_mixer_kernel, na), grid_spec=gs, name="mixer",
        out_shape=jax.ShapeDtypeStruct((n, D_MODEL), F32),
        compiler_params=_params())(meta, xa, xa, xa, xb, xb, xb, o, g_mix, w_rest, conv_w, w_attn_br,
                                   w_conv_br, w_mix_out)


def _memkv_kernel(na, mema_ref, memb_ref, g_ref, w_ref, kv_ref):
    mem = jnp.where(pl.program_id(0) < na, mema_ref[0], memb_ref[0])
    mn = _rms(mem, g_ref[...]).astype(BF16)
    kv_ref[0] = jnp.dot(mn, w_ref[...], preferred_element_type=F32).astype(BF16)


def _memkv_call(mem_a, mem_b, g_mem, w_xkv):
    na, nb = mem_a.shape[0], mem_b.shape[0]
    return pl.pallas_call(
        functools.partial(_memkv_kernel, na), grid=(na + nb,), name="mem_kv",
        in_specs=[pl.BlockSpec((1, N_MEM, D_MODEL), lambda b: (jnp.minimum(b, na - 1), 0, 0)),
                  pl.BlockSpec((1, N_MEM, D_MODEL), lambda b: (jnp.maximum(b - na, 0), 0, 0)),
                  _const_spec((1, D_MODEL)),
                  _const_spec((D_MODEL, 2 * D_MODEL))],
        out_specs=pl.BlockSpec((1, N_MEM, 2 * D_MODEL), lambda b: (b, 0, 0)),
        out_shape=jax.ShapeDtypeStruct((na + nb, N_MEM, 2 * D_MODEL), BF16),
        compiler_params=_params())(mem_a, mem_b, g_mem, w_xkv)


def _cross_kernel(meta_ref, h_ref, kv_ref, gc_ref, wq_ref, wo_ref, gm_ref, wr_ref, br_ref,
                  h2_ref, xs_ref, idx_ref, tot_ref):
    del meta_ref
    tm = h_ref.shape[0]
    h = h_ref[...]
    hn = _rms(h, gc_ref[...]).astype(BF16)
    q = (jnp.dot(hn, wq_ref[...], preferred_element_type=F32) * (X_HEAD_DIM ** -0.5)).astype(BF16)
    outs = []
    for hd in range(X_HEADS):
        qh = q[:, hd * X_HEAD_DIM:(hd + 1) * X_HEAD_DIM]
        kh = kv_ref[0, :, hd * X_HEAD_DIM:(hd + 1) * X_HEAD_DIM]
        vh = kv_ref[0, :, D_MODEL + hd * X_HEAD_DIM:D_MODEL + (hd + 1) * X_HEAD_DIM]
        s = lax.dot_general(qh, kh, (((1,), (1,)), ((), ())), preferred_element_type=F32)
        m = jnp.max(s, axis=-1, keepdims=True)
        p = jnp.exp(s - m)
        p = (p / jnp.sum(p, axis=-1, keepdims=True)).astype(BF16)
        outs.append(jnp.dot(p, vh, preferred_element_type=F32).astype(BF16))
    o = jnp.concatenate(outs, axis=1)
    h2 = h + jnp.dot(o, wo_ref[...], preferred_element_type=F32)
    h2_ref[...] = h2

    hn3 = _rms(h2, gm_ref[...])
    hi = hn3.astype(BF16)
    hi32 = hi.astype(F32)
    lo = (hn3 - hi32).astype(BF16)
    xs_ref[:, 0:D_MODEL] = hn3

    nt = (((1,), (1,)), ((), ()))
    r1 = lax.dot_general(wr_ref[...], hi, nt, preferred_element_type=F32)
    r2 = lax.dot_general(wr_ref[...], lo, nt, preferred_element_type=F32)
    e = N_EXPERTS
    logits = ((r2[e:2 * e] + r2[0:e]) + r1[e:2 * e]) + r1[0:e] + br_ref[...]
    eio = lax.broadcasted_iota(jnp.int32, (e, tm), 0)
    cur = logits
    vals, sels = [], []
    for _ in range(TOP_K):
        mx = jnp.max(cur, axis=0, keepdims=True)
        sel = jnp.min(jnp.where(cur == mx, eio, e), axis=0, keepdims=True)
        vals.append(mx)
        sels.append(sel)
        cur = jnp.where(eio == sel, -jnp.inf, cur)
    ex = [jnp.exp(v - vals[0]) for v in vals]
    tot = ex[0] + ex[1] + ex[2] + ex[3]
    idx_ref[...] = jnp.concatenate(sels, axis=0)
    assign = jnp.where(eio == sels[0], 1.0, 0.0)
    for k in range(1, TOP_K):
        assign = assign + jnp.where(eio == sels[k], 1.0, 0.0)
    segs = jnp.zeros((e, 1), jnp.int32)
    for j in range(tm // TC_COMB):
        cnt = jnp.sum(assign[:, j * TC_COMB:(j + 1) * TC_COMB], axis=1, keepdims=True).astype(jnp.int32)
        segs = segs + lax.shift_left(lax.shift_right_logical(cnt + (SEG_ALIGN - 1), LOG2_SEG), LOG2_SEG)

    @pl.when(pl.program_id(0) == 0)
    def _():
        tot_ref[...] = jnp.zeros_like(tot_ref)
    tot_ref[...] += jnp.broadcast_to(segs, tot_ref.shape)
    gws = [x / tot for x in ex]
    w_hi = [g.astype(BF16).astype(F32) for g in gws]
    w_lo = [(g - h).astype(BF16).astype(F32) for g, h in zip(gws, w_hi)]
    wt = jnp.transpose(jnp.concatenate(w_hi + w_lo, axis=0))
    et = jnp.transpose(jnp.concatenate([s.astype(F32) for s in sels]
                                       + [jnp.zeros((SUBLANES - TOP_K, tm), F32)], axis=0))
    xs_ref[:, D_MODEL:XS_W] = jnp.concatenate(
        [wt, et, jnp.zeros((tm, LANES - 2 * SUBLANES), F32)], axis=1)


def _cross_call(h1, memkv, g_cross, w_xq, w_xo, g_moe, w_r2t, b_router, groups):
    n = h1.shape[0]
    tm = TM_CROSS
    meta = _tile_meta(groups, tm)
    row = lambda i, m: (i, 0)
    col = lambda i, m: (0, i)
    gs = pltpu.PrefetchScalarGridSpec(
        num_scalar_prefetch=1, grid=(n // tm,),
        in_specs=[pl.BlockSpec((tm, D_MODEL), row),
                  pl.BlockSpec((1, N_MEM, 2 * D_MODEL), lambda i, m: (m[3, i], 0, 0)),
                  _const_spec((1, D_MODEL)),
                  _const_spec((D_MODEL, D_MODEL)),
                  _const_spec((D_MODEL, D_MODEL)),
                  _const_spec((1, D_MODEL)),
                  _const_spec((2 * N_EXPERTS, D_MODEL)),
                  _const_spec((N_EXPERTS, 1))],
        out_specs=[pl.BlockSpec((tm, D_MODEL), row),
                   pl.BlockSpec((tm, XS_W), row),
                   pl.BlockSpec((TOP_K, tm), col),
                   pl.BlockSpec((N_EXPERTS, LANES), lambda i, m: (0, 0))])
    return pl.pallas_call(
        _cross_kernel, grid_spec=gs, name="cross_router",
        out_shape=[jax.ShapeDtypeStruct((n, D_MODEL), F32),
                   jax.ShapeDtypeStruct((n, XS_W), F32),
                   jax.ShapeDtypeStruct((TOP_K, n), jnp.int32),
                   jax.ShapeDtypeStruct((N_EXPERTS, LANES), jnp.int32)],
        compiler_params=_params())(meta, h1, memkv, g_cross, w_xq, w_xo, g_moe, w_r2t, b_router)


LOG2_BM = BM_MOE.bit_length() - 1
assert 1 << LOG2_BM == BM_MOE
SEG_ALIGN = 4
LOG2_SEG = SEG_ALIGN.bit_length() - 1
COPY_ROWS = tuple(SEG_ALIGN << j for j in (3, 2, 1, 0))
K_ROWS = 1152
M_CHUNK = 576
DESC_ROWS = 2 * SUBLANES
assert 1 << LOG2_SEG == SEG_ALIGN and K_ROWS % M_CHUNK == 0 and K_ROWS // COPY_ROWS[0] <= LANES
assert K_ROWS >= TOP_K * TC_COMB + N_EXPERTS * (SEG_ALIGN - 1)
DESC_COUNT = 2 * len(COPY_ROWS)
assert DESC_COUNT < DESC_ROWS and all(a == 2 * b for a, b in zip(COPY_ROWS, COPY_ROWS[1:]))
AUX_W_HI, AUX_W_LO, AUX_EID = 0, TOP_K, 2 * TOP_K
XS_W = D_MODEL + LANES


def _sublane_scan(x, n):
    row = lax.broadcasted_iota(jnp.int32, x.shape, 0)
    sft = 1
    while sft < n:
        x = x + jnp.where(row >= sft, pltpu.roll(x, sft, 0), 0)
        sft *= 2
    return x


def _owner(ends, lane):
    ebl = lax.broadcasted_iota(jnp.int32, ends.shape, 0)
    owner = jnp.sum(jnp.where(ends <= lane, 1.0, 0.0), axis=0, keepdims=True).astype(jnp.int32)
    return ebl == owner


def _pick(sel, val):
    return jnp.sum(jnp.where(sel, val.astype(F32), 0.0), axis=0, keepdims=True).astype(jnp.int32)


def _plan_kernel(idx_ref, tot_ref, kpos_ref, desc_ref, meta_ref, tri_ref, run_ref):
    i = pl.program_id(0)
    e, tp = N_EXPERTS, TC_COMB
    nb_pad = meta_ref.shape[1]
    eio = lax.broadcasted_iota(jnp.int32, (e, tp), 0)
    idx = idx_ref[...]
    onehots = [eio == idx[k:k + 1, :] for k in range(TOP_K)]
    s = jnp.where(onehots[0], 1.0, 0.0)
    for k in range(1, TOP_K):
        s = s + jnp.where(onehots[k], 1.0, 0.0)
    cnt_t = jnp.sum(s, axis=1, keepdims=True).astype(jnp.int32)
    cnt = jnp.broadcast_to(cnt_t, (e, LANES))
    seg = lax.shift_left(lax.shift_right_logical(cnt + (SEG_ALIGN - 1), LOG2_SEG), LOG2_SEG)

    @pl.when(i == 0)
    def _():
        r = lax.broadcasted_iota(jnp.int32, (tp, tp), 0)
        c = lax.broadcasted_iota(jnp.int32, (tp, tp), 1)
        tri_ref[...] = jnp.where(r < c, 1.0, 0.0).astype(BF16)
        tot = tot_ref[...]
        pad = lax.shift_left(lax.shift_right_logical(tot + (BM_MOE - 1), LOG2_BM), LOG2_BM)
        pad_end = _sublane_scan(pad, e)
        pad_start = pad_end - pad
        run_ref[...] = pad_start
        bpos = lax.broadcasted_iota(jnp.int32, (e, nb_pad), 1) * BM_MOE
        ebl = lax.broadcasted_iota(jnp.int32, (e, nb_pad), 0)
        done = jnp.where(pad_end[:, 0:1] <= bpos, 1.0, 0.0)
        bexp = jnp.minimum(jnp.sum(done, axis=0, keepdims=True).astype(jnp.int32), e - 1)
        row_end = (pad_start + tot)[:, 0:1].astype(F32)
        rend_b = jnp.sum(jnp.where(ebl == bexp, row_end, 0.0), axis=0, keepdims=True).astype(jnp.int32)
        nvalid = jnp.clip(rend_b - bpos[0:1, :], 0, BM_MOE)
        total = jnp.sum(jnp.where(ebl == e - 1, pad_end[:, 0:1].astype(F32), 0.0), axis=0, keepdims=True)
        nused = lax.shift_right_logical(total.astype(jnp.int32), LOG2_BM)
        meta_ref[...] = jnp.concatenate(
            [bexp, nvalid, nused, jnp.zeros((SUBLANES - 3, nb_pad), jnp.int32)], axis=0)

    before = jnp.dot(s.astype(BF16), tri_ref[...], preferred_element_type=F32)
    kbase = _sublane_scan(seg, e) - seg
    pos = kbase[:, 0:1].astype(F32) + before
    kp = [jnp.sum(jnp.where(onehots[k], pos, 0.0), axis=0, keepdims=True).astype(jnp.int32)
          for k in range(TOP_K)]
    kpos_ref[0] = jnp.concatenate(kp + [jnp.zeros((SUBLANES - TOP_K, tp), jnp.int32)], axis=0)

    start = run_ref[...]
    lane = lax.broadcasted_iota(jnp.int32, (e, LANES), 1)
    lane1 = lane[0:1, :]
    lists = []
    counts = jnp.zeros((1, LANES), jnp.int32)
    done = jnp.zeros_like(seg)
    for ci, rows in enumerate(COPY_ROWS):
        n = lax.shift_right_logical(seg - done, rows.bit_length() - 1)
        end = _sublane_scan(n, e)
        sel = _owner(end, lane)
        off = _pick(sel, done) + (lane1 - _pick(sel, end - n)) * rows
        n_c = end[e - 1:e, :]
        ok = lane1 < n_c
        lists += [jnp.where(ok, _pick(sel, start) + off, 0), jnp.where(ok, _pick(sel, kbase) + off, 0)]
        counts = jnp.where(lane1 == ci, n_c, counts)
        done = done + n * rows
    desc_ref[0] = jnp.concatenate(
        lists + [counts, jnp.zeros((DESC_ROWS - DESC_COUNT - 1, LANES), jnp.int32)], axis=0)
    run_ref[...] = run_ref[...] + seg


def _plan_call(idx, tot, n_blocks):
    n = idx.shape[1]
    tp = TC_COMB
    n_tiles = n // tp
    nb_pad = -(-n_blocks // LANES) * LANES
    tile = lambda i: (i, 0, 0)
    return pl.pallas_call(
        _plan_kernel, grid=(n_tiles,), name="route_plan",
        in_specs=[pl.BlockSpec((TOP_K, tp), lambda i: (0, i)), _const_spec((N_EXPERTS, LANES))],
        out_specs=[pl.BlockSpec((1, SUBLANES, tp), tile),
                   pl.BlockSpec((1, DESC_ROWS, LANES), tile),
                   pl.BlockSpec((SUBLANES, nb_pad), lambda i: (0, 0))],
        out_shape=[jax.ShapeDtypeStruct((n_tiles, SUBLANES, tp), jnp.int32),
                   jax.ShapeDtypeStruct((n_tiles, DESC_ROWS, LANES), jnp.int32),
                   jax.ShapeDtypeStruct((SUBLANES, nb_pad), jnp.int32)],
        scratch_shapes=[pltpu.VMEM((tp, tp), BF16), pltpu.VMEM((N_EXPERTS, LANES), jnp.int32)],
        compiler_params=_params())(idx, tot)


def _slot_hits(kp, iota):
    hit = jnp.where(iota == kp[0], 1.0, 0.0)
    for k in range(1, TOP_K):
        hit = hit + jnp.where(iota == kp[k], 1.0, 0.0)
    return hit


def _for_each_copy(count, fn):
    for ci, rows in enumerate(COPY_ROWS):
        def body(j, carry, ci=ci, rows=rows):
            fn(2 * ci, j, rows)
            return carry
        lax.fori_loop(0, count(ci), body, 0)


def _dispatch_kernel(n_tiles, n_blocks, nvalid_ref, desc_ref, kpos_ref, xs_ref, out_hbm, zero_ref, buf,
                     sems, zsem, issued_ref):
    i = pl.program_id(0)
    td = xs_ref.shape[0]
    bg = zero_ref.shape[0]
    cur = i % 2

    def copy(b, dst, src, rows):
        groups = rows // SEG_ALIGN
        return pltpu.make_async_copy(buf.at[b, pl.ds(lax.shift_right_logical(src, LOG2_SEG), groups)],
                                     out_hbm.at[pl.ds(lax.shift_right_logical(dst, LOG2_SEG), groups)], sems.at[b])

    def drain(b):
        _for_each_copy(lambda ci: issued_ref[b, ci], lambda row, j, rows: copy(b, 0, 0, rows).wait())

    @pl.when(i == 0)
    def _():
        zero_ref[...] = jnp.zeros_like(zero_ref)

        def fill(b):
            return pltpu.make_async_copy(zero_ref, out_hbm.at[pl.ds(b * bg, bg)], zsem)

        def start(b, carry):
            @pl.when(nvalid_ref[b] < bg * SEG_ALIGN)
            def _():
                fill(b).start()
            return carry

        def wait(b, carry):
            @pl.when(nvalid_ref[b] < bg * SEG_ALIGN)
            def _():
                fill(b).wait()
            return carry
        lax.fori_loop(0, n_blocks, start, 0)
        lax.fori_loop(0, n_blocks, wait, 0)

    @pl.when(i >= 2)
    def _():
        drain(cur)

    kp = kpos_ref[0]
    x = xs_ref[...].astype(BF16)
    for c in range(K_ROWS // M_CHUNK):
        rows = lax.broadcasted_iota(jnp.int32, (M_CHUNK, td), 0) + c * M_CHUNK
        sel = _slot_hits([kp[k:k + 1, :] for k in range(TOP_K)], rows).astype(BF16)
        mg = M_CHUNK // SEG_ALIGN
        buf[cur, c * mg:(c + 1) * mg] = jnp.dot(sel, x, preferred_element_type=F32).reshape(mg, SEG_ALIGN, -1)

    _for_each_copy(lambda ci: desc_ref[0, DESC_COUNT, ci],
                   lambda row, j, rows: copy(cur, desc_ref[0, row, j], desc_ref[0, row + 1, j], rows).start())
    for ci in range(len(COPY_ROWS)):
        issued_ref[cur, ci] = desc_ref[0, DESC_COUNT, ci]

    @pl.when(i == n_tiles - 1)
    def _():
        drain(cur)

        @pl.when(i >= 1)
        def _():
            drain(1 - cur)


def _dispatch_call(nvalid, desc, kpos, xs, n_blocks):
    n, width = xs.shape
    td, bm = TC_COMB, BM_MOE
    n_tiles = n // td
    gs = pltpu.PrefetchScalarGridSpec(
        num_scalar_prefetch=1, grid=(n_tiles,),
        in_specs=[pl.BlockSpec((1, DESC_ROWS, LANES), lambda i, nv: (i, 0, 0), memory_space=pltpu.SMEM),
                  pl.BlockSpec((1, SUBLANES, td), lambda i, nv: (i, 0, 0)),
                  pl.BlockSpec((td, width), lambda i, nv: (i, 0))],
        out_specs=pl.BlockSpec(memory_space=pl.ANY),
        scratch_shapes=[pltpu.VMEM((bm // SEG_ALIGN, SEG_ALIGN, width), xs.dtype),
                        pltpu.VMEM((2, K_ROWS // SEG_ALIGN, SEG_ALIGN, width), xs.dtype),
                        pltpu.SemaphoreType.DMA((2,)), pltpu.SemaphoreType.DMA(()),
                        pltpu.SMEM((2, len(COPY_ROWS)), jnp.int32)])
    return pl.pallas_call(
        functools.partial(_dispatch_kernel, n_tiles, n_blocks), grid_spec=gs, name="moe_dispatch",
        out_shape=jax.ShapeDtypeStruct((n_blocks * bm // SEG_ALIGN, SEG_ALIGN, width), xs.dtype),
        compiler_params=_params())(nvalid, desc, kpos, xs)


def _ffn_kernel(bexp_ref, nused_ref, x_ref, wg_hbm, bg_ref, wu_hbm, bu_ref, wd_hbm, bd_ref,
                y_ref, wf32, wbf, wsems, slot_ref):
    b = pl.program_id(0)
    nused = nused_ref[0]
    e = bexp_ref[b]
    wgb, wub, wdb = wbf.at[0], wbf.at[1], wbf.at[2]

    def fetch(expert, slot):
        return [pltpu.make_async_copy(src.at[expert], wf32.at[slot, j], wsems.at[slot])
                for j, src in enumerate((wg_hbm, wu_hbm, wd_hbm))]

    @pl.when(b == 0)
    def _():
        slot_ref[0] = 0
        for d in fetch(e, 0):
            d.start()

    @pl.when((b < nused) & ((b == 0) | (e != bexp_ref[jnp.maximum(b - 1, 0)])))
    def _():
        slot = slot_ref[0]
        for d in fetch(e, slot):
            d.wait()
        for j in range(3):
            wbf[j] = wf32[slot, j].astype(BF16)
        nxt = lax.while_loop(lambda j: (j < nused) & (bexp_ref[jnp.minimum(j, nused - 1)] == e),
                             lambda j: j + 1, b + 1)

        @pl.when(nxt < nused)
        def _():
            for d in fetch(bexp_ref[nxt], 1 - slot):
                d.start()
        slot_ref[0] = 1 - slot

    @pl.when(b < nused)
    def _():
        rows = x_ref[...].reshape(-1, XS_W)
        x = rows[:, 0:D_MODEL].astype(BF16)
        aux = rows[:, D_MODEL:XS_W]
        eid = bexp_ref[b].astype(F32)
        row_w = jnp.zeros((x.shape[0], 1), F32)
        for k in range(TOP_K):
            w_k = aux[:, AUX_W_HI + k:AUX_W_HI + k + 1] + aux[:, AUX_W_LO + k:AUX_W_LO + k + 1]
            row_w = row_w + jnp.where(aux[:, AUX_EID + k:AUX_EID + k + 1] == eid, w_k, 0.0)

        def proj(w_ref, b_ref):
            return jnp.dot(x, w_ref[...], preferred_element_type=F32) + b_ref[0]

        a = jnp.minimum(proj(wgb, bg_ref), SWIGLU_LIMIT)
        u = jnp.clip(proj(wub, bu_ref), -SWIGLU_LIMIT, SWIGLU_LIMIT)
        hid = (a * jax.nn.sigmoid(SWIGLU_ALPHA * a) * (u + 1.0)).astype(BF16)
        y = (jnp.dot(hid, wdb[...], preferred_element_type=F32) + bd_ref[0]) * row_w
        y_ref[...] = y.reshape(y_ref.shape)

    @pl.when(b >= nused)
    def _():
        y_ref[...] = jnp.zeros_like(y_ref)


def _ffn_call(bexp, nused, xs_sorted, w_gate, b_gate, w_up, b_up, w_down, b_down):
    bm = BM_MOE
    bg = bm // SEG_ALIGN
    n_blocks = xs_sorted.shape[0] // bg
    assert D_FF == D_MODEL, "the three expert matrices share one staging shape"
    bspec = lambda c: pl.BlockSpec((1, 1, c), lambda b, be, nu: (be[b], 0, 0))
    whole = pl.BlockSpec(memory_space=pl.ANY)
    gs = pltpu.PrefetchScalarGridSpec(
        num_scalar_prefetch=2, grid=(n_blocks,),
        in_specs=[pl.BlockSpec((bg, SEG_ALIGN, XS_W), lambda b, be, nu: (jnp.minimum(b, nu[0] - 1), 0, 0)),
                  whole, bspec(D_FF), whole, bspec(D_FF), whole, bspec(D_MODEL)],
        out_specs=pl.BlockSpec((bg, SEG_ALIGN, D_MODEL), lambda b, be, nu: (b, 0, 0)),
        scratch_shapes=[pltpu.VMEM((2, 3, D_MODEL, D_FF), F32), pltpu.VMEM((3, D_MODEL, D_FF), BF16),
                        pltpu.SemaphoreType.DMA((2,)), pltpu.SMEM((1,), jnp.int32)])
    return pl.pallas_call(
        _ffn_kernel, grid_spec=gs, name="expert_ffn",
        out_shape=jax.ShapeDtypeStruct((n_blocks * bg, SEG_ALIGN, D_MODEL), F32),
        compiler_params=_params())(bexp, nused, xs_sorted, w_gate, b_gate, w_up, b_up, w_down, b_down)


def _combine_kernel(nt, desc_ref, descn_ref, kpos_ref, y_hbm, h_ref, gf_ref, out_ref, ybuf, sems):
    i = pl.program_id(0)
    tc = h_ref.shape[0]
    cur = i % 2

    def copy(b, dst, src, rows):
        groups = rows // SEG_ALIGN
        return pltpu.make_async_copy(y_hbm.at[pl.ds(lax.shift_right_logical(dst, LOG2_SEG), groups)],
                                     ybuf.at[b, pl.ds(lax.shift_right_logical(src, LOG2_SEG), groups)], sems.at[b])

    def fetch(b, ref):
        _for_each_copy(lambda ci: ref[0, DESC_COUNT, ci],
                       lambda row, j, rows: copy(b, ref[0, row, j], ref[0, row + 1, j], rows).start())

    @pl.when(i == 0)
    def _():
        ybuf[...] = jnp.zeros_like(ybuf)
        fetch(0, desc_ref)

    @pl.when(i + 1 < nt)
    def _():
        fetch(1 - cur, descn_ref)

    _for_each_copy(lambda ci: desc_ref[0, DESC_COUNT, ci], lambda row, j, rows: copy(cur, 0, 0, rows).wait())

    kpt = jnp.transpose(kpos_ref[0].astype(F32))
    kp = [kpt[:, k:k + 1] for k in range(TOP_K)]
    acc = h_ref[...]
    for c in range(K_ROWS // M_CHUNK):
        cols = (lax.broadcasted_iota(jnp.int32, (tc, M_CHUNK), 1) + c * M_CHUNK).astype(F32)
        pick = _slot_hits(kp, cols).astype(BF16)
        mg = M_CHUNK // SEG_ALIGN
        y = ybuf[cur, c * mg:(c + 1) * mg].reshape(M_CHUNK, -1).astype(BF16)
        acc = acc + jnp.dot(pick, y, preferred_element_type=F32)
    out_ref[...] = _rms(acc, gf_ref[...])


def _combine_call(desc, kpos, y, h2, g_final, tile0, n_tiles):
    tc = TC_COMB
    last = tile0 + n_tiles - 1
    return pl.pallas_call(
        functools.partial(_combine_kernel, n_tiles), grid=(n_tiles,), name="moe_combine",
        in_specs=[pl.BlockSpec((1, DESC_ROWS, LANES), lambda i: (tile0 + i, 0, 0), memory_space=pltpu.SMEM),
                  pl.BlockSpec((1, DESC_ROWS, LANES), lambda i: (jnp.minimum(tile0 + i + 1, last), 0, 0),
                               memory_space=pltpu.SMEM),
                  pl.BlockSpec((1, SUBLANES, tc), lambda i: (tile0 + i, 0, 0)),
                  pl.BlockSpec(memory_space=pl.ANY),
                  pl.BlockSpec((tc, D_MODEL), lambda i: (tile0 + i, 0)),
                  _const_spec((1, D_MODEL))],
        out_specs=pl.BlockSpec((tc, D_MODEL), lambda i: (i, 0)),
        out_shape=jax.ShapeDtypeStruct((n_tiles * tc, D_MODEL), F32),
        scratch_shapes=[pltpu.VMEM((2, K_ROWS // SEG_ALIGN, SEG_ALIGN, D_MODEL), F32),
                        pltpu.SemaphoreType.DMA((2,))],
        compiler_params=_params())(desc, desc, kpos, y, h2, g_final)


def _forward(xs, mems, g_mix, w_in, sink, conv_w, w_attn_br, w_conv_br, w_mix_out, g_cross, g_mem,
             w_xq, w_xkv, w_xo, g_moe, w_router, b_router, w_gate, b_gate, w_up, b_up, w_down,
             b_down, g_final):
    groups = [(x.shape[0], x.shape[1]) for x in xs]
    for _, s in groups:
        assert s % max(TM_QKV, TQ_ATTN, TM_MIX, TM_CROSS, TC_COMB) == 0
    assert len(xs) == 2, "two request groups"
    xa, xb = (x.reshape(-1, D_MODEL) for x in xs)
    n = xa.shape[0] + xb.shape[0]
    tables = _rope_tables(max(s for _, s in groups))
    assert w_in.shape[0] == 1, "single-layer trunk: the final norm is fused into the combine kernel"
    l = 0
    row2 = lambda v: v.reshape(1, -1)
    w_in_b = w_in[l].astype(BF16)
    perm = np.array([(kv * GROUP + g) * HEAD_DIM + d
                     for g in range(GROUP) for kv in range(N_KV_HEADS) for d in range(HEAD_DIM)])
    w_qkv = jnp.concatenate([w_in_b[:, :ATTN_W][:, perm], w_in_b[:, ATTN_W:QKV_W]], axis=1)
    q, kv = _qkv_call(xa, xb, row2(g_mix[l]), w_qkv, tables, groups)
    o = _attn_call(q, kv, sink[l], groups)
    h = _mixer_call(xa, xb, o, row2(g_mix[l]), w_in_b[:, QKV_W:], conv_w[l],
                    w_attn_br[l].astype(BF16)[perm, :], w_conv_br[l].astype(BF16),
                    w_mix_out[l].astype(BF16), groups)
    memkv = _memkv_call(mems[0], mems[1], row2(g_mem[l]), w_xkv[l].astype(BF16))
    wr = w_router[l]
    wr_hi = wr.astype(BF16)
    wr_lo = (wr - wr_hi.astype(F32)).astype(BF16)
    w_r2t = jnp.concatenate([wr_hi.T, wr_lo.T], axis=0)
    h2, xs_rows, idx, tot = _cross_call(h, memkv, row2(g_cross[l]), w_xq[l].astype(BF16),
                                        w_xo[l].astype(BF16), row2(g_moe[l]), w_r2t,
                                        b_router[l].reshape(-1, 1), groups)
    seg_rows = TOP_K * n + (n // TC_COMB) * N_EXPERTS * (SEG_ALIGN - 1)
    n_blocks = -(-seg_rows // BM_MOE) + N_EXPERTS
    kpos, desc, meta = _plan_call(idx, tot, n_blocks)
    xs_sorted = _dispatch_call(meta[1], desc, kpos, xs_rows, n_blocks)
    y = _ffn_call(meta[0], meta[2, 0:1], xs_sorted, w_gate[l], b_gate[l][:, None, :],
                  w_up[l], b_up[l][:, None, :], w_down[l], b_down[l][:, None, :])
    tc = TC_COMB
    outs, t0 = [], 0
    for nb, s in groups:
        nt = nb * s // tc
        outs.append(_combine_call(desc, kpos, y, h2, row2(g_final), t0, nt))
        t0 += nt
    return tuple(o.reshape(x.shape) for o, x in zip(outs, xs))


def kernel(x_prompt, x_sample, mem_prompt, mem_sample, g_mix, w_in, sink, conv_w, w_attn_br, w_conv_br,
           w_mix_out, g_cross, g_mem, w_xq, w_xkv, w_xo, g_moe, w_router, b_router, w_gate, b_gate,
           w_up, b_up, w_down, b_down, g_final):
    return _forward([x_prompt, x_sample], [mem_prompt, mem_sample], g_mix, w_in, sink, conv_w,
                    w_attn_br, w_conv_br, w_mix_out, g_cross, g_mem, w_xq, w_xkv, w_xo, g_moe,
                    w_router, b_router, w_gate, b_gate, w_up, b_up, w_down, b_down, g_final)
```

```python
import functools

import numpy as np
import jax
import jax.numpy as jnp
from jax import lax
from jax.experimental import pallas as pl
from jax.experimental.pallas import tpu as pltpu

F32 = jnp.float32
BF16 = jnp.bfloat16

D_MODEL = 1024
N_HEADS = 8
N_KV_HEADS = 2
HEAD_DIM = 64
GROUP = N_HEADS // N_KV_HEADS
ATTN_W = N_HEADS * HEAD_DIM
KV_W = N_KV_HEADS * HEAD_DIM
QKV_W = ATTN_W + 2 * KV_W
WINDOW = 128
ROT_DIM = HEAD_DIM // 4
ROPE_THETA = 500000.0
CONV_W = D_MODEL // 2
REST_W = 3 * CONV_W + 2 * D_MODEL
N_MEM = 256
X_HEADS = 4
X_HEAD_DIM = D_MODEL // X_HEADS
N_EXPERTS = 32
TOP_K = 4
D_FF = D_MODEL
SWIGLU_ALPHA = 1.702
SWIGLU_LIMIT = 7.0
EPS = 1e-5

LANES = 128
SUBLANES = 8
KEY_BLOCK = WINDOW
TM_QKV = 1024
TQ_ATTN = 1024
TM_MIX = 512
TM_CROSS = 1024
BM_MOE = 512
TC_COMB = 256
NEG_BIG = -1e30
VMEM_LIMIT = 56 * 1024 * 1024


def _rms(x, g):
    var = jnp.mean(x * x, axis=-1, keepdims=True)
    return x * lax.rsqrt(var + EPS) * g


def _tile_meta(groups, tile):
    pos, first, last, bidx = [], [], [], []
    b0 = 0
    for nb, s in groups:
        per = s // tile
        for b in range(nb):
            for j in range(per):
                pos.append(j)
                first.append(int(j == 0))
                last.append(int(j == per - 1))
                bidx.append(b0 + b)
        b0 += nb
    return jnp.asarray(np.array([pos, first, last, bidx], dtype=np.int32))


def _const_spec(shape):
    nd = len(shape)
    return pl.BlockSpec(shape, lambda *_: (0,) * nd, pipeline_mode=pl.Buffered(1))


def _params(vmem=VMEM_LIMIT):
    return pltpu.CompilerParams(dimension_semantics=("arbitrary",), vmem_limit_bytes=vmem)


def _qkv_kernel(na, meta_ref, xa_ref, xb_ref, g_ref, w_ref, cos_ref, sa_ref, sb_ref, q_ref, kv_ref):
    del meta_ref
    x = jnp.where(pl.program_id(0) < na, xa_ref[...], xb_ref[...])
    xn = _rms(x, g_ref[...]).astype(BF16)
    proj = jnp.dot(xn, w_ref[...], preferred_element_type=F32)
    c, sa, sb = cos_ref[...], sa_ref[...], sb_ref[...]
    n_rot = (ATTN_W + KV_W) // LANES
    for gi in range(n_rot):
        p = proj[:, gi * LANES:(gi + 1) * LANES]
        r = p * c + pltpu.roll(p, LANES - ROT_DIM // 2, 1) * sa + pltpu.roll(p, ROT_DIM // 2, 1) * sb
        if gi < ATTN_W // LANES:
            q_ref[:, gi * LANES:(gi + 1) * LANES] = (r * (HEAD_DIM ** -0.5)).astype(BF16)
        else:
            kv_ref[:, 0:KV_W] = r.astype(BF16)
    kv_ref[:, KV_W:2 * KV_W] = proj[:, ATTN_W + KV_W:QKV_W].astype(BF16)


def _rope_tables(s_max):
    half = ROT_DIM // 2
    inv_freq = ROPE_THETA ** (-(jnp.arange(half, dtype=F32) * 2.0) / ROT_DIM)
    ang = jnp.arange(s_max, dtype=F32)[:, None] * inv_freq[None, :]
    cos, sin = jnp.cos(ang), jnp.sin(ang)
    d = np.arange(LANES) % HEAD_DIM
    j = d % half
    cos_l, sin_l = cos[:, j], sin[:, j]
    rot = jnp.asarray(d < ROT_DIM)[None, :]
    lo = jnp.asarray(d < half)[None, :]
    hi = jnp.asarray((d >= half) & (d < ROT_DIM))[None, :]
    c = jnp.where(rot, cos_l, 1.0)
    sa = jnp.where(lo, -sin_l, 0.0)
    sb = jnp.where(hi, sin_l, 0.0)
    return c, sa, sb


def _qkv_call(xa, xb, g_mix, w_qkv, tables, groups):
    tm = TM_QKV
    na = xa.shape[0] // tm
    n = xa.shape[0] + xb.shape[0]
    meta = _tile_meta(groups, tm)
    row = lambda i, m: (i, 0)
    tab = lambda i, m: (m[0, i], 0)
    gs = pltpu.PrefetchScalarGridSpec(
        num_scalar_prefetch=1, grid=(n // tm,),
        in_specs=[pl.BlockSpec((tm, D_MODEL), lambda i, m: (jnp.minimum(i, na - 1), 0)),
                  pl.BlockSpec((tm, D_MODEL), lambda i, m: (jnp.maximum(i - na, 0), 0)),
                  _const_spec((1, D_MODEL)),
                  _const_spec((D_MODEL, QKV_W)),
                  pl.BlockSpec((tm, LANES), tab), pl.BlockSpec((tm, LANES), tab),
                  pl.BlockSpec((tm, LANES), tab)],
        out_specs=[pl.BlockSpec((tm, ATTN_W), row), pl.BlockSpec((tm, 2 * KV_W), row)])
    return pl.pallas_call(
        functools.partial(_qkv_kernel, na), grid_spec=gs, name="qkv_rope",
        out_shape=[jax.ShapeDtypeStruct((n, ATTN_W), BF16), jax.ShapeDtypeStruct((n, 2 * KV_W), BF16)],
        compiler_params=_params())(meta, xa, xb, g_mix, w_qkv, *tables)


def _attn_kernel(meta_ref, sink_ref, q_ref, kvp_ref, kvm_ref, kvn_ref, o_ref, kcat_ref):
    i = pl.program_id(0)
    kb = KEY_BLOCK
    tq = q_ref.shape[0]
    kcat_ref[0:kb, :] = kvp_ref[...]
    kcat_ref[kb:kb + tq, :] = kvm_ref[...]
    kcat_ref[kb + tq:kb + tq + kb, :] = kvn_ref[...]
    r = lax.broadcasted_iota(jnp.int32, (kb, kb), 0)
    c = lax.broadcasted_iota(jnp.int32, (kb, kb), 1)
    band_prev = jnp.where(c >= r, 0.0, NEG_BIG)
    band_next = jnp.where(c <= r, 0.0, NEG_BIG)
    klane = lax.broadcasted_iota(jnp.int32, (3 * kb, KV_W), 1)
    olane = lax.broadcasted_iota(jnp.int32, (GROUP * kb, KV_W), 1)
    nsub = tq // kb
    for s in range(nsub):
        pen_prev = jnp.where(meta_ref[1, i * nsub + s] == 1, NEG_BIG, 0.0)
        pen_next = jnp.where(meta_ref[2, i * nsub + s] == 1, NEG_BIG, 0.0)
        bias_prev = jnp.concatenate([band_prev + pen_prev] * GROUP, axis=0)
        bias_next = jnp.concatenate([band_next + pen_next] * GROUP, axis=0)
        kw = kcat_ref[s * kb:(s + 3) * kb, :]
        kc = kw[:, 0:KV_W]
        vc = jnp.concatenate([kw[:, KV_W:2 * KV_W], jnp.ones((3 * kb, KV_W), BF16)], axis=1)
        q4 = jnp.concatenate([q_ref[s * kb:(s + 1) * kb, g * KV_W:(g + 1) * KV_W] for g in range(GROUP)],
                             axis=0)
        res = []
        for kh in range(N_KV_HEADS):
            own = (klane >= kh * HEAD_DIM) & (klane < (kh + 1) * HEAD_DIM)
            kk = jnp.where(own, kc, jnp.zeros_like(kc))
            sc = lax.dot_general(q4, kk, (((1,), (1,)), ((), ())), preferred_element_type=F32)
            sc = jnp.concatenate([sc[:, 0:kb] + bias_prev, sc[:, kb:2 * kb], sc[:, 2 * kb:3 * kb] + bias_next],
                                 axis=1)
            snk = jnp.concatenate([jnp.full((kb, 1), sink_ref[kh * GROUP + g], F32) for g in range(GROUP)],
                                  axis=0)
            m = jnp.maximum(jnp.max(sc, axis=-1, keepdims=True), snk)
            p = jnp.exp(sc - m).astype(BF16)
            pv = jnp.dot(p, vc, preferred_element_type=F32)
            denom = pv[:, KV_W:KV_W + 1] + jnp.exp(snk - m)
            res.append(pv[:, 0:KV_W] / denom)
        out = res[0]
        for kh in range(1, N_KV_HEADS):
            out = jnp.where(olane >= kh * HEAD_DIM, res[kh], out)
        out = out.astype(BF16)
        for g in range(GROUP):
            o_ref[s * kb:(s + 1) * kb, g * KV_W:(g + 1) * KV_W] = out[g * kb:(g + 1) * kb, :]


def _attn_call(q, kv, sink, groups):
    n = q.shape[0]
    tq, kb = TQ_ATTN, KEY_BLOCK
    per = tq // kb
    nkb = n // kb
    meta = _tile_meta(groups, kb)
    gs = pltpu.PrefetchScalarGridSpec(
        num_scalar_prefetch=1, grid=(n // tq,),
        in_specs=[pl.BlockSpec(memory_space=pltpu.SMEM),
                  pl.BlockSpec((tq, ATTN_W), lambda i, m: (i, 0)),
                  pl.BlockSpec((kb, 2 * KV_W), lambda i, m: (jnp.maximum(i * per - 1, 0), 0)),
                  pl.BlockSpec((tq, 2 * KV_W), lambda i, m: (i, 0)),
                  pl.BlockSpec((kb, 2 * KV_W), lambda i, m: (jnp.minimum((i + 1) * per, nkb - 1), 0))],
        out_specs=pl.BlockSpec((tq, ATTN_W), lambda i, m: (i, 0)),
        scratch_shapes=[pltpu.VMEM((tq + 2 * kb, 2 * KV_W), BF16)])
    return pl.pallas_call(
        _attn_kernel, grid_spec=gs, name="window_attn",
        out_shape=jax.ShapeDtypeStruct((n, ATTN_W), BF16),
        compiler_params=_params())(meta, sink, q, kv, kv, kv)


def _mixer_kernel(na, meta_ref, xa_ref, xpa_ref, xna_ref, xb_ref, xpb_ref, xnb_ref, o_ref, g_ref, w_ref,
                  cw_ref, wa_ref, wc_ref, wm_ref, h_ref):
    i = pl.program_id(0)
    tm = xa_ref.shape[0]
    g = g_ref[...]
    in_a = i < na
    x = jnp.where(in_a, xa_ref[...], xb_ref[...])
    xh = jnp.concatenate([jnp.where(in_a, xpa_ref[...], xpb_ref[...]),
                          jnp.where(in_a, xna_ref[...], xnb_ref[...])], axis=0)
    xb = jnp.concatenate([_rms(x, g).astype(BF16), _rms(xh, g).astype(BF16)], axis=0)
    proj_all = jnp.dot(xb, w_ref[...], preferred_element_type=F32)
    proj = proj_all[0:tm, :]
    cb = proj[:, 0:CONV_W]
    u = proj[:, CONV_W:2 * CONV_W] * proj[:, 2 * CONV_W:3 * CONV_W]
    uh = proj_all[tm:tm + 2 * SUBLANES, CONV_W:2 * CONV_W] * proj_all[tm:tm + 2 * SUBLANES, 2 * CONV_W:3 * CONV_W]
    first = meta_ref[1, i]
    last = meta_ref[2, i]
    u_prev = jnp.where(first == 1, 0.0, uh[SUBLANES - 1:SUBLANES, :])
    u_next = jnp.where(last == 1, 0.0, uh[SUBLANES:SUBLANES + 1, :])
    row = lax.broadcasted_iota(jnp.int32, (tm, 1), 0)
    up = jnp.where(row == 0, u_prev, pltpu.roll(u, 1, 0))
    dn = jnp.where(row == tm - 1, u_next, pltpu.roll(u, tm - 1, 0))
    cw = cw_ref[...]
    y = up * cw[0:1, :] + u * cw[1:2, :] + dn * cw[2:3, :]
    conv = (cb * y).astype(BF16)
    conv_br = jnp.dot(conv, wc_ref[...], preferred_element_type=F32)
    attn_br = jnp.dot(o_ref[...], wa_ref[...], preferred_element_type=F32)
    g0 = jax.nn.sigmoid(proj[:, 3 * CONV_W:3 * CONV_W + D_MODEL])
    g1 = jax.nn.sigmoid(proj[:, 3 * CONV_W + D_MODEL:REST_W])
    merged = (g0 * attn_br + g1 * conv_br).astype(BF16)
    h_ref[...] = x + jnp.dot(merged, wm_ref[...], preferred_element_type=F32)


def _mixer_call(xa, xb, o, g_mix, w_rest, conv_w, w_attn_br, w_conv_br, w_mix_out, groups):
    tm = TM_MIX
    per = tm // SUBLANES
    na, nb = xa.shape[0] // tm, xb.shape[0] // tm
    n = xa.shape[0] + xb.shape[0]
    meta = _tile_meta(groups, tm)

    def x_specs(tile_of, n_tiles):
        last8 = n_tiles * per - 1
        return [pl.BlockSpec((tm, D_MODEL), lambda i, m: (jnp.clip(tile_of(i), 0, n_tiles - 1), 0)),
                pl.BlockSpec((SUBLANES, D_MODEL), lambda i, m: (jnp.clip(tile_of(i) * per - 1, 0, last8), 0)),
                pl.BlockSpec((SUBLANES, D_MODEL), lambda i, m: (jnp.clip((tile_of(i) + 1) * per, 0, last8), 0))]

    gs = pltpu.PrefetchScalarGridSpec(
        num_scalar_prefetch=1, grid=(n // tm,),
        in_specs=x_specs(lambda i: i, na) + x_specs(lambda i: i - na, nb) + [
                  pl.BlockSpec((tm, ATTN_W), lambda i, m: (i, 0)),
                  _const_spec((1, D_MODEL)),
                  _const_spec((D_MODEL, REST_W)),
                  _const_spec((3, CONV_W)),
                  _const_spec((ATTN_W, D_MODEL)),
                  _const_spec((CONV_W, D_MODEL)),
                  _const_spec((D_MODEL, D_MODEL))],
        out_specs=pl.BlockSpec((tm, D_MODEL), lambda i, m: (i, 0)))
    return pl.pallas_call(
        functools.partial(_mixer_kernel, na), grid_spec=gs, name="mixer",
        out_shape=jax.ShapeDtypeStruct((n, D_MODEL), F32),
        compiler_params=_params())(meta, xa, xa, xa, xb, xb, xb, o, g_mix, w_rest, conv_w, w_attn_br,
                                   w_conv_br, w_mix_out)


def _memkv_kernel(na, mema_ref, memb_ref, g_ref, w_ref, kv_ref):
    mem = jnp.where(pl.program_id(0) < na, mema_ref[0], memb_ref[0])
    mn = _rms(mem, g_ref[...]).astype(BF16)
    kv_ref[0] = jnp.dot(mn, w_ref[...], preferred_element_type=F32).astype(BF16)


def _memkv_call(mem_a, mem_b, g_mem, w_xkv):
    na, nb = mem_a.shape[0], mem_b.shape[0]
    return pl.pallas_call(
        functools.partial(_memkv_kernel, na), grid=(na + nb,), name="mem_kv",
        in_specs=[pl.BlockSpec((1, N_MEM, D_MODEL), lambda b: (jnp.minimum(b, na - 1), 0, 0)),
                  pl.BlockSpec((1, N_MEM, D_MODEL), lambda b: (jnp.maximum(b - na, 0), 0, 0)),
                  _const_spec((1, D_MODEL)),
                  _const_spec((D_MODEL, 2 * D_MODEL))],
        out_specs=pl.BlockSpec((1, N_MEM, 2 * D_MODEL), lambda b: (b, 0, 0)),
        out_shape=jax.ShapeDtypeStruct((na + nb, N_MEM, 2 * D_MODEL), BF16),
        compiler_params=_params())(mem_a, mem_b, g_mem, w_xkv)


def _cross_kernel(meta_ref, h_ref, kv_ref, gc_ref, wq_ref, wo_ref, gm_ref, wr_ref, br_ref,
                  h2_ref, xs_ref, idx_ref, tot_ref):
    del meta_ref
    tm = h_ref.shape[0]
    h = h_ref[...]
    hn = _rms(h, gc_ref[...]).astype(BF16)
    q = (jnp.dot(hn, wq_ref[...], preferred_element_type=F32) * (X_HEAD_DIM ** -0.5)).astype(BF16)
    outs = []
    for hd in range(X_HEADS):
        qh = q[:, hd * X_HEAD_DIM:(hd + 1) * X_HEAD_DIM]
        kh = kv_ref[0, :, hd * X_HEAD_DIM:(hd + 1) * X_HEAD_DIM]
        vh = kv_ref[0, :, D_MODEL + hd * X_HEAD_DIM:D_MODEL + (hd + 1) * X_HEAD_DIM]
        s = lax.dot_general(qh, kh, (((1,), (1,)), ((), ())), preferred_element_type=F32)
        m = jnp.max(s, axis=-1, keepdims=True)
        p = jnp.exp(s - m)
        p = (p / jnp.sum(p, axis=-1, keepdims=True)).astype(BF16)
        outs.append(jnp.dot(p, vh, preferred_element_type=F32).astype(BF16))
    o = jnp.concatenate(outs, axis=1)
    h2 = h + jnp.dot(o, wo_ref[...], preferred_element_type=F32)
    h2_ref[...] = h2

    hn3 = _rms(h2, gm_ref[...])
    hi = hn3.astype(BF16)
    hi32 = hi.astype(F32)
    lo = (hn3 - hi32).astype(BF16)
    xs_ref[:, 0:D_MODEL] = hn3

    nt = (((1,), (1,)), ((), ()))
    r1 = lax.dot_general(wr_ref[...], hi, nt, preferred_element_type=F32)
    r2 = lax.dot_general(wr_ref[...], lo, nt, preferred_element_type=F32)
    e = N_EXPERTS
    logits = ((r2[e:2 * e] + r2[0:e]) + r1[e:2 * e]) + r1[0:e] + br_ref[...]
    eio = lax.broadcasted_iota(jnp.int32, (e, tm), 0)
    cur = logits
    vals, sels = [], []
    for _ in range(TOP_K):
        mx = jnp.max(cur, axis=0, keepdims=True)
        sel = jnp.min(jnp.where(cur == mx, eio, e), axis=0, keepdims=True)
        vals.append(mx)
        sels.append(sel)
        cur = jnp.where(eio == sel, -jnp.inf, cur)
    ex = [jnp.exp(v - vals[0]) for v in vals]
    tot = ex[0] + ex[1] + ex[2] + ex[3]
    idx_ref[...] = jnp.concatenate(sels, axis=0)
    assign = jnp.where(eio == sels[0], 1.0, 0.0)
    for k in range(1, TOP_K):
        assign = assign + jnp.where(eio == sels[k], 1.0, 0.0)
    segs = jnp.zeros((e, 1), jnp.int32)
    for j in range(tm // TC_COMB):
        cnt = jnp.sum(assign[:, j * TC_COMB:(j + 1) * TC_COMB], axis=1, keepdims=True).astype(jnp.int32)
        segs = segs + lax.shift_left(lax.shift_right_logical(cnt + (SEG_ALIGN - 1), LOG2_SEG), LOG2_SEG)

    @pl.when(pl.program_id(0) == 0)
    def _():
        tot_ref[...] = jnp.zeros_like(tot_ref)
    tot_ref[...] += jnp.broadcast_to(segs, tot_ref.shape)
    gws = [x / tot for x in ex]
    w_hi = [g.astype(BF16).astype(F32) for g in gws]
    w_lo = [(g - h).astype(BF16).astype(F32) for g, h in zip(gws, w_hi)]
    wt = jnp.transpose(jnp.concatenate(w_hi + w_lo, axis=0))
    et = jnp.transpose(jnp.concatenate([s.astype(F32) for s in sels]
                                       + [jnp.zeros((SUBLANES - TOP_K, tm), F32)], axis=0))
    xs_ref[:, D_MODEL:XS_W] = jnp.concatenate(
        [wt, et, jnp.zeros((tm, LANES - 2 * SUBLANES), F32)], axis=1)


def _cross_call(h1, memkv, g_cross, w_xq, w_xo, g_moe, w_r2t, b_router, groups):
    n = h1.shape[0]
    tm = TM_CROSS
    meta = _tile_meta(groups, tm)
    row = lambda i, m: (i, 0)
    col = lambda i, m: (0, i)
    gs = pltpu.PrefetchScalarGridSpec(
        num_scalar_prefetch=1, grid=(n // tm,),
        in_specs=[pl.BlockSpec((tm, D_MODEL), row),
                  pl.BlockSpec((1, N_MEM, 2 * D_MODEL), lambda i, m: (m[3, i], 0, 0)),
                  _const_spec((1, D_MODEL)),
                  _const_spec((D_MODEL, D_MODEL)),
                  _const_spec((D_MODEL, D_MODEL)),
                  _const_spec((1, D_MODEL)),
                  _const_spec((2 * N_EXPERTS, D_MODEL)),
                  _const_spec((N_EXPERTS, 1))],
        out_specs=[pl.BlockSpec((tm, D_MODEL), row),
                   pl.BlockSpec((tm, XS_W), row),
                   pl.BlockSpec((TOP_K, tm), col),
                   pl.BlockSpec((N_EXPERTS, LANES), lambda i, m: (0, 0))])
    return pl.pallas_call(
        _cross_kernel, grid_spec=gs, name="cross_router",
        out_shape=[jax.ShapeDtypeStruct((n, D_MODEL), F32),
                   jax.ShapeDtypeStruct((n, XS_W), F32),
                   jax.ShapeDtypeStruct((TOP_K, n), jnp.int32),
                   jax.ShapeDtypeStruct((N_EXPERTS, LANES), jnp.int32)],
        compiler_params=_params())(meta, h1, memkv, g_cross, w_xq, w_xo, g_moe, w_r2t, b_router)


LOG2_BM = BM_MOE.bit_length() - 1
assert 1 << LOG2_BM == BM_MOE
SEG_ALIGN = 4
LOG2_SEG = SEG_ALIGN.bit_length() - 1
COPY_ROWS = tuple(SEG_ALIGN << j for j in (3, 2, 1, 0))
K_ROWS = 1152
M_CHUNK = 576
DESC_ROWS = 2 * SUBLANES
assert 1 << LOG2_SEG == SEG_ALIGN and K_ROWS % M_CHUNK == 0 and K_ROWS // COPY_ROWS[0] <= LANES
assert K_ROWS >= TOP_K * TC_COMB + N_EXPERTS * (SEG_ALIGN - 1)
DESC_COUNT = 2 * len(COPY_ROWS)
assert DESC_COUNT < DESC_ROWS and all(a == 2 * b for a, b in zip(COPY_ROWS, COPY_ROWS[1:]))
AUX_W_HI, AUX_W_LO, AUX_EID = 0, TOP_K, 2 * TOP_K
XS_W = D_MODEL + LANES


def _sublane_scan(x, n):
    row = lax.broadcasted_iota(jnp.int32, x.shape, 0)
    sft = 1
    while sft < n:
        x = x + jnp.where(row >= sft, pltpu.roll(x, sft, 0), 0)
        sft *= 2
    return x


def _owner(ends, lane):
    ebl = lax.broadcasted_iota(jnp.int32, ends.shape, 0)
    owner = jnp.sum(jnp.where(ends <= lane, 1.0, 0.0), axis=0, keepdims=True).astype(jnp.int32)
    return ebl == owner


def _pick(sel, val):
    return jnp.sum(jnp.where(sel, val.astype(F32), 0.0), axis=0, keepdims=True).astype(jnp.int32)


def _plan_kernel(idx_ref, tot_ref, kpos_ref, desc_ref, meta_ref, tri_ref, run_ref):
    i = pl.program_id(0)
    e, tp = N_EXPERTS, TC_COMB
    nb_pad = meta_ref.shape[1]
    eio = lax.broadcasted_iota(jnp.int32, (e, tp), 0)
    idx = idx_ref[...]
    onehots = [eio == idx[k:k + 1, :] for k in range(TOP_K)]
    s = jnp.where(onehots[0], 1.0, 0.0)
    for k in range(1, TOP_K):
        s = s + jnp.where(onehots[k], 1.0, 0.0)
    cnt_t = jnp.sum(s, axis=1, keepdims=True).astype(jnp.int32)
    cnt = jnp.broadcast_to(cnt_t, (e, LANES))
    seg = lax.shift_left(lax.shift_right_logical(cnt + (SEG_ALIGN - 1), LOG2_SEG), LOG2_SEG)

    @pl.when(i == 0)
    def _():
        r = lax.broadcasted_iota(jnp.int32, (tp, tp), 0)
        c = lax.broadcasted_iota(jnp.int32, (tp, tp), 1)
        tri_ref[...] = jnp.where(r < c, 1.0, 0.0).astype(BF16)
        tot = tot_ref[...]
        pad = lax.shift_left(lax.shift_right_logical(tot + (BM_MOE - 1), LOG2_BM), LOG2_BM)
        pad_end = _sublane_scan(pad, e)
        pad_start = pad_end - pad
        run_ref[...] = pad_start
        bpos = lax.broadcasted_iota(jnp.int32, (e, nb_pad), 1) * BM_MOE
        ebl = lax.broadcasted_iota(jnp.int32, (e, nb_pad), 0)
        done = jnp.where(pad_end[:, 0:1] <= bpos, 1.0, 0.0)
        bexp = jnp.minimum(jnp.sum(done, axis=0, keepdims=True).astype(jnp.int32), e - 1)
        row_end = (pad_start + tot)[:, 0:1].astype(F32)
        rend_b = jnp.sum(jnp.where(ebl == bexp, row_end, 0.0), axis=0, keepdims=True).astype(jnp.int32)
        nvalid = jnp.clip(rend_b - bpos[0:1, :], 0, BM_MOE)
        total = jnp.sum(jnp.where(ebl == e - 1, pad_end[:, 0:1].astype(F32), 0.0), axis=0, keepdims=True)
        nused = lax.shift_right_logical(total.astype(jnp.int32), LOG2_BM)
        meta_ref[...] = jnp.concatenate(
            [bexp, nvalid, nused, jnp.zeros((SUBLANES - 3, nb_pad), jnp.int32)], axis=0)

    before = jnp.dot(s.astype(BF16), tri_ref[...], preferred_element_type=F32)
    kbase = _sublane_scan(seg, e) - seg
    pos = kbase[:, 0:1].astype(F32) + before
    kp = [jnp.sum(jnp.where(onehots[k], pos, 0.0), axis=0, keepdims=True).astype(jnp.int32)
          for k in range(TOP_K)]
    kpos_ref[0] = jnp.concatenate(kp + [jnp.zeros((SUBLANES - TOP_K, tp), jnp.int32)], axis=0)

    start = run_ref[...]
    lane = lax.broadcasted_iota(jnp.int32, (e, LANES), 1)
    lane1 = lane[0:1, :]
    lists = []
    counts = jnp.zeros((1, LANES), jnp.int32)
    done = jnp.zeros_like(seg)
    for ci, rows in enumerate(COPY_ROWS):
        n = lax.shift_right_logical(seg - done, rows.bit_length() - 1)
        end = _sublane_scan(n, e)
        sel = _owner(end, lane)
        off = _pick(sel, done) + (lane1 - _pick(sel, end - n)) * rows
        n_c = end[e - 1:e, :]
        ok = lane1 < n_c
        lists += [jnp.where(ok, _pick(sel, start) + off, 0), jnp.where(ok, _pick(sel, kbase) + off, 0)]
        counts = jnp.where(lane1 == ci, n_c, counts)
        done = done + n * rows
    desc_ref[0] = jnp.concatenate(
        lists + [counts, jnp.zeros((DESC_ROWS - DESC_COUNT - 1, LANES), jnp.int32)], axis=0)
    run_ref[...] = run_ref[...] + seg


def _plan_call(idx, tot, n_blocks):
    n = idx.shape[1]
    tp = TC_COMB
    n_tiles = n // tp
    nb_pad = -(-n_blocks // LANES) * LANES
    tile = lambda i: (i, 0, 0)
    return pl.pallas_call(
        _plan_kernel, grid=(n_tiles,), name="route_plan",
        in_specs=[pl.BlockSpec((TOP_K, tp), lambda i: (0, i)), _const_spec((N_EXPERTS, LANES))],
        out_specs=[pl.BlockSpec((1, SUBLANES, tp), tile),
                   pl.BlockSpec((1, DESC_ROWS, LANES), tile),
                   pl.BlockSpec((SUBLANES, nb_pad), lambda i: (0, 0))],
        out_shape=[jax.ShapeDtypeStruct((n_tiles, SUBLANES, tp), jnp.int32),
                   jax.ShapeDtypeStruct((n_tiles, DESC_ROWS, LANES), jnp.int32),
                   jax.ShapeDtypeStruct((SUBLANES, nb_pad), jnp.int32)],
        scratch_shapes=[pltpu.VMEM((tp, tp), BF16), pltpu.VMEM((N_EXPERTS, LANES), jnp.int32)],
        compiler_params=_params())(idx, tot)


def _slot_hits(kp, iota):
    hit = jnp.where(iota == kp[0], 1.0, 0.0)
    for k in range(1, TOP_K):
        hit = hit + jnp.where(iota == kp[k], 1.0, 0.0)
    return hit


def _for_each_copy(count, fn):
    for ci, rows in enumerate(COPY_ROWS):
        def body(j, carry, ci=ci, rows=rows):
            fn(2 * ci, j, rows)
            return carry
        lax.fori_loop(0, count(ci), body, 0)


def _dispatch_kernel(n_tiles, n_blocks, nvalid_ref, desc_ref, kpos_ref, xs_ref, out_hbm, zero_ref, buf,
                     sems, zsem, issued_ref):
    i = pl.program_id(0)
    td = xs_ref.shape[0]
    bg = zero_ref.shape[0]
    cur = i % 2

    def copy(b, dst, src, rows):
        groups = rows // SEG_ALIGN
        return pltpu.make_async_copy(buf.at[b, pl.ds(lax.shift_right_logical(src, LOG2_SEG), groups)],
                                     out_hbm.at[pl.ds(lax.shift_right_logical(dst, LOG2_SEG), groups)], sems.at[b])

    def drain(b):
        _for_each_copy(lambda ci: issued_ref[b, ci], lambda row, j, rows: copy(b, 0, 0, rows).wait())

    @pl.when(i == 0)
    def _():
        zero_ref[...] = jnp.zeros_like(zero_ref)

        def fill(b):
            return pltpu.make_async_copy(zero_ref, out_hbm.at[pl.ds(b * bg, bg)], zsem)

        def start(b, carry):
            @pl.when(nvalid_ref[b] < bg * SEG_ALIGN)
            def _():
                fill(b).start()
            return carry

        def wait(b, carry):
            @pl.when(nvalid_ref[b] < bg * SEG_ALIGN)
            def _():
                fill(b).wait()
            return carry
        lax.fori_loop(0, n_blocks, start, 0)
        lax.fori_loop(0, n_blocks, wait, 0)

    @pl.when(i >= 2)
    def _():
        drain(cur)

    kp = kpos_ref[0]
    x = xs_ref[...].astype(BF16)
    for c in range(K_ROWS // M_CHUNK):
        rows = lax.broadcasted_iota(jnp.int32, (M_CHUNK, td), 0) + c * M_CHUNK
        sel = _slot_hits([kp[k:k + 1, :] for k in range(TOP_K)], rows).astype(BF16)
        mg = M_CHUNK // SEG_ALIGN
        buf[cur, c * mg:(c + 1) * mg] = jnp.dot(sel, x, preferred_element_type=F32).reshape(mg, SEG_ALIGN, -1)

    _for_each_copy(lambda ci: desc_ref[0, DESC_COUNT, ci],
                   lambda row, j, rows: copy(cur, desc_ref[0, row, j], desc_ref[0, row + 1, j], rows).start())
    for ci in range(len(COPY_ROWS)):
        issued_ref[cur, ci] = desc_ref[0, DESC_COUNT, ci]

    @pl.when(i == n_tiles - 1)
    def _():
        drain(cur)

        @pl.when(i >= 1)
        def _():
            drain(1 - cur)


def _dispatch_call(nvalid, desc, kpos, xs, n_blocks):
    n, width = xs.shape
    td, bm = TC_COMB, BM_MOE
    n_tiles = n // td
    gs = pltpu.PrefetchScalarGridSpec(
        num_scalar_prefetch=1, grid=(n_tiles,),
        in_specs=[pl.BlockSpec((1, DESC_ROWS, LANES), lambda i, nv: (i, 0, 0), memory_space=pltpu.SMEM),
                  pl.BlockSpec((1, SUBLANES, td), lambda i, nv: (i, 0, 0)),
                  pl.BlockSpec((td, width), lambda i, nv: (i, 0))],
        out_specs=pl.BlockSpec(memory_space=pl.ANY),
        scratch_shapes=[pltpu.VMEM((bm // SEG_ALIGN, SEG_ALIGN, width), xs.dtype),
                        pltpu.VMEM((2, K_ROWS // SEG_ALIGN, SEG_ALIGN, width), xs.dtype),
                        pltpu.SemaphoreType.DMA((2,)), pltpu.SemaphoreType.DMA(()),
                        pltpu.SMEM((2, len(COPY_ROWS)), jnp.int32)])
    return pl.pallas_call(
        functools.partial(_dispatch_kernel, n_tiles, n_blocks), grid_spec=gs, name="moe_dispatch",
        out_shape=jax.ShapeDtypeStruct((n_blocks * bm // SEG_ALIGN, SEG_ALIGN, width), xs.dtype),
        compiler_params=_params())(nvalid, desc, kpos, xs)


def _ffn_kernel(bexp_ref, nused_ref, x_ref, wg_hbm, bg_ref, wu_hbm, bu_ref, wd_hbm, bd_ref,
                y_ref, wf32, wbf, wsems, slot_ref):
    b = pl.program_id(0)
    nused = nused_ref[0]
    e = bexp_ref[b]
    wgb, wub, wdb = wbf.at[0], wbf.at[1], wbf.at[2]

    def fetch(expert, slot):
        return [pltpu.make_async_copy(src.at[expert], wf32.at[slot, j], wsems.at[slot])
                for j, src in enumerate((wg_hbm, wu_hbm, wd_hbm))]

    @pl.when(b == 0)
    def _():
        slot_ref[0] = 0
        for d in fetch(e, 0):
            d.start()

    @pl.when((b < nused) & ((b == 0) | (e != bexp_ref[jnp.maximum(b - 1, 0)])))
    def _():
        slot = slot_ref[0]
        for d in fetch(e, slot):
            d.wait()
        for j in range(3):
            wbf[j] = wf32[slot, j].astype(BF16)
        nxt = lax.while_loop(lambda j: (j < nused) & (bexp_ref[jnp.minimum(j, nused - 1)] == e),
                             lambda j: j + 1, b + 1)

        @pl.when(nxt < nused)
        def _():
            for d in fetch(bexp_ref[nxt], 1 - slot):
                d.start()
        slot_ref[0] = 1 - slot

    @pl.when(b < nused)
    def _():
        rows = x_ref[...].reshape(-1, XS_W)
        x = rows[:, 0:D_MODEL].astype(BF16)
        aux = rows[:, D_MODEL:XS_W]
        eid = bexp_ref[b].astype(F32)
        row_w = jnp.zeros((x.shape[0], 1), F32)
        for k in range(TOP_K):
            w_k = aux[:, AUX_W_HI + k:AUX_W_HI + k + 1] + aux[:, AUX_W_LO + k:AUX_W_LO + k + 1]
            row_w = row_w + jnp.where(aux[:, AUX_EID + k:AUX_EID + k + 1] == eid, w_k, 0.0)

        def proj(w_ref, b_ref):
            return jnp.dot(x, w_ref[...], preferred_element_type=F32) + b_ref[0]

        a = jnp.minimum(proj(wgb, bg_ref), SWIGLU_LIMIT)
        u = jnp.clip(proj(wub, bu_ref), -SWIGLU_LIMIT, SWIGLU_LIMIT)
        hid = (a * jax.nn.sigmoid(SWIGLU_ALPHA * a) * (u + 1.0)).astype(BF16)
        y = (jnp.dot(hid, wdb[...], preferred_element_type=F32) + bd_ref[0]) * row_w
        y_ref[...] = y.reshape(y_ref.shape)

    @pl.when(b >= nused)
    def _():
        y_ref[...] = jnp.zeros_like(y_ref)


def _ffn_call(bexp, nused, xs_sorted, w_gate, b_gate, w_up, b_up, w_down, b_down):
    bm = BM_MOE
    bg = bm // SEG_ALIGN
    n_blocks = xs_sorted.shape[0] // bg
    assert D_FF == D_MODEL, "the three expert matrices share one staging shape"
    bspec = lambda c: pl.BlockSpec((1, 1, c), lambda b, be, nu: (be[b], 0, 0))
    whole = pl.BlockSpec(memory_space=pl.ANY)
    gs = pltpu.PrefetchScalarGridSpec(
        num_scalar_prefetch=2, grid=(n_blocks,),
        in_specs=[pl.BlockSpec((bg, SEG_ALIGN, XS_W), lambda b, be, nu: (jnp.minimum(b, nu[0] - 1), 0, 0)),
                  whole, bspec(D_FF), whole, bspec(D_FF), whole, bspec(D_MODEL)],
        out_specs=pl.BlockSpec((bg, SEG_ALIGN, D_MODEL), lambda b, be, nu: (b, 0, 0)),
        scratch_shapes=[pltpu.VMEM((2, 3, D_MODEL, D_FF), F32), pltpu.VMEM((3, D_MODEL, D_FF), BF16),
                        pltpu.SemaphoreType.DMA((2,)), pltpu.SMEM((1,), jnp.int32)])
    return pl.pallas_call(
        _ffn_kernel, grid_spec=gs, name="expert_ffn",
        out_shape=jax.ShapeDtypeStruct((n_blocks * bg, SEG_ALIGN, D_MODEL), F32),
        compiler_params=_params())(bexp, nused, xs_sorted, w_gate, b_gate, w_up, b_up, w_down, b_down)


def _combine_kernel(nt, desc_ref, descn_ref, kpos_ref, y_hbm, h_ref, gf_ref, out_ref, ybuf, sems):
    i = pl.program_id(0)
    tc = h_ref.shape[0]
    cur = i % 2

    def copy(b, dst, src, rows):
        groups = rows // SEG_ALIGN
        return pltpu.make_async_copy(y_hbm.at[pl.ds(lax.shift_right_logical(dst, LOG2_SEG), groups)],
                                     ybuf.at[b, pl.ds(lax.shift_right_logical(src, LOG2_SEG), groups)], sems.at[b])

    def fetch(b, ref):
        _for_each_copy(lambda ci: ref[0, DESC_COUNT, ci],
                       lambda row, j, rows: copy(b, ref[0, row, j], ref[0, row + 1, j], rows).start())

    @pl.when(i == 0)
    def _():
        ybuf[...] = jnp.zeros_like(ybuf)
        fetch(0, desc_ref)

    @pl.when(i + 1 < nt)
    def _():
        fetch(1 - cur, descn_ref)

    _for_each_copy(lambda ci: desc_ref[0, DESC_COUNT, ci], lambda row, j, rows: copy(cur, 0, 0, rows).wait())

    kpt = jnp.transpose(kpos_ref[0].astype(F32))
    kp = [kpt[:, k:k + 1] for k in range(TOP_K)]
    acc = h_ref[...]
    for c in range(K_ROWS // M_CHUNK):
        cols = (lax.broadcasted_iota(jnp.int32, (tc, M_CHUNK), 1) + c * M_CHUNK).astype(F32)
        pick = _slot_hits(kp, cols).astype(BF16)
        mg = M_CHUNK // SEG_ALIGN
        y = ybuf[cur, c * mg:(c + 1) * mg].reshape(M_CHUNK, -1).astype(BF16)
        acc = acc + jnp.dot(pick, y, preferred_element_type=F32)
    out_ref[...] = _rms(acc, gf_ref[...])


def _combine_call(desc, kpos, y, h2, g_final, tile0, n_tiles):
    tc = TC_COMB
    last = tile0 + n_tiles - 1
    return pl.pallas_call(
        functools.partial(_combine_kernel, n_tiles), grid=(n_tiles,), name="moe_combine",
        in_specs=[pl.BlockSpec((1, DESC_ROWS, LANES), lambda i: (tile0 + i, 0, 0), memory_space=pltpu.SMEM),
                  pl.BlockSpec((1, DESC_ROWS, LANES), lambda i: (jnp.minimum(tile0 + i + 1, last), 0, 0),
                               memory_space=pltpu.SMEM),
                  pl.BlockSpec((1, SUBLANES, tc), lambda i: (tile0 + i, 0, 0)),
                  pl.BlockSpec(memory_space=pl.ANY),
                  pl.BlockSpec((tc, D_MODEL), lambda i: (tile0 + i, 0)),
                  _const_spec((1, D_MODEL))],
        out_specs=pl.BlockSpec((tc, D_MODEL), lambda i: (i, 0)),
        out_shape=jax.ShapeDtypeStruct((n_tiles * tc, D_MODEL), F32),
        scratch_shapes=[pltpu.VMEM((2, K_ROWS // SEG_ALIGN, SEG_ALIGN, D_MODEL), F32),
                        pltpu.SemaphoreType.DMA((2,))],
        compiler_params=_params())(desc, desc, kpos, y, h2, g_final)


def _forward(xs, mems, g_mix, w_in, sink, conv_w, w_attn_br, w_conv_br, w_mix_out, g_cross, g_mem,
             w_xq, w_xkv, w_xo, g_moe, w_router, b_router, w_gate, b_gate, w_up, b_up, w_down,
             b_down, g_final):
    groups = [(x.shape[0], x.shape[1]) for x in xs]
    for _, s in groups:
        assert s % max(TM_QKV, TQ_ATTN, TM_MIX, TM_CROSS, TC_COMB) == 0
    assert len(xs) == 2, "two request groups"
    xa, xb = (x.reshape(-1, D_MODEL) for x in xs)
    n = xa.shape[0] + xb.shape[0]
    tables = _rope_tables(max(s for _, s in groups))
    assert w_in.shape[0] == 1, "single-layer trunk: the final norm is fused into the combine kernel"
    l = 0
    row2 = lambda v: v.reshape(1, -1)
    w_in_b = w_in[l].astype(BF16)
    perm = np.array([(kv * GROUP + g) * HEAD_DIM + d
                     for g in range(GROUP) for kv in range(N_KV_HEADS) for d in range(HEAD_DIM)])
    w_qkv = jnp.concatenate([w_in_b[:, :ATTN_W][:, perm], w_in_b[:, ATTN_W:QKV_W]], axis=1)
    q, kv = _qkv_call(xa, xb, row2(g_mix[l]), w_qkv, tables, groups)
    o = _attn_call(q, kv, sink[l], groups)
    h = _mixer_call(xa, xb, o, row2(g_mix[l]), w_in_b[:, QKV_W:], conv_w[l],
                    w_attn_br[l].astype(BF16)[perm, :], w_conv_br[l].astype(BF16),
                    w_mix_out[l].astype(BF16), groups)
    memkv = _memkv_call(mems[0], mems[1], row2(g_mem[l]), w_xkv[l].astype(BF16))
    wr = w_router[l]
    wr_hi = wr.astype(BF16)
    wr_lo = (wr - wr_hi.astype(F32)).astype(BF16)
    w_r2t = jnp.concatenate([wr_hi.T, wr_lo.T], axis=0)
    h2, xs_rows, idx, tot = _cross_call(h, memkv, row2(g_cross[l]), w_xq[l].astype(BF16),
                                        w_xo[l].astype(BF16), row2(g_moe[l]), w_r2t,
                                        b_router[l].reshape(-1, 1), groups)
    seg_rows = TOP_K * n + (n // TC_COMB) * N_EXPERTS * (SEG_ALIGN - 1)
    n_blocks = -(-seg_rows // BM_MOE) + N_EXPERTS
    kpos, desc, meta = _plan_call(idx, tot, n_blocks)
    xs_sorted = _dispatch_call(meta[1], desc, kpos, xs_rows, n_blocks)
    y = _ffn_call(meta[0], meta[2, 0:1], xs_sorted, w_gate[l], b_gate[l][:, None, :],
                  w_up[l], b_up[l][:, None, :], w_down[l], b_down[l][:, None, :])
    tc = TC_COMB
    outs, t0 = [], 0
    for nb, s in groups:
        nt = nb * s // tc
        outs.append(_combine_call(desc, kpos, y, h2, row2(g_final), t0, nt))
        t0 += nt
    return tuple(o.reshape(x.shape) for o, x in zip(outs, xs))


def kernel(x_prompt, x_sample, mem_prompt, mem_sample, g_mix, w_in, sink, conv_w, w_attn_br, w_conv_br,
           w_mix_out, g_cross, g_mem, w_xq, w_xkv, w_xo, g_moe, w_router, b_router, w_gate, b_gate,
           w_up, b_up, w_down, b_down, g_final):
    return _forward([x_prompt, x_sample], [mem_prompt, mem_sample], g_mix, w_in, sink, conv_w,
                    w_attn_br, w_conv_br, w_mix_out, g_cross, g_mem, w_xq, w_xkv, w_xo, g_moe,
                    w_router, b_router, w_gate, b_gate, w_up, b_up, w_down, b_down, g_final)
```

```python
import functools

import numpy as np
import jax
import jax.numpy as jnp
from jax import lax
from jax.experimental import pallas as pl
from jax.experimental.pallas import tpu as pltpu

F32 = jnp.float32
BF16 = jnp.bfloat16

D_MODEL = 1024
N_HEADS = 8
N_KV_HEADS = 2
HEAD_DIM = 64
GROUP = N_HEADS // N_KV_HEADS
ATTN_W = N_HEADS * HEAD_DIM
KV_W = N_KV_HEADS * HEAD_DIM
QKV_W = ATTN_W + 2 * KV_W
WINDOW = 128
ROT_DIM = HEAD_DIM // 4
ROPE_THETA = 500000.0
CONV_W = D_MODEL // 2
REST_W = 3 * CONV_W + 2 * D_MODEL
N_MEM = 256
X_HEADS = 4
X_HEAD_DIM = D_MODEL // X_HEADS
N_EXPERTS = 32
TOP_K = 4
D_FF = D_MODEL
SWIGLU_ALPHA = 1.702
SWIGLU_LIMIT = 7.0
EPS = 1e-5

LANES = 128
SUBLANES = 8
KEY_BLOCK = WINDOW
TM_QKV = 1024
TQ_ATTN = 1024
TM_MIX = 512
TM_CROSS = 1024
BM_MOE = 512
TC_COMB = 256
NEG_BIG = -1e30
VMEM_LIMIT = 56 * 1024 * 1024


def _rms(x, g):
    var = jnp.mean(x * x, axis=-1, keepdims=True)
    return x * lax.rsqrt(var + EPS) * g


def _tile_meta(groups, tile):
    pos, first, last, bidx = [], [], [], []
    b0 = 0
    for nb, s in groups:
        per = s // tile
        for b in range(nb):
            for j in range(per):
                pos.append(j)
                first.append(int(j == 0))
                last.append(int(j == per - 1))
                bidx.append(b0 + b)
        b0 += nb
    return jnp.asarray(np.array([pos, first, last, bidx], dtype=np.int32))


def _const_spec(shape):
    nd = len(shape)
    return pl.BlockSpec(shape, lambda *_: (0,) * nd, pipeline_mode=pl.Buffered(1))


def _params(vmem=VMEM_LIMIT):
    return pltpu.CompilerParams(dimension_semantics=("arbitrary",), vmem_limit_bytes=vmem)


def _qkv_kernel(na, meta_ref, xa_ref, xb_ref, g_ref, w_ref, cos_ref, sa_ref, sb_ref, q_ref, kv_ref):
    del meta_ref
    x = jnp.where(pl.program_id(0) < na, xa_ref[...], xb_ref[...])
    xn = _rms(x, g_ref[...]).astype(BF16)
    proj = jnp.dot(xn, w_ref[...], preferred_element_type=F32)
    c, sa, sb = cos_ref[...], sa_ref[...], sb_ref[...]
    n_rot = (ATTN_W + KV_W) // LANES
    for gi in range(n_rot):
        p = proj[:, gi * LANES:(gi + 1) * LANES]
        r = p * c + pltpu.roll(p, LANES - ROT_DIM // 2, 1) * sa + pltpu.roll(p, ROT_DIM // 2, 1) * sb
        if gi < ATTN_W // LANES:
            q_ref[:, gi * LANES:(gi + 1) * LANES] = (r * (HEAD_DIM ** -0.5)).astype(BF16)
        else:
            kv_ref[:, 0:KV_W] = r.astype(BF16)
    kv_ref[:, KV_W:2 * KV_W] = proj[:, ATTN_W + KV_W:QKV_W].astype(BF16)


def _rope_tables(s_max):
    half = ROT_DIM // 2
    inv_freq = ROPE_THETA ** (-(jnp.arange(half, dtype=F32) * 2.0) / ROT_DIM)
    ang = jnp.arange(s_max, dtype=F32)[:, None] * inv_freq[None, :]
    cos, sin = jnp.cos(ang), jnp.sin(ang)
    d = np.arange(LANES) % HEAD_DIM
    j = d % half
    cos_l, sin_l = cos[:, j], sin[:, j]
    rot = jnp.asarray(d < ROT_DIM)[None, :]
    lo = jnp.asarray(d < half)[None, :]
    hi = jnp.asarray((d >= half) & (d < ROT_DIM))[None, :]
    c = jnp.where(rot, cos_l, 1.0)
    sa = jnp.where(lo, -sin_l, 0.0)
    sb = jnp.where(hi, sin_l, 0.0)
    return c, sa, sb


def _qkv_call(xa, xb, g_mix, w_qkv, tables, groups):
    tm = TM_QKV
    na = xa.shape[0] // tm
    n = xa.shape[0] + xb.shape[0]
    meta = _tile_meta(groups, tm)
    row = lambda i, m: (i, 0)
    tab = lambda i, m: (m[0, i], 0)
    gs = pltpu.PrefetchScalarGridSpec(
        num_scalar_prefetch=1, grid=(n // tm,),
        in_specs=[pl.BlockSpec((tm, D_MODEL), lambda i, m: (jnp.minimum(i, na - 1), 0)),
                  pl.BlockSpec((tm, D_MODEL), lambda i, m: (jnp.maximum(i - na, 0), 0)),
                  _const_spec((1, D_MODEL)),
                  _const_spec((D_MODEL, QKV_W)),
                  pl.BlockSpec((tm, LANES), tab), pl.BlockSpec((tm, LANES), tab),
                  pl.BlockSpec((tm, LANES), tab)],
        out_specs=[pl.BlockSpec((tm, ATTN_W), row), pl.BlockSpec((tm, 2 * KV_W), row)])
    return pl.pallas_call(
        functools.partial(_qkv_kernel, na), grid_spec=gs, name="qkv_rope",
        out_shape=[jax.ShapeDtypeStruct((n, ATTN_W), BF16), jax.ShapeDtypeStruct((n, 2 * KV_W), BF16)],
        compiler_params=_params())(meta, xa, xb, g_mix, w_qkv, *tables)


def _attn_kernel(meta_ref, sink_ref, q_ref, kvp_ref, kvm_ref, kvn_ref, o_ref, kcat_ref):
    i = pl.program_id(0)
    kb = KEY_BLOCK
    tq = q_ref.shape[0]
    kcat_ref[0:kb, :] = kvp_ref[...]
    kcat_ref[kb:kb + tq, :] = kvm_ref[...]
    kcat_ref[kb + tq:kb + tq + kb, :] = kvn_ref[...]
    r = lax.broadcasted_iota(jnp.int32, (kb, kb), 0)
    c = lax.broadcasted_iota(jnp.int32, (kb, kb), 1)
    band_prev = jnp.where(c >= r, 0.0, NEG_BIG)
    band_next = jnp.where(c <= r, 0.0, NEG_BIG)
    klane = lax.broadcasted_iota(jnp.int32, (3 * kb, KV_W), 1)
    olane = lax.broadcasted_iota(jnp.int32, (GROUP * kb, KV_W), 1)
    nsub = tq // kb
    for s in range(nsub):
        pen_prev = jnp.where(meta_ref[1, i * nsub + s] == 1, NEG_BIG, 0.0)
        pen_next = jnp.where(meta_ref[2, i * nsub + s] == 1, NEG_BIG, 0.0)
        bias_prev = jnp.concatenate([band_prev + pen_prev] * GROUP, axis=0)
        bias_next = jnp.concatenate([band_next + pen_next] * GROUP, axis=0)
        kw = kcat_ref[s * kb:(s + 3) * kb, :]
        kc = kw[:, 0:KV_W]
        vc = jnp.concatenate([kw[:, KV_W:2 * KV_W], jnp.ones((3 * kb, KV_W), BF16)], axis=1)
        q4 = jnp.concatenate([q_ref[s * kb:(s + 1) * kb, g * KV_W:(g + 1) * KV_W] for g in range(GROUP)],
                             axis=0)
        res = []
        for kh in range(N_KV_HEADS):
            own = (klane >= kh * HEAD_DIM) & (klane < (kh + 1) * HEAD_DIM)
            kk = jnp.where(own, kc, jnp.zeros_like(kc))
            sc = lax.dot_general(q4, kk, (((1,), (1,)), ((), ())), preferred_element_type=F32)
            sc = jnp.concatenate([sc[:, 0:kb] + bias_prev, sc[:, kb:2 * kb], sc[:, 2 * kb:3 * kb] + bias_next],
                                 axis=1)
            snk = jnp.concatenate([jnp.full((kb, 1), sink_ref[kh * GROUP + g], F32) for g in range(GROUP)],
                                  axis=0)
            m = jnp.maximum(jnp.max(sc, axis=-1, keepdims=True), snk)
            p = jnp.exp(sc - m).astype(BF16)
            pv = jnp.dot(p, vc, preferred_element_type=F32)
            denom = pv[:, KV_W:KV_W + 1] + jnp.exp(snk - m)
            res.append(pv[:, 0:KV_W] / denom)
        out = res[0]
        for kh in range(1, N_KV_HEADS):
            out = jnp.where(olane >= kh * HEAD_DIM, res[kh], out)
        out = out.astype(BF16)
        for g in range(GROUP):
            o_ref[s * kb:(s + 1) * kb, g * KV_W:(g + 1) * KV_W] = out[g * kb:(g + 1) * kb, :]


def _attn_call(q, kv, sink, groups):
    n = q.shape[0]
    tq, kb = TQ_ATTN, KEY_BLOCK
    per = tq // kb
    nkb = n // kb
    meta = _tile_meta(groups, kb)
    gs = pltpu.PrefetchScalarGridSpec(
        num_scalar_prefetch=1, grid=(n // tq,),
        in_specs=[pl.BlockSpec(memory_space=pltpu.SMEM),
                  pl.BlockSpec((tq, ATTN_W), lambda i, m: (i, 0)),
                  pl.BlockSpec((kb, 2 * KV_W), lambda i, m: (jnp.maximum(i * per - 1, 0), 0)),
                  pl.BlockSpec((tq, 2 * KV_W), lambda i, m: (i, 0)),
                  pl.BlockSpec((kb, 2 * KV_W), lambda i, m: (jnp.minimum((i + 1) * per, nkb - 1), 0))],
        out_specs=pl.BlockSpec((tq, ATTN_W), lambda i, m: (i, 0)),
        scratch_shapes=[pltpu.VMEM((tq + 2 * kb, 2 * KV_W), BF16)])
    return pl.pallas_call(
        _attn_kernel, grid_spec=gs, name="window_attn",
        out_shape=jax.ShapeDtypeStruct((n, ATTN_W), BF16),
        compiler_params=_params())(meta, sink, q, kv, kv, kv)


def _mixer_kernel(na, meta_ref, xa_ref, xpa_ref, xna_ref, xb_ref, xpb_ref, xnb_ref, o_ref, g_ref, w_ref,
                  cw_ref, wa_ref, wc_ref, wm_ref, h_ref):
    i = pl.program_id(0)
    tm = xa_ref.shape[0]
    g = g_ref[...]
    in_a = i < na
    x = jnp.where(in_a, xa_ref[...], xb_ref[...])
    xh = jnp.concatenate([jnp.where(in_a, xpa_ref[...], xpb_ref[...]),
                          jnp.where(in_a, xna_ref[...], xnb_ref[...])], axis=0)
    xb = jnp.concatenate([_rms(x, g).astype(BF16), _rms(xh, g).astype(BF16)], axis=0)
    proj_all = jnp.dot(xb, w_ref[...], preferred_element_type=F32)
    proj = proj_all[0:tm, :]
    cb = proj[:, 0:CONV_W]
    u = proj[:, CONV_W:2 * CONV_W] * proj[:, 2 * CONV_W:3 * CONV_W]
    uh = proj_all[tm:tm + 2 * SUBLANES, CONV_W:2 * CONV_W] * proj_all[tm:tm + 2 * SUBLANES, 2 * CONV_W:3 * CONV_W]
    first = meta_ref[1, i]
    last = meta_ref[2, i]
    u_prev = jnp.where(first == 1, 0.0, uh[SUBLANES - 1:SUBLANES, :])
    u_next = jnp.where(last == 1, 0.0, uh[SUBLANES:SUBLANES + 1, :])
    row = lax.broadcasted_iota(jnp.int32, (tm, 1), 0)
    up = jnp.where(row == 0, u_prev, pltpu.roll(u, 1, 0))
    dn = jnp.where(row == tm - 1, u_next, pltpu.roll(u, tm - 1, 0))
    cw = cw_ref[...]
    y = up * cw[0:1, :] + u * cw[1:2, :] + dn * cw[2:3, :]
    conv = (cb * y).astype(BF16)
    conv_br = jnp.dot(conv, wc_ref[...], preferred_element_type=F32)
    attn_br = jnp.dot(o_ref[...], wa_ref[...], preferred_element_type=F32)
    g0 = jax.nn.sigmoid(proj[:, 3 * CONV_W:3 * CONV_W + D_MODEL])
    g1 = jax.nn.sigmoid(proj[:, 3 * CONV_W + D_MODEL:REST_W])
    merged = (g0 * attn_br + g1 * conv_br).astype(BF16)
    h_ref[...] = x + jnp.dot(merged, wm_ref[...], preferred_element_type=F32)


def _mixer_call(xa, xb, o, g_mix, w_rest, conv_w, w_attn_br, w_conv_br, w_mix_out, groups):
    tm = TM_MIX
    per = tm // SUBLANES
    na, nb = xa.shape[0] // tm, xb.shape[0] // tm
    n = xa.shape[0] + xb.shape[0]
    meta = _tile_meta(groups, tm)

    def x_specs(tile_of, n_tiles):
        last8 = n_tiles * per - 1
        return [pl.BlockSpec((tm, D_MODEL), lambda i, m: (jnp.clip(tile_of(i), 0, n_tiles - 1), 0)),
                pl.BlockSpec((SUBLANES, D_MODEL), lambda i, m: (jnp.clip(tile_of(i) * per - 1, 0, last8), 0)),
                pl.BlockSpec((SUBLANES, D_MODEL), lambda i, m: (jnp.clip((tile_of(i) + 1) * per, 0, last8), 0))]

    gs = pltpu.PrefetchScalarGridSpec(
        num_scalar_prefetch=1, grid=(n // tm,),
        in_specs=x_specs(lambda i: i, na) + x_specs(lambda i: i - na, nb) + [
                  pl.BlockSpec((tm, ATTN_W), lambda i, m: (i, 0)),
                  _const_spec((1, D_MODEL)),
                  _const_spec((D_MODEL, REST_W)),
                  _const_spec((3, CONV_W)),
                  _const_spec((ATTN_W, D_MODEL)),
                  _const_spec((CONV_W, D_MODEL)),
                  _const_spec((D_MODEL, D_MODEL))],
        out_specs=pl.BlockSpec((tm, D_MODEL), lambda i, m: (i, 0)))
    return pl.pallas_call(
        functools.partial(_mixer_kernel, na), grid_spec=gs, name="mixer",
        out_shape=jax.ShapeDtypeStruct((n, D_MODEL), F32),
        compiler_params=_params())(meta, xa, xa, xa, xb, xb, xb, o, g_mix, w_rest, conv_w, w_attn_br,
                                   w_conv_br, w_mix_out)


def _memkv_kernel(na, mema_ref, memb_ref, g_ref, w_ref, kv_ref):
    mem = jnp.where(pl.program_id(0) < na, mema_ref[0], memb_ref[0])
    mn = _rms(mem, g_ref[...]).astype(BF16)
    kv_ref[0] = jnp.dot(mn, w_ref[...], preferred_element_type=F32).astype(BF16)


def _memkv_call(mem_a, mem_b, g_mem, w_xkv):
    na, nb = mem_a.shape[0], mem_b.shape[0]
    return pl.pallas_call(
        functools.partial(_memkv_kernel, na), grid=(na + nb,), name="mem_kv",
        in_specs=[pl.BlockSpec((1, N_MEM, D_MODEL), lambda b: (jnp.minimum(b, na - 1), 0, 0)),
                  pl.BlockSpec((1, N_MEM, D_MODEL), lambda b: (jnp.maximum(b - na, 0), 0, 0)),
                  _const_spec((1, D_MODEL)),
                  _const_spec((D_MODEL, 2 * D_MODEL))],
        out_specs=pl.BlockSpec((1, N_MEM, 2 * D_MODEL), lambda b: (b, 0, 0)),
        out_shape=jax.ShapeDtypeStruct((na + nb, N_MEM, 2 * D_MODEL), BF16),
        compiler_params=_params())(mem_a, mem_b, g_mem, w_xkv)


def _cross_kernel(meta_ref, h_ref, kv_ref, gc_ref, wq_ref, wo_ref, gm_ref, wr_ref, br_ref,
                  h2_ref, xs_ref, idx_ref, tot_ref):
    del meta_ref
    tm = h_ref.shape[0]
    h = h_ref[...]
    hn = _rms(h, gc_ref[...]).astype(BF16)
    q = (jnp.dot(hn, wq_ref[...], preferred_element_type=F32) * (X_HEAD_DIM ** -0.5)).astype(BF16)
    outs = []
    for hd in range(X_HEADS):
        qh = q[:, hd * X_HEAD_DIM:(hd + 1) * X_HEAD_DIM]
        kh = kv_ref[0, :, hd * X_HEAD_DIM:(hd + 1) * X_HEAD_DIM]
        vh = kv_ref[0, :, D_MODEL + hd * X_HEAD_DIM:D_MODEL + (hd + 1) * X_HEAD_DIM]
        s = lax.dot_general(qh, kh, (((1,), (1,)), ((), ())), preferred_element_type=F32)
        m = jnp.max(s, axis=-1, keepdims=True)
        p = jnp.exp(s - m)
        p = (p / jnp.sum(p, axis=-1, keepdims=True)).astype(BF16)
        outs.append(jnp.dot(p, vh, preferred_element_type=F32).astype(BF16))
    o = jnp.concatenate(outs, axis=1)
    h2 = h + jnp.dot(o, wo_ref[...], preferred_element_type=F32)
    h2_ref[...] = h2

    hn3 = _rms(h2, gm_ref[...])
    hi = hn3.astype(BF16)
    hi32 = hi.astype(F32)
    lo = (hn3 - hi32).astype(BF16)
    xs_ref[:, 0:D_MODEL] = hn3

    nt = (((1,), (1,)), ((), ()))
    r1 = lax.dot_general(wr_ref[...], hi, nt, preferred_element_type=F32)
    r2 = lax.dot_general(wr_ref[...], lo, nt, preferred_element_type=F32)
    e = N_EXPERTS
    logits = ((r2[e:2 * e] + r2[0:e]) + r1[e:2 * e]) + r1[0:e] + br_ref[...]
    eio = lax.broadcasted_iota(jnp.int32, (e, tm), 0)
    cur = logits
    vals, sels = [], []
    for _ in range(TOP_K):
        mx = jnp.max(cur, axis=0, keepdims=True)
        sel = jnp.min(jnp.where(cur == mx, eio, e), axis=0, keepdims=True)
        vals.append(mx)
        sels.append(sel)
        cur = jnp.where(eio == sel, -jnp.inf, cur)
    ex = [jnp.exp(v - vals[0]) for v in vals]
    tot = ex[0] + ex[1] + ex[2] + ex[3]
    idx_ref[...] = jnp.concatenate(sels, axis=0)
    assign = jnp.where(eio == sels[0], 1.0, 0.0)
    for k in range(1, TOP_K):
        assign = assign + jnp.where(eio == sels[k], 1.0, 0.0)
    segs = jnp.zeros((e, 1), jnp.int32)
    for j in range(tm // TC_COMB):
        cnt = jnp.sum(assign[:, j * TC_COMB:(j + 1) * TC_COMB], axis=1, keepdims=True).astype(jnp.int32)
        segs = segs + lax.shift_left(lax.shift_right_logical(cnt + (SEG_ALIGN - 1), LOG2_SEG), LOG2_SEG)

    @pl.when(pl.program_id(0) == 0)
    def _():
        tot_ref[...] = jnp.zeros_like(tot_ref)
    tot_ref[...] += jnp.broadcast_to(segs, tot_ref.shape)
    gws = [x / tot for x in ex]
    w_hi = [g.astype(BF16).astype(F32) for g in gws]
    w_lo = [(g - h).astype(BF16).astype(F32) for g, h in zip(gws, w_hi)]
    wt = jnp.transpose(jnp.concatenate(w_hi + w_lo, axis=0))
    et = jnp.transpose(jnp.concatenate([s.astype(F32) for s in sels]
                                       + [jnp.zeros((SUBLANES - TOP_K, tm), F32)], axis=0))
    xs_ref[:, D_MODEL:XS_W] = jnp.concatenate(
        [wt, et, jnp.zeros((tm, LANES - 2 * SUBLANES), F32)], axis=1)


def _cross_call(h1, memkv, g_cross, w_xq, w_xo, g_moe, w_r2t, b_router, groups):
    n = h1.shape[0]
    tm = TM_CROSS
    meta = _tile_meta(groups, tm)
    row = lambda i, m: (i, 0)
    col = lambda i, m: (0, i)
    gs = pltpu.PrefetchScalarGridSpec(
        num_scalar_prefetch=1, grid=(n // tm,),
        in_specs=[pl.BlockSpec((tm, D_MODEL), row),
                  pl.BlockSpec((1, N_MEM, 2 * D_MODEL), lambda i, m: (m[3, i], 0, 0)),
                  _const_spec((1, D_MODEL)),
                  _const_spec((D_MODEL, D_MODEL)),
                  _const_spec((D_MODEL, D_MODEL)),
                  _const_spec((1, D_MODEL)),
                  _const_spec((2 * N_EXPERTS, D_MODEL)),
                  _const_spec((N_EXPERTS, 1))],
        out_specs=[pl.BlockSpec((tm, D_MODEL), row),
                   pl.BlockSpec((tm, XS_W), row),
                   pl.BlockSpec((TOP_K, tm), col),
                   pl.BlockSpec((N_EXPERTS, LANES), lambda i, m: (0, 0))])
    return pl.pallas_call(
        _cross_kernel, grid_spec=gs, name="cross_router",
        out_shape=[jax.ShapeDtypeStruct((n, D_MODEL), F32),
                   jax.ShapeDtypeStruct((n, XS_W), F32),
                   jax.ShapeDtypeStruct((TOP_K, n), jnp.int32),
                   jax.ShapeDtypeStruct((N_EXPERTS, LANES), jnp.int32)],
        compiler_params=_params())(meta, h1, memkv, g_cross, w_xq, w_xo, g_moe, w_r2t, b_router)


LOG2_BM = BM_MOE.bit_length() - 1
assert 1 << LOG2_BM == BM_MOE
SEG_ALIGN = 4
LOG2_SEG = SEG_ALIGN.bit_length() - 1
COPY_ROWS = tuple(SEG_ALIGN << j for j in (3, 2, 1, 0))
K_ROWS = 1152
M_CHUNK = 576
DESC_ROWS = 2 * SUBLANES
PLAN_TILES = 4
assert 1 << LOG2_SEG == SEG_ALIGN and K_ROWS % M_CHUNK == 0 and K_ROWS // COPY_ROWS[0] <= LANES
assert K_ROWS >= TOP_K * TC_COMB + N_EXPERTS * (SEG_ALIGN - 1)
DESC_COUNT = 2 * len(COPY_ROWS)
assert DESC_COUNT < DESC_ROWS and all(a == 2 * b for a, b in zip(COPY_ROWS, COPY_ROWS[1:]))
AUX_W_HI, AUX_W_LO, AUX_EID = 0, TOP_K, 2 * TOP_K
XS_W = D_MODEL + LANES


def _sublane_scan(x, n):
    row = lax.broadcasted_iota(jnp.int32, x.shape, 0)
    sft = 1
    while sft < n:
        x = x + jnp.where(row >= sft, pltpu.roll(x, sft, 0), 0)
        sft *= 2
    return x


def _owner(ends, lane):
    ebl = lax.broadcasted_iota(jnp.int32, ends.shape, 0)
    owner = jnp.sum(jnp.where(ends <= lane, 1.0, 0.0), axis=0, keepdims=True).astype(jnp.int32)
    return ebl == owner


def _pick(sel, val):
    return jnp.sum(jnp.where(sel, val.astype(F32), 0.0), axis=0, keepdims=True).astype(jnp.int32)


def _plan_kernel(idx_ref, tot_ref, kpos_ref, desc_ref, meta_ref, tri_ref, run_ref):
    i = pl.program_id(0)
    e, tp = N_EXPERTS, TC_COMB
    nb_pad = meta_ref.shape[1]

    @pl.when(i == 0)
    def _():
        r = lax.broadcasted_iota(jnp.int32, (tp, tp), 0)
        c = lax.broadcasted_iota(jnp.int32, (tp, tp), 1)
        tri_ref[...] = jnp.where(r < c, 1.0, 0.0).astype(BF16)
        tot = tot_ref[...]
        pad = lax.shift_left(lax.shift_right_logical(tot + (BM_MOE - 1), LOG2_BM), LOG2_BM)
        pad_end = _sublane_scan(pad, e)
        pad_start = pad_end - pad
        run_ref[...] = pad_start
        bpos = lax.broadcasted_iota(jnp.int32, (e, nb_pad), 1) * BM_MOE
        ebl = lax.broadcasted_iota(jnp.int32, (e, nb_pad), 0)
        done = jnp.where(pad_end[:, 0:1] <= bpos, 1.0, 0.0)
        bexp = jnp.minimum(jnp.sum(done, axis=0, keepdims=True).astype(jnp.int32), e - 1)
        row_end = (pad_start + tot)[:, 0:1].astype(F32)
        rend_b = jnp.sum(jnp.where(ebl == bexp, row_end, 0.0), axis=0, keepdims=True).astype(jnp.int32)
        nvalid = jnp.clip(rend_b - bpos[0:1, :], 0, BM_MOE)
        total = jnp.sum(jnp.where(ebl == e - 1, pad_end[:, 0:1].astype(F32), 0.0), axis=0, keepdims=True)
        nused = lax.shift_right_logical(total.astype(jnp.int32), LOG2_BM)
        meta_ref[...] = jnp.concatenate(
            [bexp, nvalid, nused, jnp.zeros((SUBLANES - 3, nb_pad), jnp.int32)], axis=0)

    for j in range(PLAN_TILES):
        _plan_tile(j, idx_ref[:, j * tp:(j + 1) * tp], kpos_ref, desc_ref, tri_ref, run_ref)


def _plan_tile(j, idx, kpos_ref, desc_ref, tri_ref, run_ref):
    e, tp = N_EXPERTS, TC_COMB
    eio = lax.broadcasted_iota(jnp.int32, (e, tp), 0)
    onehots = [eio == idx[k:k + 1, :] for k in range(TOP_K)]
    s = jnp.where(onehots[0], 1.0, 0.0)
    for k in range(1, TOP_K):
        s = s + jnp.where(onehots[k], 1.0, 0.0)
    cnt_t = jnp.sum(s, axis=1, keepdims=True).astype(jnp.int32)
    cnt = jnp.broadcast_to(cnt_t, (e, LANES))
    seg = lax.shift_left(lax.shift_right_logical(cnt + (SEG_ALIGN - 1), LOG2_SEG), LOG2_SEG)
    before = jnp.dot(s.astype(BF16), tri_ref[...], preferred_element_type=F32)
    kbase = _sublane_scan(seg, e) - seg
    pos = kbase[:, 0:1].astype(F32) + before
    kp = [jnp.sum(jnp.where(onehots[k], pos, 0.0), axis=0, keepdims=True).astype(jnp.int32)
          for k in range(TOP_K)]
    kpos_ref[j] = jnp.concatenate(kp + [jnp.zeros((SUBLANES - TOP_K, tp), jnp.int32)], axis=0)

    start = run_ref[...]
    lane = lax.broadcasted_iota(jnp.int32, (e, LANES), 1)
    lane1 = lane[0:1, :]
    lists = []
    counts = jnp.zeros((1, LANES), jnp.int32)
    done = jnp.zeros_like(seg)
    for ci, rows in enumerate(COPY_ROWS):
        n = lax.shift_right_logical(seg - done, rows.bit_length() - 1)
        end = _sublane_scan(n, e)
        sel = _owner(end, lane)
        off = _pick(sel, done) + (lane1 - _pick(sel, end - n)) * rows
        n_c = end[e - 1:e, :]
        ok = lane1 < n_c
        lists += [jnp.where(ok, _pick(sel, start) + off, 0), jnp.where(ok, _pick(sel, kbase) + off, 0)]
        counts = jnp.where(lane1 == ci, n_c, counts)
        done = done + n * rows
    desc_ref[j] = jnp.concatenate(
        lists + [counts, jnp.zeros((DESC_ROWS - DESC_COUNT - 1, LANES), jnp.int32)], axis=0)
    run_ref[...] = run_ref[...] + seg


def _plan_call(idx, tot, n_blocks):
    n = idx.shape[1]
    tp = TC_COMB
    n_tiles = n // tp
    assert n_tiles % PLAN_TILES == 0
    nb_pad = -(-n_blocks // LANES) * LANES
    tile = lambda i: (i, 0, 0)
    return pl.pallas_call(
        _plan_kernel, grid=(n_tiles // PLAN_TILES,), name="route_plan",
        in_specs=[pl.BlockSpec((TOP_K, PLAN_TILES * tp), lambda i: (0, i)), _const_spec((N_EXPERTS, LANES))],
        out_specs=[pl.BlockSpec((PLAN_TILES, SUBLANES, tp), tile),
                   pl.BlockSpec((PLAN_TILES, DESC_ROWS, LANES), tile),
                   pl.BlockSpec((SUBLANES, nb_pad), lambda i: (0, 0))],
        out_shape=[jax.ShapeDtypeStruct((n_tiles, SUBLANES, tp), jnp.int32),
                   jax.ShapeDtypeStruct((n_tiles, DESC_ROWS, LANES), jnp.int32),
                   jax.ShapeDtypeStruct((SUBLANES, nb_pad), jnp.int32)],
        scratch_shapes=[pltpu.VMEM((tp, tp), BF16), pltpu.VMEM((N_EXPERTS, LANES), jnp.int32)],
        compiler_params=_params())(idx, tot)


def _slot_hits(kp, iota):
    hit = jnp.where(iota == kp[0], 1.0, 0.0)
    for k in range(1, TOP_K):
        hit = hit + jnp.where(iota == kp[k], 1.0, 0.0)
    return hit


def _for_each_copy(count, fn):
    for ci, rows in enumerate(COPY_ROWS):
        def body(j, carry, ci=ci, rows=rows):
            fn(2 * ci, j, rows)
            return carry
        lax.fori_loop(0, count(ci), body, 0)


def _dispatch_kernel(n_tiles, n_blocks, nvalid_ref, desc_ref, kpos_ref, xs_ref, out_hbm, zero_ref, buf,
                     sems, zsem, issued_ref):
    i = pl.program_id(0)
    td = xs_ref.shape[0]
    bg = zero_ref.shape[0]
    cur = i % 2

    def copy(b, dst, src, rows):
        groups = rows // SEG_ALIGN
        return pltpu.make_async_copy(buf.at[b, pl.ds(lax.shift_right_logical(src, LOG2_SEG), groups)],
                                     out_hbm.at[pl.ds(lax.shift_right_logical(dst, LOG2_SEG), groups)], sems.at[b])

    def drain(b):
        _for_each_copy(lambda ci: issued_ref[b, ci], lambda row, j, rows: copy(b, 0, 0, rows).wait())

    @pl.when(i == 0)
    def _():
        zero_ref[...] = jnp.zeros_like(zero_ref)

        def fill(b):
            return pltpu.make_async_copy(zero_ref, out_hbm.at[pl.ds(b * bg, bg)], zsem)

        def start(b, carry):
            @pl.when(nvalid_ref[b] < bg * SEG_ALIGN)
            def _():
                fill(b).start()
            return carry

        def wait(b, carry):
            @pl.when(nvalid_ref[b] < bg * SEG_ALIGN)
            def _():
                fill(b).wait()
            return carry
        lax.fori_loop(0, n_blocks, start, 0)
        lax.fori_loop(0, n_blocks, wait, 0)

    @pl.when(i >= 2)
    def _():
        drain(cur)

    kp = kpos_ref[0]
    x = xs_ref[...].astype(BF16)
    for c in range(K_ROWS // M_CHUNK):
        rows = lax.broadcasted_iota(jnp.int32, (M_CHUNK, td), 0) + c * M_CHUNK
        sel = _slot_hits([kp[k:k + 1, :] for k in range(TOP_K)], rows).astype(BF16)
        mg = M_CHUNK // SEG_ALIGN
        buf[cur, c * mg:(c + 1) * mg] = jnp.dot(sel, x, preferred_element_type=F32).reshape(mg, SEG_ALIGN, -1)

    _for_each_copy(lambda ci: desc_ref[0, DESC_COUNT, ci],
                   lambda row, j, rows: copy(cur, desc_ref[0, row, j], desc_ref[0, row + 1, j], rows).start())
    for ci in range(len(COPY_ROWS)):
        issued_ref[cur, ci] = desc_ref[0, DESC_COUNT, ci]

    @pl.when(i == n_tiles - 1)
    def _():
        drain(cur)

        @pl.when(i >= 1)
        def _():
            drain(1 - cur)


def _dispatch_call(nvalid, desc, kpos, xs, n_blocks):
    n, width = xs.shape
    td, bm = TC_COMB, BM_MOE
    n_tiles = n // td
    gs = pltpu.PrefetchScalarGridSpec(
        num_scalar_prefetch=1, grid=(n_tiles,),
        in_specs=[pl.BlockSpec((1, DESC_ROWS, LANES), lambda i, nv: (i, 0, 0), memory_space=pltpu.SMEM),
                  pl.BlockSpec((1, SUBLANES, td), lambda i, nv: (i, 0, 0)),
                  pl.BlockSpec((td, width), lambda i, nv: (i, 0))],
        out_specs=pl.BlockSpec(memory_space=pl.ANY),
        scratch_shapes=[pltpu.VMEM((bm // SEG_ALIGN, SEG_ALIGN, width), xs.dtype),
                        pltpu.VMEM((2, K_ROWS // SEG_ALIGN, SEG_ALIGN, width), xs.dtype),
                        pltpu.SemaphoreType.DMA((2,)), pltpu.SemaphoreType.DMA(()),
                        pltpu.SMEM((2, len(COPY_ROWS)), jnp.int32)])
    return pl.pallas_call(
        functools.partial(_dispatch_kernel, n_tiles, n_blocks), grid_spec=gs, name="moe_dispatch",
        out_shape=jax.ShapeDtypeStruct((n_blocks * bm // SEG_ALIGN, SEG_ALIGN, width), xs.dtype),
        compiler_params=_params())(nvalid, desc, kpos, xs)


def _ffn_kernel(bexp_ref, nused_ref, x_ref, wg_hbm, bg_ref, wu_hbm, bu_ref, wd_hbm, bd_ref,
                y_ref, wf32, wbf, wsems, slot_ref):
    b = pl.program_id(0)
    nused = nused_ref[0]
    e = bexp_ref[b]
    wgb, wub, wdb = wbf.at[0], wbf.at[1], wbf.at[2]

    def fetch(expert, slot):
        return [pltpu.make_async_copy(src.at[expert], wf32.at[slot, j], wsems.at[slot])
                for j, src in enumerate((wg_hbm, wu_hbm, wd_hbm))]

    @pl.when(b == 0)
    def _():
        slot_ref[0] = 0
        for d in fetch(e, 0):
            d.start()

    @pl.when((b < nused) & ((b == 0) | (e != bexp_ref[jnp.maximum(b - 1, 0)])))
    def _():
        slot = slot_ref[0]
        for d in fetch(e, slot):
            d.wait()
        for j in range(3):
            wbf[j] = wf32[slot, j].astype(BF16)
        nxt = lax.while_loop(lambda j: (j < nused) & (bexp_ref[jnp.minimum(j, nused - 1)] == e),
                             lambda j: j + 1, b + 1)

        @pl.when(nxt < nused)
        def _():
            for d in fetch(bexp_ref[nxt], 1 - slot):
                d.start()
        slot_ref[0] = 1 - slot

    @pl.when(b < nused)
    def _():
        rows = x_ref[...].reshape(-1, XS_W)
        x = rows[:, 0:D_MODEL].astype(BF16)
        aux = rows[:, D_MODEL:XS_W]
        eid = bexp_ref[b].astype(F32)
        row_w = jnp.zeros((x.shape[0], 1), F32)
        for k in range(TOP_K):
            w_k = aux[:, AUX_W_HI + k:AUX_W_HI + k + 1] + aux[:, AUX_W_LO + k:AUX_W_LO + k + 1]
            row_w = row_w + jnp.where(aux[:, AUX_EID + k:AUX_EID + k + 1] == eid, w_k, 0.0)

        def proj(w_ref, b_ref):
            return jnp.dot(x, w_ref[...], preferred_element_type=F32) + b_ref[0]

        a = jnp.minimum(proj(wgb, bg_ref), SWIGLU_LIMIT)
        u = jnp.clip(proj(wub, bu_ref), -SWIGLU_LIMIT, SWIGLU_LIMIT)
        hid = (a * jax.nn.sigmoid(SWIGLU_ALPHA * a) * (u + 1.0)).astype(BF16)
        y = (jnp.dot(hid, wdb[...], preferred_element_type=F32) + bd_ref[0]) * row_w
        y_ref[...] = y.reshape(y_ref.shape)

    @pl.when(b >= nused)
    def _():
        y_ref[...] = jnp.zeros_like(y_ref)


def _ffn_call(bexp, nused, xs_sorted, w_gate, b_gate, w_up, b_up, w_down, b_down):
    bm = BM_MOE
    bg = bm // SEG_ALIGN
    n_blocks = xs_sorted.shape[0] // bg
    assert D_FF == D_MODEL, "the three expert matrices share one staging shape"
    bspec = lambda c: pl.BlockSpec((1, 1, c), lambda b, be, nu: (be[b], 0, 0))
    whole = pl.BlockSpec(memory_space=pl.ANY)
    gs = pltpu.PrefetchScalarGridSpec(
        num_scalar_prefetch=2, grid=(n_blocks,),
        in_specs=[pl.BlockSpec((bg, SEG_ALIGN, XS_W), lambda b, be, nu: (jnp.minimum(b, nu[0] - 1), 0, 0)),
                  whole, bspec(D_FF), whole, bspec(D_FF), whole, bspec(D_MODEL)],
        out_specs=pl.BlockSpec((bg, SEG_ALIGN, D_MODEL), lambda b, be, nu: (b, 0, 0)),
        scratch_shapes=[pltpu.VMEM((2, 3, D_MODEL, D_FF), F32), pltpu.VMEM((3, D_MODEL, D_FF), BF16),
                        pltpu.SemaphoreType.DMA((2,)), pltpu.SMEM((1,), jnp.int32)])
    return pl.pallas_call(
        _ffn_kernel, grid_spec=gs, name="expert_ffn",
        out_shape=jax.ShapeDtypeStruct((n_blocks * bg, SEG_ALIGN, D_MODEL), F32),
        compiler_params=_params())(bexp, nused, xs_sorted, w_gate, b_gate, w_up, b_up, w_down, b_down)


def _combine_kernel(nt, desc_ref, descn_ref, kpos_ref, y_hbm, h_ref, gf_ref, out_ref, ybuf, sems):
    i = pl.program_id(0)
    tc = h_ref.shape[0]
    cur = i % 2

    def copy(b, dst, src, rows):
        groups = rows // SEG_ALIGN
        return pltpu.make_async_copy(y_hbm.at[pl.ds(lax.shift_right_logical(dst, LOG2_SEG), groups)],
                                     ybuf.at[b, pl.ds(lax.shift_right_logical(src, LOG2_SEG), groups)], sems.at[b])

    def fetch(b, ref):
        _for_each_copy(lambda ci: ref[0, DESC_COUNT, ci],
                       lambda row, j, rows: copy(b, ref[0, row, j], ref[0, row + 1, j], rows).start())

    @pl.when(i == 0)
    def _():
        ybuf[...] = jnp.zeros_like(ybuf)
        fetch(0, desc_ref)

    @pl.when(i + 1 < nt)
    def _():
        fetch(1 - cur, descn_ref)

    _for_each_copy(lambda ci: desc_ref[0, DESC_COUNT, ci], lambda row, j, rows: copy(cur, 0, 0, rows).wait())

    kpt = jnp.transpose(kpos_ref[0].astype(F32))
    kp = [kpt[:, k:k + 1] for k in range(TOP_K)]
    cols = lax.broadcasted_iota(jnp.int32, (tc, K_ROWS), 1).astype(F32)
    pick = _slot_hits(kp, cols).astype(BF16)
    y = ybuf[cur].reshape(K_ROWS, -1).astype(BF16)
    out_ref[...] = _rms(h_ref[...] + jnp.dot(pick, y, preferred_element_type=F32), gf_ref[...])


def _combine_call(desc, kpos, y, h2, g_final, tile0, n_tiles):
    tc = TC_COMB
    last = tile0 + n_tiles - 1
    return pl.pallas_call(
        functools.partial(_combine_kernel, n_tiles), grid=(n_tiles,), name="moe_combine",
        in_specs=[pl.BlockSpec((1, DESC_ROWS, LANES), lambda i: (tile0 + i, 0, 0), memory_space=pltpu.SMEM),
                  pl.BlockSpec((1, DESC_ROWS, LANES), lambda i: (jnp.minimum(tile0 + i + 1, last), 0, 0),
                               memory_space=pltpu.SMEM),
                  pl.BlockSpec((1, SUBLANES, tc), lambda i: (tile0 + i, 0, 0)),
                  pl.BlockSpec(memory_space=pl.ANY),
                  pl.BlockSpec((tc, D_MODEL), lambda i: (tile0 + i, 0)),
                  _const_spec((1, D_MODEL))],
        out_specs=pl.BlockSpec((tc, D_MODEL), lambda i: (i, 0)),
        out_shape=jax.ShapeDtypeStruct((n_tiles * tc, D_MODEL), F32),
        scratch_shapes=[pltpu.VMEM((2, K_ROWS // SEG_ALIGN, SEG_ALIGN, D_MODEL), F32),
                        pltpu.SemaphoreType.DMA((2,))],
        compiler_params=_params())(desc, desc, kpos, y, h2, g_final)


def _forward(xs, mems, g_mix, w_in, sink, conv_w, w_attn_br, w_conv_br, w_mix_out, g_cross, g_mem,
             w_xq, w_xkv, w_xo, g_moe, w_router, b_router, w_gate, b_gate, w_up, b_up, w_down,
             b_down, g_final):
    groups = [(x.shape[0], x.shape[1]) for x in xs]
    for _, s in groups:
        assert s % max(TM_QKV, TQ_ATTN, TM_MIX, TM_CROSS, TC_COMB) == 0
    assert len(xs) == 2, "two request groups"
    xa, xb = (x.reshape(-1, D_MODEL) for x in xs)
    n = xa.shape[0] + xb.shape[0]
    tables = _rope_tables(max(s for _, s in groups))
    assert w_in.shape[0] == 1, "single-layer trunk: the final norm is fused into the combine kernel"
    l = 0
    row2 = lambda v: v.reshape(1, -1)
    w_in_b = w_in[l].astype(BF16)
    perm = np.array([(kv * GROUP + g) * HEAD_DIM + d
                     for g in range(GROUP) for kv in range(N_KV_HEADS) for d in range(HEAD_DIM)])
    w_qkv = jnp.concatenate([w_in_b[:, :ATTN_W][:, perm], w_in_b[:, ATTN_W:QKV_W]], axis=1)
    q, kv = _qkv_call(xa, xb, row2(g_mix[l]), w_qkv, tables, groups)
    o = _attn_call(q, kv, sink[l], groups)
    h = _mixer_call(xa, xb, o, row2(g_mix[l]), w_in_b[:, QKV_W:], conv_w[l],
                    w_attn_br[l].astype(BF16)[perm, :], w_conv_br[l].astype(BF16),
                    w_mix_out[l].astype(BF16), groups)
    memkv = _memkv_call(mems[0], mems[1], row2(g_mem[l]), w_xkv[l].astype(BF16))
    wr = w_router[l]
    wr_hi = wr.astype(BF16)
    wr_lo = (wr - wr_hi.astype(F32)).astype(BF16)
    w_r2t = jnp.concatenate([wr_hi.T, wr_lo.T], axis=0)
    h2, xs_rows, idx, tot = _cross_call(h, memkv, row2(g_cross[l]), w_xq[l].astype(BF16),
                                        w_xo[l].astype(BF16), row2(g_moe[l]), w_r2t,
                                        b_router[l].reshape(-1, 1), groups)
    seg_rows = TOP_K * n + (n // TC_COMB) * N_EXPERTS * (SEG_ALIGN - 1)
    n_blocks = -(-seg_rows // BM_MOE) + N_EXPERTS
    kpos, desc, meta = _plan_call(idx, tot, n_blocks)
    xs_sorted = _dispatch_call(meta[1], desc, kpos, xs_rows, n_blocks)
    y = _ffn_call(meta[0], meta[2, 0:1], xs_sorted, w_gate[l], b_gate[l][:, None, :],
                  w_up[l], b_up[l][:, None, :], w_down[l], b_down[l][:, None, :])
    tc = TC_COMB
    outs, t0 = [], 0
    for nb, s in groups:
        nt = nb * s // tc
        outs.append(_combine_call(desc, kpos, y, h2, row2(g_final), t0, nt))
        t0 += nt
    return tuple(o.reshape(x.shape) for o, x in zip(outs, xs))


def kernel(x_prompt, x_sample, mem_prompt, mem_sample, g_mix, w_in, sink, conv_w, w_attn_br, w_conv_br,
           w_mix_out, g_cross, g_mem, w_xq, w_xkv, w_xo, g_moe, w_router, b_router, w_gate, b_gate,
           w_up, b_up, w_down, b_down, g_final):
    return _forward([x_prompt, x_sample], [mem_prompt, mem_sample], g_mix, w_in, sink, conv_w,
                    w_attn_br, w_conv_br, w_mix_out, g_cross, g_mem, w_xq, w_xkv, w_xo, g_moe,
                    w_router, b_router, w_gate, b_gate, w_up, b_up, w_down, b_down, g_final)
```

```python
import functools

import numpy as np
import jax
import jax.numpy as jnp
from jax import lax
from jax.experimental import pallas as pl
from jax.experimental.pallas import tpu as pltpu

F32 = jnp.float32
BF16 = jnp.bfloat16

D_MODEL = 1024
N_HEADS = 8
N_KV_HEADS = 2
HEAD_DIM = 64
GROUP = N_HEADS // N_KV_HEADS
ATTN_W = N_HEADS * HEAD_DIM
KV_W = N_KV_HEADS * HEAD_DIM
QKV_W = ATTN_W + 2 * KV_W
WINDOW = 128
ROT_DIM = HEAD_DIM // 4
ROPE_THETA = 500000.0
CONV_W = D_MODEL // 2
REST_W = 3 * CONV_W + 2 * D_MODEL
N_MEM = 256
X_HEADS = 4
X_HEAD_DIM = D_MODEL // X_HEADS
N_EXPERTS = 32
TOP_K = 4
D_FF = D_MODEL
SWIGLU_ALPHA = 1.702
SWIGLU_LIMIT = 7.0
EPS = 1e-5

LANES = 128
SUBLANES = 8
KEY_BLOCK = WINDOW
TM_QKV = 1024
TQ_ATTN = 1024
TM_MIX = 512
TM_CROSS = 1024
BM_MOE = 512
TC_COMB = 256
NEG_BIG = -1e30
VMEM_LIMIT = 56 * 1024 * 1024


def _rms(x, g):
    var = jnp.mean(x * x, axis=-1, keepdims=True)
    return x * lax.rsqrt(var + EPS) * g


def _tile_meta(groups, tile):
    pos, first, last, bidx = [], [], [], []
    b0 = 0
    for nb, s in groups:
        per = s // tile
        for b in range(nb):
            for j in range(per):
                pos.append(j)
                first.append(int(j == 0))
                last.append(int(j == per - 1))
                bidx.append(b0 + b)
        b0 += nb
    return jnp.asarray(np.array([pos, first, last, bidx], dtype=np.int32))


def _const_spec(shape):
    nd = len(shape)
    return pl.BlockSpec(shape, lambda *_: (0,) * nd, pipeline_mode=pl.Buffered(1))


def _params(vmem=VMEM_LIMIT):
    return pltpu.CompilerParams(dimension_semantics=("arbitrary",), vmem_limit_bytes=vmem)


def _qkv_kernel(na, meta_ref, xa_ref, xb_ref, g_ref, w_ref, cos_ref, sa_ref, sb_ref, q_ref, kv_ref):
    del meta_ref
    x = jnp.where(pl.program_id(0) < na, xa_ref[...], xb_ref[...])
    xn = _rms(x, g_ref[...]).astype(BF16)
    proj = jnp.dot(xn, w_ref[...], preferred_element_type=F32)
    c, sa, sb = cos_ref[...], sa_ref[...], sb_ref[...]
    n_rot = (ATTN_W + KV_W) // LANES
    for gi in range(n_rot):
        p = proj[:, gi * LANES:(gi + 1) * LANES]
        r = p * c + pltpu.roll(p, LANES - ROT_DIM // 2, 1) * sa + pltpu.roll(p, ROT_DIM // 2, 1) * sb
        if gi < ATTN_W // LANES:
            q_ref[:, gi * LANES:(gi + 1) * LANES] = (r * (HEAD_DIM ** -0.5)).astype(BF16)
        else:
            kv_ref[:, 0:KV_W] = r.astype(BF16)
    kv_ref[:, KV_W:2 * KV_W] = proj[:, ATTN_W + KV_W:QKV_W].astype(BF16)


def _rope_tables(s_max):
    half = ROT_DIM // 2
    inv_freq = ROPE_THETA ** (-(jnp.arange(half, dtype=F32) * 2.0) / ROT_DIM)
    ang = jnp.arange(s_max, dtype=F32)[:, None] * inv_freq[None, :]
    cos, sin = jnp.cos(ang), jnp.sin(ang)
    d = np.arange(LANES) % HEAD_DIM
    j = d % half
    cos_l, sin_l = cos[:, j], sin[:, j]
    rot = jnp.asarray(d < ROT_DIM)[None, :]
    lo = jnp.asarray(d < half)[None, :]
    hi = jnp.asarray((d >= half) & (d < ROT_DIM))[None, :]
    c = jnp.where(rot, cos_l, 1.0)
    sa = jnp.where(lo, -sin_l, 0.0)
    sb = jnp.where(hi, sin_l, 0.0)
    return c, sa, sb


def _qkv_call(xa, xb, g_mix, w_qkv, tables, groups):
    tm = TM_QKV
    na = xa.shape[0] // tm
    n = xa.shape[0] + xb.shape[0]
    meta = _tile_meta(groups, tm)
    row = lambda i, m: (i, 0)
    tab = lambda i, m: (m[0, i], 0)
    gs = pltpu.PrefetchScalarGridSpec(
        num_scalar_prefetch=1, grid=(n // tm,),
        in_specs=[pl.BlockSpec((tm, D_MODEL), lambda i, m: (jnp.minimum(i, na - 1), 0)),
                  pl.BlockSpec((tm, D_MODEL), lambda i, m: (jnp.maximum(i - na, 0), 0)),
                  _const_spec((1, D_MODEL)),
                  _const_spec((D_MODEL, QKV_W)),
                  pl.BlockSpec((tm, LANES), tab), pl.BlockSpec((tm, LANES), tab),
                  pl.BlockSpec((tm, LANES), tab)],
        out_specs=[pl.BlockSpec((tm, ATTN_W), row), pl.BlockSpec((tm, 2 * KV_W), row)])
    return pl.pallas_call(
        functools.partial(_qkv_kernel, na), grid_spec=gs, name="qkv_rope",
        out_shape=[jax.ShapeDtypeStruct((n, ATTN_W), BF16), jax.ShapeDtypeStruct((n, 2 * KV_W), BF16)],
        compiler_params=_params())(meta, xa, xb, g_mix, w_qkv, *tables)


def _attn_kernel(meta_ref, sink_ref, q_ref, kvp_ref, kvm_ref, kvn_ref, o_ref, kcat_ref):
    i = pl.program_id(0)
    kb = KEY_BLOCK
    tq = q_ref.shape[0]
    kcat_ref[0:kb, :] = kvp_ref[...]
    kcat_ref[kb:kb + tq, :] = kvm_ref[...]
    kcat_ref[kb + tq:kb + tq + kb, :] = kvn_ref[...]
    r = lax.broadcasted_iota(jnp.int32, (kb, kb), 0)
    c = lax.broadcasted_iota(jnp.int32, (kb, kb), 1)
    band_prev = jnp.where(c >= r, 0.0, NEG_BIG)
    band_next = jnp.where(c <= r, 0.0, NEG_BIG)
    klane = lax.broadcasted_iota(jnp.int32, (3 * kb, KV_W), 1)
    olane = lax.broadcasted_iota(jnp.int32, (GROUP * kb, KV_W), 1)
    nsub = tq // kb
    for s in range(nsub):
        pen_prev = jnp.where(meta_ref[1, i * nsub + s] == 1, NEG_BIG, 0.0)
        pen_next = jnp.where(meta_ref[2, i * nsub + s] == 1, NEG_BIG, 0.0)
        bias_prev = jnp.concatenate([band_prev + pen_prev] * GROUP, axis=0)
        bias_next = jnp.concatenate([band_next + pen_next] * GROUP, axis=0)
        kw = kcat_ref[s * kb:(s + 3) * kb, :]
        kc = kw[:, 0:KV_W]
        vc = jnp.concatenate([kw[:, KV_W:2 * KV_W], jnp.ones((3 * kb, KV_W), BF16)], axis=1)
        q4 = jnp.concatenate([q_ref[s * kb:(s + 1) * kb, g * KV_W:(g + 1) * KV_W] for g in range(GROUP)],
                             axis=0)
        res = []
        for kh in range(N_KV_HEADS):
            own = (klane >= kh * HEAD_DIM) & (klane < (kh + 1) * HEAD_DIM)
            kk = jnp.where(own, kc, jnp.zeros_like(kc))
            sc = lax.dot_general(q4, kk, (((1,), (1,)), ((), ())), preferred_element_type=F32)
            sc = jnp.concatenate([sc[:, 0:kb] + bias_prev, sc[:, kb:2 * kb], sc[:, 2 * kb:3 * kb] + bias_next],
                                 axis=1)
            snk = jnp.concatenate([jnp.full((kb, 1), sink_ref[kh * GROUP + g], F32) for g in range(GROUP)],
                                  axis=0)
            m = jnp.maximum(jnp.max(sc, axis=-1, keepdims=True), snk)
            p = jnp.exp(sc - m).astype(BF16)
            pv = jnp.dot(p, vc, preferred_element_type=F32)
            denom = pv[:, KV_W:KV_W + 1] + jnp.exp(snk - m)
            res.append(pv[:, 0:KV_W] / denom)
        out = res[0]
        for kh in range(1, N_KV_HEADS):
            out = jnp.where(olane >= kh * HEAD_DIM, res[kh], out)
        out = out.astype(BF16)
        for g in range(GROUP):
            o_ref[s * kb:(s + 1) * kb, g * KV_W:(g + 1) * KV_W] = out[g * kb:(g + 1) * kb, :]


def _attn_call(q, kv, sink, groups):
    n = q.shape[0]
    tq, kb = TQ_ATTN, KEY_BLOCK
    per = tq // kb
    nkb = n // kb
    meta = _tile_meta(groups, kb)
    gs = pltpu.PrefetchScalarGridSpec(
        num_scalar_prefetch=1, grid=(n // tq,),
        in_specs=[pl.BlockSpec(memory_space=pltpu.SMEM),
                  pl.BlockSpec((tq, ATTN_W), lambda i, m: (i, 0)),
                  pl.BlockSpec((kb, 2 * KV_W), lambda i, m: (jnp.maximum(i * per - 1, 0), 0)),
                  pl.BlockSpec((tq, 2 * KV_W), lambda i, m: (i, 0)),
                  pl.BlockSpec((kb, 2 * KV_W), lambda i, m: (jnp.minimum((i + 1) * per, nkb - 1), 0))],
        out_specs=pl.BlockSpec((tq, ATTN_W), lambda i, m: (i, 0)),
        scratch_shapes=[pltpu.VMEM((tq + 2 * kb, 2 * KV_W), BF16)])
    return pl.pallas_call(
        _attn_kernel, grid_spec=gs, name="window_attn",
        out_shape=jax.ShapeDtypeStruct((n, ATTN_W), BF16),
        compiler_params=_params())(meta, sink, q, kv, kv, kv)


def _mixer_kernel(na, meta_ref, xa_ref, xpa_ref, xna_ref, xb_ref, xpb_ref, xnb_ref, o_ref, g_ref, w_ref,
                  cw_ref, wa_ref, wc_ref, wm_ref, h_ref):
    i = pl.program_id(0)
    tm = xa_ref.shape[0]
    g = g_ref[...]
    in_a = i < na
    x = jnp.where(in_a, xa_ref[...], xb_ref[...])
    xh = jnp.concatenate([jnp.where(in_a, xpa_ref[...], xpb_ref[...]),
                          jnp.where(in_a, xna_ref[...], xnb_ref[...])], axis=0)
    xb = jnp.concatenate([_rms(x, g).astype(BF16), _rms(xh, g).astype(BF16)], axis=0)
    proj_all = jnp.dot(xb, w_ref[...], preferred_element_type=F32)
    proj = proj_all[0:tm, :]
    cb = proj[:, 0:CONV_W]
    u = proj[:, CONV_W:2 * CONV_W] * proj[:, 2 * CONV_W:3 * CONV_W]
    uh = proj_all[tm:tm + 2 * SUBLANES, CONV_W:2 * CONV_W] * proj_all[tm:tm + 2 * SUBLANES, 2 * CONV_W:3 * CONV_W]
    first = meta_ref[1, i]
    last = meta_ref[2, i]
    u_prev = jnp.where(first == 1, 0.0, uh[SUBLANES - 1:SUBLANES, :])
    u_next = jnp.where(last == 1, 0.0, uh[SUBLANES:SUBLANES + 1, :])
    row = lax.broadcasted_iota(jnp.int32, (tm, 1), 0)
    up = jnp.where(row == 0, u_prev, pltpu.roll(u, 1, 0))
    dn = jnp.where(row == tm - 1, u_next, pltpu.roll(u, tm - 1, 0))
    cw = cw_ref[...]
    y = up * cw[0:1, :] + u * cw[1:2, :] + dn * cw[2:3, :]
    conv = (cb * y).astype(BF16)
    conv_br = jnp.dot(conv, wc_ref[...], preferred_element_type=F32)
    attn_br = jnp.dot(o_ref[...], wa_ref[...], preferred_element_type=F32)
    g0 = jax.nn.sigmoid(proj[:, 3 * CONV_W:3 * CONV_W + D_MODEL])
    g1 = jax.nn.sigmoid(proj[:, 3 * CONV_W + D_MODEL:REST_W])
    merged = (g0 * attn_br + g1 * conv_br).astype(BF16)
    h_ref[...] = x + jnp.dot(merged, wm_ref[...], preferred_element_type=F32)


def _mixer_call(xa, xb, o, g_mix, w_rest, conv_w, w_attn_br, w_conv_br, w_mix_out, groups):
    tm = TM_MIX
    per = tm // SUBLANES
    na, nb = xa.shape[0] // tm, xb.shape[0] // tm
    n = xa.shape[0] + xb.shape[0]
    meta = _tile_meta(groups, tm)

    def x_specs(tile_of, n_tiles):
        last8 = n_tiles * per - 1
        return [pl.BlockSpec((tm, D_MODEL), lambda i, m: (jnp.clip(tile_of(i), 0, n_tiles - 1), 0)),
                pl.BlockSpec((SUBLANES, D_MODEL), lambda i, m: (jnp.clip(tile_of(i) * per - 1, 0, last8), 0)),
                pl.BlockSpec((SUBLANES, D_MODEL), lambda i, m: (jnp.clip((tile_of(i) + 1) * per, 0, last8), 0))]

    gs = pltpu.PrefetchScalarGridSpec(
        num_scalar_prefetch=1, grid=(n // tm,),
        in_specs=x_specs(lambda i: i, na) + x_specs(lambda i: i - na, nb) + [
                  pl.BlockSpec((tm, ATTN_W), lambda i, m: (i, 0)),
                  _const_spec((1, D_MODEL)),
                  _const_spec((D_MODEL, REST_W)),
                  _const_spec((3, CONV_W)),
                  _const_spec((ATTN_W, D_MODEL)),
                  _const_spec((CONV_W, D_MODEL)),
                  _const_spec((D_MODEL, D_MODEL))],
        out_specs=pl.BlockSpec((tm, D_MODEL), lambda i, m: (i, 0)))
    return pl.pallas_call(
        functools.partial(_mixer_kernel, na), grid_spec=gs, name="mixer",
        out_shape=jax.ShapeDtypeStruct((n, D_MODEL), F32),
        compiler_params=_params())(meta, xa, xa, xa, xb, xb, xb, o, g_mix, w_rest, conv_w, w_attn_br,
                                   w_conv_br, w_mix_out)


def _memkv_kernel(na, mema_ref, memb_ref, g_ref, w_ref, kv_ref):
    mem = jnp.where(pl.program_id(0) < na, mema_ref[0], memb_ref[0])
    mn = _rms(mem, g_ref[...]).astype(BF16)
    kv_ref[0] = jnp.dot(mn, w_ref[...], preferred_element_type=F32).astype(BF16)


def _memkv_call(mem_a, mem_b, g_mem, w_xkv):
    na, nb = mem_a.shape[0], mem_b.shape[0]
    return pl.pallas_call(
        functools.partial(_memkv_kernel, na), grid=(na + nb,), name="mem_kv",
        in_specs=[pl.BlockSpec((1, N_MEM, D_MODEL), lambda b: (jnp.minimum(b, na - 1), 0, 0)),
                  pl.BlockSpec((1, N_MEM, D_MODEL), lambda b: (jnp.maximum(b - na, 0), 0, 0)),
                  _const_spec((1, D_MODEL)),
                  _const_spec((D_MODEL, 2 * D_MODEL))],
        out_specs=pl.BlockSpec((1, N_MEM, 2 * D_MODEL), lambda b: (b, 0, 0)),
        out_shape=jax.ShapeDtypeStruct((na + nb, N_MEM, 2 * D_MODEL), BF16),
        compiler_params=_params())(mem_a, mem_b, g_mem, w_xkv)


def _cross_kernel(meta_ref, h_ref, kv_ref, gc_ref, wq_ref, wo_ref, gm_ref, wr_ref, br_ref,
                  h2_ref, xs_ref, idx_ref, tot_ref):
    del meta_ref
    tm = h_ref.shape[0]
    h = h_ref[...]
    hn = _rms(h, gc_ref[...]).astype(BF16)
    q = (jnp.dot(hn, wq_ref[...], preferred_element_type=F32) * (X_HEAD_DIM ** -0.5)).astype(BF16)
    outs = []
    for hd in range(X_HEADS):
        qh = q[:, hd * X_HEAD_DIM:(hd + 1) * X_HEAD_DIM]
        kh = kv_ref[0, :, hd * X_HEAD_DIM:(hd + 1) * X_HEAD_DIM]
        vh = kv_ref[0, :, D_MODEL + hd * X_HEAD_DIM:D_MODEL + (hd + 1) * X_HEAD_DIM]
        s = lax.dot_general(qh, kh, (((1,), (1,)), ((), ())), preferred_element_type=F32)
        m = jnp.max(s, axis=-1, keepdims=True)
        p = jnp.exp(s - m)
        p = (p / jnp.sum(p, axis=-1, keepdims=True)).astype(BF16)
        outs.append(jnp.dot(p, vh, preferred_element_type=F32).astype(BF16))
    o = jnp.concatenate(outs, axis=1)
    h2 = h + jnp.dot(o, wo_ref[...], preferred_element_type=F32)
    h2_ref[...] = h2

    hn3 = _rms(h2, gm_ref[...])
    hi = hn3.astype(BF16)
    hi32 = hi.astype(F32)
    lo = (hn3 - hi32).astype(BF16)
    xs_ref[:, 0:D_MODEL] = hn3

    nt = (((1,), (1,)), ((), ()))
    r1 = lax.dot_general(wr_ref[...], hi, nt, preferred_element_type=F32)
    r2 = lax.dot_general(wr_ref[...], lo, nt, preferred_element_type=F32)
    e = N_EXPERTS
    logits = ((r2[e:2 * e] + r2[0:e]) + r1[e:2 * e]) + r1[0:e] + br_ref[...]
    eio = lax.broadcasted_iota(jnp.int32, (e, tm), 0)
    cur = logits
    vals, sels = [], []
    for _ in range(TOP_K):
        mx = jnp.max(cur, axis=0, keepdims=True)
        sel = jnp.min(jnp.where(cur == mx, eio, e), axis=0, keepdims=True)
        vals.append(mx)
        sels.append(sel)
        cur = jnp.where(eio == sel, -jnp.inf, cur)
    ex = [jnp.exp(v - vals[0]) for v in vals]
    tot = ex[0] + ex[1] + ex[2] + ex[3]
    idx_ref[...] = jnp.concatenate(sels, axis=0)
    assign = jnp.where(eio == sels[0], 1.0, 0.0)
    for k in range(1, TOP_K):
        assign = assign + jnp.where(eio == sels[k], 1.0, 0.0)
    segs = jnp.zeros((e, 1), jnp.int32)
    for j in range(tm // TC_COMB):
        cnt = jnp.sum(assign[:, j * TC_COMB:(j + 1) * TC_COMB], axis=1, keepdims=True).astype(jnp.int32)
        segs = segs + lax.shift_left(lax.shift_right_logical(cnt + (SEG_ALIGN - 1), LOG2_SEG), LOG2_SEG)

    @pl.when(pl.program_id(0) == 0)
    def _():
        tot_ref[...] = jnp.zeros_like(tot_ref)
    tot_ref[...] += jnp.broadcast_to(segs, tot_ref.shape)
    gws = [x / tot for x in ex]
    w_hi = [g.astype(BF16).astype(F32) for g in gws]
    w_lo = [(g - h).astype(BF16).astype(F32) for g, h in zip(gws, w_hi)]
    wt = jnp.transpose(jnp.concatenate(w_hi + w_lo, axis=0))
    et = jnp.transpose(jnp.concatenate([s.astype(F32) for s in sels]
                                       + [jnp.zeros((SUBLANES - TOP_K, tm), F32)], axis=0))
    xs_ref[:, D_MODEL:XS_W] = jnp.concatenate(
        [wt, et, jnp.zeros((tm, LANES - 2 * SUBLANES), F32)], axis=1)


def _cross_call(h1, memkv, g_cross, w_xq, w_xo, g_moe, w_r2t, b_router, groups):
    n = h1.shape[0]
    tm = TM_CROSS
    meta = _tile_meta(groups, tm)
    row = lambda i, m: (i, 0)
    col = lambda i, m: (0, i)
    gs = pltpu.PrefetchScalarGridSpec(
        num_scalar_prefetch=1, grid=(n // tm,),
        in_specs=[pl.BlockSpec((tm, D_MODEL), row),
                  pl.BlockSpec((1, N_MEM, 2 * D_MODEL), lambda i, m: (m[3, i], 0, 0)),
                  _const_spec((1, D_MODEL)),
                  _const_spec((D_MODEL, D_MODEL)),
                  _const_spec((D_MODEL, D_MODEL)),
                  _const_spec((1, D_MODEL)),
                  _const_spec((2 * N_EXPERTS, D_MODEL)),
                  _const_spec((N_EXPERTS, 1))],
        out_specs=[pl.BlockSpec((tm, D_MODEL), row),
                   pl.BlockSpec((tm, XS_W), row),
                   pl.BlockSpec((TOP_K, tm), col),
                   pl.BlockSpec((N_EXPERTS, LANES), lambda i, m: (0, 0))])
    return pl.pallas_call(
        _cross_kernel, grid_spec=gs, name="cross_router",
        out_shape=[jax.ShapeDtypeStruct((n, D_MODEL), F32),
                   jax.ShapeDtypeStruct((n, XS_W), F32),
                   jax.ShapeDtypeStruct((TOP_K, n), jnp.int32),
                   jax.ShapeDtypeStruct((N_EXPERTS, LANES), jnp.int32)],
        compiler_params=_params())(meta, h1, memkv, g_cross, w_xq, w_xo, g_moe, w_r2t, b_router)


LOG2_BM = BM_MOE.bit_length() - 1
assert 1 << LOG2_BM == BM_MOE
SEG_ALIGN = 4
LOG2_SEG = SEG_ALIGN.bit_length() - 1
COPY_ROWS = tuple(SEG_ALIGN << j for j in (3, 2, 1, 0))
K_ROWS = 1152
M_CHUNK = 576
DESC_ROWS = 2 * SUBLANES
PLAN_TILES = 4
assert 1 << LOG2_SEG == SEG_ALIGN and K_ROWS % M_CHUNK == 0 and K_ROWS // COPY_ROWS[0] <= LANES
assert K_ROWS >= TOP_K * TC_COMB + N_EXPERTS * (SEG_ALIGN - 1)
DESC_COUNT = 2 * len(COPY_ROWS)
assert DESC_COUNT < DESC_ROWS and all(a == 2 * b for a, b in zip(COPY_ROWS, COPY_ROWS[1:]))
AUX_W_HI, AUX_W_LO, AUX_EID = 0, TOP_K, 2 * TOP_K
XS_W = D_MODEL + LANES


def _sublane_scan(x, n):
    row = lax.broadcasted_iota(jnp.int32, x.shape, 0)
    sft = 1
    while sft < n:
        x = x + jnp.where(row >= sft, pltpu.roll(x, sft, 0), 0)
        sft *= 2
    return x


def _owner(ends, lane):
    ebl = lax.broadcasted_iota(jnp.int32, ends.shape, 0)
    owner = jnp.sum(jnp.where(ends <= lane, 1.0, 0.0), axis=0, keepdims=True).astype(jnp.int32)
    return ebl == owner


def _pick(sel, val):
    return jnp.sum(jnp.where(sel, val.astype(F32), 0.0), axis=0, keepdims=True).astype(jnp.int32)


def _plan_kernel(idx_ref, tot_ref, kpos_ref, desc_ref, meta_ref, tri_ref, run_ref):
    i = pl.program_id(0)
    e, tp = N_EXPERTS, TC_COMB
    nb_pad = meta_ref.shape[1]

    @pl.when(i == 0)
    def _():
        r = lax.broadcasted_iota(jnp.int32, (tp, tp), 0)
        c = lax.broadcasted_iota(jnp.int32, (tp, tp), 1)
        tri_ref[...] = jnp.where(r < c, 1.0, 0.0).astype(BF16)
        tot = tot_ref[...]
        pad = lax.shift_left(lax.shift_right_logical(tot + (BM_MOE - 1), LOG2_BM), LOG2_BM)
        pad_end = _sublane_scan(pad, e)
        pad_start = pad_end - pad
        run_ref[...] = pad_start
        bpos = lax.broadcasted_iota(jnp.int32, (e, nb_pad), 1) * BM_MOE
        ebl = lax.broadcasted_iota(jnp.int32, (e, nb_pad), 0)
        done = jnp.where(pad_end[:, 0:1] <= bpos, 1.0, 0.0)
        bexp = jnp.minimum(jnp.sum(done, axis=0, keepdims=True).astype(jnp.int32), e - 1)
        row_end = (pad_start + tot)[:, 0:1].astype(F32)
        rend_b = jnp.sum(jnp.where(ebl == bexp, row_end, 0.0), axis=0, keepdims=True).astype(jnp.int32)
        nvalid = jnp.clip(rend_b - bpos[0:1, :], 0, BM_MOE)
        total = jnp.sum(jnp.where(ebl == e - 1, pad_end[:, 0:1].astype(F32), 0.0), axis=0, keepdims=True)
        nused = lax.shift_right_logical(total.astype(jnp.int32), LOG2_BM)
        meta_ref[...] = jnp.concatenate(
            [bexp, nvalid, nused, jnp.zeros((SUBLANES - 3, nb_pad), jnp.int32)], axis=0)

    for j in range(PLAN_TILES):
        _plan_tile(j, idx_ref[:, j * tp:(j + 1) * tp], kpos_ref, desc_ref, tri_ref, run_ref)


def _plan_tile(j, idx, kpos_ref, desc_ref, tri_ref, run_ref):
    e, tp = N_EXPERTS, TC_COMB
    eio = lax.broadcasted_iota(jnp.int32, (e, tp), 0)
    onehots = [eio == idx[k:k + 1, :] for k in range(TOP_K)]
    s = jnp.where(onehots[0], 1.0, 0.0)
    for k in range(1, TOP_K):
        s = s + jnp.where(onehots[k], 1.0, 0.0)
    cnt_t = jnp.sum(s, axis=1, keepdims=True).astype(jnp.int32)
    cnt = jnp.broadcast_to(cnt_t, (e, LANES))
    seg = lax.shift_left(lax.shift_right_logical(cnt + (SEG_ALIGN - 1), LOG2_SEG), LOG2_SEG)
    before = jnp.dot(s.astype(BF16), tri_ref[...], preferred_element_type=F32)
    kbase = _sublane_scan(seg, e) - seg
    pos = kbase[:, 0:1].astype(F32) + before
    kp = [jnp.sum(jnp.where(onehots[k], pos, 0.0), axis=0, keepdims=True).astype(jnp.int32)
          for k in range(TOP_K)]
    kpos_ref[j] = jnp.concatenate(kp + [jnp.zeros((SUBLANES - TOP_K, tp), jnp.int32)], axis=0)

    start = run_ref[...]
    lane = lax.broadcasted_iota(jnp.int32, (e, LANES), 1)
    lane1 = lane[0:1, :]
    lists = []
    counts = jnp.zeros((1, LANES), jnp.int32)
    done = jnp.zeros_like(seg)
    for ci, rows in enumerate(COPY_ROWS):
        n = lax.shift_right_logical(seg - done, rows.bit_length() - 1)
        end = _sublane_scan(n, e)
        sel = _owner(end, lane)
        off = _pick(sel, done) + (lane1 - _pick(sel, end - n)) * rows
        n_c = end[e - 1:e, :]
        ok = lane1 < n_c
        lists += [jnp.where(ok, _pick(sel, start) + off, 0), jnp.where(ok, _pick(sel, kbase) + off, 0)]
        counts = jnp.where(lane1 == ci, n_c, counts)
        done = done + n * rows
    desc_ref[j] = jnp.concatenate(
        lists + [counts, jnp.zeros((DESC_ROWS - DESC_COUNT - 1, LANES), jnp.int32)], axis=0)
    run_ref[...] = run_ref[...] + seg


def _plan_call(idx, tot, n_blocks):
    n = idx.shape[1]
    tp = TC_COMB
    n_tiles = n // tp
    assert n_tiles % PLAN_TILES == 0
    nb_pad = -(-n_blocks // LANES) * LANES
    tile = lambda i: (i, 0, 0)
    return pl.pallas_call(
        _plan_kernel, grid=(n_tiles // PLAN_TILES,), name="route_plan",
        in_specs=[pl.BlockSpec((TOP_K, PLAN_TILES * tp), lambda i: (0, i)), _const_spec((N_EXPERTS, LANES))],
        out_specs=[pl.BlockSpec((PLAN_TILES, SUBLANES, tp), tile),
                   pl.BlockSpec((PLAN_TILES, DESC_ROWS, LANES), tile),
                   pl.BlockSpec((SUBLANES, nb_pad), lambda i: (0, 0))],
        out_shape=[jax.ShapeDtypeStruct((n_tiles, SUBLANES, tp), jnp.int32),
                   jax.ShapeDtypeStruct((n_tiles, DESC_ROWS, LANES), jnp.int32),
                   jax.ShapeDtypeStruct((SUBLANES, nb_pad), jnp.int32)],
        scratch_shapes=[pltpu.VMEM((tp, tp), BF16), pltpu.VMEM((N_EXPERTS, LANES), jnp.int32)],
        compiler_params=_params())(idx, tot)


def _slot_hits(kp, iota):
    hit = jnp.where(iota == kp[0], 1.0, 0.0)
    for k in range(1, TOP_K):
        hit = hit + jnp.where(iota == kp[k], 1.0, 0.0)
    return hit


def _for_each_copy(count, fn):
    for ci, rows in enumerate(COPY_ROWS):
        n = count(ci)

        def pair(j, carry, ci=ci, rows=rows):
            fn(2 * ci, 2 * j, rows)
            fn(2 * ci, 2 * j + 1, rows)
            return carry
        lax.fori_loop(0, lax.shift_right_logical(n, 1), pair, 0)

        @pl.when((n & 1) == 1)
        def _(ci=ci, rows=rows, n=n):
            fn(2 * ci, n - 1, rows)


def _dispatch_kernel(n_tiles, n_blocks, nvalid_ref, desc_ref, kpos_ref, xs_ref, out_hbm, zero_ref, buf,
                     sems, zsem, issued_ref):
    i = pl.program_id(0)
    td = xs_ref.shape[0]
    bg = zero_ref.shape[0]
    cur = i % 2

    def copy(b, dst, src, rows):
        groups = rows // SEG_ALIGN
        return pltpu.make_async_copy(buf.at[b, pl.ds(lax.shift_right_logical(src, LOG2_SEG), groups)],
                                     out_hbm.at[pl.ds(lax.shift_right_logical(dst, LOG2_SEG), groups)], sems.at[b])

    def drain(b):
        _for_each_copy(lambda ci: issued_ref[b, ci], lambda row, j, rows: copy(b, 0, 0, rows).wait())

    @pl.when(i == 0)
    def _():
        zero_ref[...] = jnp.zeros_like(zero_ref)

        def fill(b):
            return pltpu.make_async_copy(zero_ref, out_hbm.at[pl.ds(b * bg, bg)], zsem)

        def start(b, carry):
            @pl.when(nvalid_ref[b] < bg * SEG_ALIGN)
            def _():
                fill(b).start()
            return carry

        def wait(b, carry):
            @pl.when(nvalid_ref[b] < bg * SEG_ALIGN)
            def _():
                fill(b).wait()
            return carry
        lax.fori_loop(0, n_blocks, start, 0)
        lax.fori_loop(0, n_blocks, wait, 0)

    @pl.when(i >= 2)
    def _():
        drain(cur)

    kp = kpos_ref[0]
    x = xs_ref[...].astype(BF16)
    for c in range(K_ROWS // M_CHUNK):
        rows = lax.broadcasted_iota(jnp.int32, (M_CHUNK, td), 0) + c * M_CHUNK
        sel = _slot_hits([kp[k:k + 1, :] for k in range(TOP_K)], rows).astype(BF16)
        mg = M_CHUNK // SEG_ALIGN
        buf[cur, c * mg:(c + 1) * mg] = jnp.dot(sel, x, preferred_element_type=F32).reshape(mg, SEG_ALIGN, -1)

    _for_each_copy(lambda ci: desc_ref[0, DESC_COUNT, ci],
                   lambda row, j, rows: copy(cur, desc_ref[0, row, j], desc_ref[0, row + 1, j], rows).start())
    for ci in range(len(COPY_ROWS)):
        issued_ref[cur, ci] = desc_ref[0, DESC_COUNT, ci]

    @pl.when(i == n_tiles - 1)
    def _():
        drain(cur)

        @pl.when(i >= 1)
        def _():
            drain(1 - cur)


def _dispatch_call(nvalid, desc, kpos, xs, n_blocks):
    n, width = xs.shape
    td, bm = TC_COMB, BM_MOE
    n_tiles = n // td
    gs = pltpu.PrefetchScalarGridSpec(
        num_scalar_prefetch=1, grid=(n_tiles,),
        in_specs=[pl.BlockSpec((1, DESC_ROWS, LANES), lambda i, nv: (i, 0, 0), memory_space=pltpu.SMEM),
                  pl.BlockSpec((1, SUBLANES, td), lambda i, nv: (i, 0, 0)),
                  pl.BlockSpec((td, width), lambda i, nv: (i, 0))],
        out_specs=pl.BlockSpec(memory_space=pl.ANY),
        scratch_shapes=[pltpu.VMEM((bm // SEG_ALIGN, SEG_ALIGN, width), xs.dtype),
                        pltpu.VMEM((2, K_ROWS // SEG_ALIGN, SEG_ALIGN, width), xs.dtype),
                        pltpu.SemaphoreType.DMA((2,)), pltpu.SemaphoreType.DMA(()),
                        pltpu.SMEM((2, len(COPY_ROWS)), jnp.int32)])
    return pl.pallas_call(
        functools.partial(_dispatch_kernel, n_tiles, n_blocks), grid_spec=gs, name="moe_dispatch",
        out_shape=jax.ShapeDtypeStruct((n_blocks * bm // SEG_ALIGN, SEG_ALIGN, width), xs.dtype),
        compiler_params=_params())(nvalid, desc, kpos, xs)


def _ffn_kernel(bexp_ref, nused_ref, x_ref, wg_hbm, bg_ref, wu_hbm, bu_ref, wd_hbm, bd_ref,
                y_ref, wf32, wbf, wsems, slot_ref):
    b = pl.program_id(0)
    nused = nused_ref[0]
    e = bexp_ref[b]
    wgb, wub, wdb = wbf.at[0], wbf.at[1], wbf.at[2]

    def fetch(expert, slot):
        return [pltpu.make_async_copy(src.at[expert], wf32.at[slot, j], wsems.at[slot])
                for j, src in enumerate((wg_hbm, wu_hbm, wd_hbm))]

    @pl.when(b == 0)
    def _():
        slot_ref[0] = 0
        for d in fetch(e, 0):
            d.start()

    @pl.when((b < nused) & ((b == 0) | (e != bexp_ref[jnp.maximum(b - 1, 0)])))
    def _():
        slot = slot_ref[0]
        for d in fetch(e, slot):
            d.wait()
        for j in range(3):
            wbf[j] = wf32[slot, j].astype(BF16)
        nxt = lax.while_loop(lambda j: (j < nused) & (bexp_ref[jnp.minimum(j, nused - 1)] == e),
                             lambda j: j + 1, b + 1)

        @pl.when(nxt < nused)
        def _():
            for d in fetch(bexp_ref[nxt], 1 - slot):
                d.start()
        slot_ref[0] = 1 - slot

    @pl.when(b < nused)
    def _():
        rows = x_ref[...].reshape(-1, XS_W)
        x = rows[:, 0:D_MODEL].astype(BF16)
        aux = rows[:, D_MODEL:XS_W]
        eid = bexp_ref[b].astype(F32)
        row_w = jnp.zeros((x.shape[0], 1), F32)
        for k in range(TOP_K):
            w_k = aux[:, AUX_W_HI + k:AUX_W_HI + k + 1] + aux[:, AUX_W_LO + k:AUX_W_LO + k + 1]
            row_w = row_w + jnp.where(aux[:, AUX_EID + k:AUX_EID + k + 1] == eid, w_k, 0.0)

        def proj(w_ref, b_ref):
            return jnp.dot(x, w_ref[...], preferred_element_type=F32) + b_ref[0]

        a = jnp.minimum(proj(wgb, bg_ref), SWIGLU_LIMIT)
        u = jnp.clip(proj(wub, bu_ref), -SWIGLU_LIMIT, SWIGLU_LIMIT)
        hid = (a * jax.nn.sigmoid(SWIGLU_ALPHA * a) * (u + 1.0)).astype(BF16)
        y = (jnp.dot(hid, wdb[...], preferred_element_type=F32) + bd_ref[0]) * row_w
        y_ref[...] = y.reshape(y_ref.shape)

    @pl.when(b >= nused)
    def _():
        y_ref[...] = jnp.zeros_like(y_ref)


def _ffn_call(bexp, nused, xs_sorted, w_gate, b_gate, w_up, b_up, w_down, b_down):
    bm = BM_MOE
    bg = bm // SEG_ALIGN
    n_blocks = xs_sorted.shape[0] // bg
    assert D_FF == D_MODEL, "the three expert matrices share one staging shape"
    bspec = lambda c: pl.BlockSpec((1, 1, c), lambda b, be, nu: (be[b], 0, 0))
    whole = pl.BlockSpec(memory_space=pl.ANY)
    gs = pltpu.PrefetchScalarGridSpec(
        num_scalar_prefetch=2, grid=(n_blocks,),
        in_specs=[pl.BlockSpec((bg, SEG_ALIGN, XS_W), lambda b, be, nu: (jnp.minimum(b, nu[0] - 1), 0, 0)),
                  whole, bspec(D_FF), whole, bspec(D_FF), whole, bspec(D_MODEL)],
        out_specs=pl.BlockSpec((bg, SEG_ALIGN, D_MODEL), lambda b, be, nu: (b, 0, 0)),
        scratch_shapes=[pltpu.VMEM((2, 3, D_MODEL, D_FF), F32), pltpu.VMEM((3, D_MODEL, D_FF), BF16),
                        pltpu.SemaphoreType.DMA((2,)), pltpu.SMEM((1,), jnp.int32)])
    return pl.pallas_call(
        _ffn_kernel, grid_spec=gs, name="expert_ffn",
        out_shape=jax.ShapeDtypeStruct((n_blocks * bg, SEG_ALIGN, D_MODEL), F32),
        compiler_params=_params())(bexp, nused, xs_sorted, w_gate, b_gate, w_up, b_up, w_down, b_down)


def _combine_kernel(nt, desc_ref, descn_ref, kpos_ref, y_hbm, h_ref, gf_ref, out_ref, ybuf, sems):
    i = pl.program_id(0)
    tc = h_ref.shape[0]
    cur = i % 2

    def copy(b, dst, src, rows):
        groups = rows // SEG_ALIGN
        return pltpu.make_async_copy(y_hbm.at[pl.ds(lax.shift_right_logical(dst, LOG2_SEG), groups)],
                                     ybuf.at[b, pl.ds(lax.shift_right_logical(src, LOG2_SEG), groups)], sems.at[b])

    def fetch(b, ref):
        _for_each_copy(lambda ci: ref[0, DESC_COUNT, ci],
                       lambda row, j, rows: copy(b, ref[0, row, j], ref[0, row + 1, j], rows).start())

    @pl.when(i == 0)
    def _():
        ybuf[...] = jnp.zeros_like(ybuf)
        fetch(0, desc_ref)

    @pl.when(i + 1 < nt)
    def _():
        fetch(1 - cur, descn_ref)

    _for_each_copy(lambda ci: desc_ref[0, DESC_COUNT, ci], lambda row, j, rows: copy(cur, 0, 0, rows).wait())

    kpt = jnp.transpose(kpos_ref[0].astype(F32))
    kp = [kpt[:, k:k + 1] for k in range(TOP_K)]
    cols = lax.broadcasted_iota(jnp.int32, (tc, K_ROWS), 1).astype(F32)
    pick = _slot_hits(kp, cols).astype(BF16)
    y = ybuf[cur].reshape(K_ROWS, -1).astype(BF16)
    out_ref[...] = _rms(h_ref[...] + jnp.dot(pick, y, preferred_element_type=F32), gf_ref[...])


def _combine_call(desc, kpos, y, h2, g_final, tile0, n_tiles):
    tc = TC_COMB
    last = tile0 + n_tiles - 1
    return pl.pallas_call(
        functools.partial(_combine_kernel, n_tiles), grid=(n_tiles,), name="moe_combine",
        in_specs=[pl.BlockSpec((1, DESC_ROWS, LANES), lambda i: (tile0 + i, 0, 0), memory_space=pltpu.SMEM),
                  pl.BlockSpec((1, DESC_ROWS, LANES), lambda i: (jnp.minimum(tile0 + i + 1, last), 0, 0),
                               memory_space=pltpu.SMEM),
                  pl.BlockSpec((1, SUBLANES, tc), lambda i: (tile0 + i, 0, 0)),
                  pl.BlockSpec(memory_space=pl.ANY),
                  pl.BlockSpec((tc, D_MODEL), lambda i: (tile0 + i, 0)),
                  _const_spec((1, D_MODEL))],
        out_specs=pl.BlockSpec((tc, D_MODEL), lambda i: (i, 0)),
        out_shape=jax.ShapeDtypeStruct((n_tiles * tc, D_MODEL), F32),
        scratch_shapes=[pltpu.VMEM((2, K_ROWS // SEG_ALIGN, SEG_ALIGN, D_MODEL), F32),
                        pltpu.SemaphoreType.DMA((2,))],
        compiler_params=_params())(desc, desc, kpos, y, h2, g_final)


def _forward(xs, mems, g_mix, w_in, sink, conv_w, w_attn_br, w_conv_br, w_mix_out, g_cross, g_mem,
             w_xq, w_xkv, w_xo, g_moe, w_router, b_router, w_gate, b_gate, w_up, b_up, w_down,
             b_down, g_final):
    groups = [(x.shape[0], x.shape[1]) for x in xs]
    for _, s in groups:
        assert s % max(TM_QKV, TQ_ATTN, TM_MIX, TM_CROSS, TC_COMB) == 0
    assert len(xs) == 2, "two request groups"
    xa, xb = (x.reshape(-1, D_MODEL) for x in xs)
    n = xa.shape[0] + xb.shape[0]
    tables = _rope_tables(max(s for _, s in groups))
    assert w_in.shape[0] == 1, "single-layer trunk: the final norm is fused into the combine kernel"
    l = 0
    row2 = lambda v: v.reshape(1, -1)
    w_in_b = w_in[l].astype(BF16)
    perm = np.array([(kv * GROUP + g) * HEAD_DIM + d
                     for g in range(GROUP) for kv in range(N_KV_HEADS) for d in range(HEAD_DIM)])
    w_qkv = jnp.concatenate([w_in_b[:, :ATTN_W][:, perm], w_in_b[:, ATTN_W:QKV_W]], axis=1)
    q, kv = _qkv_call(xa, xb, row2(g_mix[l]), w_qkv, tables, groups)
    o = _attn_call(q, kv, sink[l], groups)
    h = _mixer_call(xa, xb, o, row2(g_mix[l]), w_in_b[:, QKV_W:], conv_w[l],
                    w_attn_br[l].astype(BF16)[perm, :], w_conv_br[l].astype(BF16),
                    w_mix_out[l].astype(BF16), groups)
    memkv = _memkv_call(mems[0], mems[1], row2(g_mem[l]), w_xkv[l].astype(BF16))
    wr = w_router[l]
    wr_hi = wr.astype(BF16)
    wr_lo = (wr - wr_hi.astype(F32)).astype(BF16)
    w_r2t = jnp.concatenate([wr_hi.T, wr_lo.T], axis=0)
    h2, xs_rows, idx, tot = _cross_call(h, memkv, row2(g_cross[l]), w_xq[l].astype(BF16),
                                        w_xo[l].astype(BF16), row2(g_moe[l]), w_r2t,
                                        b_router[l].reshape(-1, 1), groups)
    seg_rows = TOP_K * n + (n // TC_COMB) * N_EXPERTS * (SEG_ALIGN - 1)
    n_blocks = -(-seg_rows // BM_MOE) + N_EXPERTS
    kpos, desc, meta = _plan_call(idx, tot, n_blocks)
    xs_sorted = _dispatch_call(meta[1], desc, kpos, xs_rows, n_blocks)
    y = _ffn_call(meta[0], meta[2, 0:1], xs_sorted, w_gate[l], b_gate[l][:, None, :],
                  w_up[l], b_up[l][:, None, :], w_down[l], b_down[l][:, None, :])
    tc = TC_COMB
    outs, t0 = [], 0
    for nb, s in groups:
        nt = nb * s // tc
        outs.append(_combine_call(desc, kpos, y, h2, row2(g_final), t0, nt))
        t0 += nt
    return tuple(o.reshape(x.shape) for o, x in zip(outs, xs))


def kernel(x_prompt, x_sample, mem_prompt, mem_sample, g_mix, w_in, sink, conv_w, w_attn_br, w_conv_br,
           w_mix_out, g_cross, g_mem, w_xq, w_xkv, w_xo, g_moe, w_router, b_router, w_gate, b_gate,
           w_up, b_up, w_down, b_down, g_final):
    return _forward([x_prompt, x_sample], [mem_prompt, mem_sample], g_mix, w_in, sink, conv_w,
                    w_attn_br, w_conv_br, w_mix_out, g_cross, g_mem, w_xq, w_xkv, w_xo, g_moe,
                    w_router, b_router, w_gate, b_gate, w_up, b_up, w_down, b_down, g_final)
```

```python
import functools

import numpy as np
import jax
import jax.numpy as jnp
from jax import lax
from jax.experimental import pallas as pl
from jax.experimental.pallas import tpu as pltpu

F32 = jnp.float32
BF16 = jnp.bfloat16

D_MODEL = 1024
N_HEADS = 8
N_KV_HEADS = 2
HEAD_DIM = 64
GROUP = N_HEADS // N_KV_HEADS
ATTN_W = N_HEADS * HEAD_DIM
KV_W = N_KV_HEADS * HEAD_DIM
QKV_W = ATTN_W + 2 * KV_W
WINDOW = 128
ROT_DIM = HEAD_DIM // 4
ROPE_THETA = 500000.0
CONV_W = D_MODEL // 2
REST_W = 3 * CONV_W + 2 * D_MODEL
N_MEM = 256
X_HEADS = 4
X_HEAD_DIM = D_MODEL // X_HEADS
N_EXPERTS = 32
TOP_K = 4
D_FF = D_MODEL
SWIGLU_ALPHA = 1.702
SWIGLU_LIMIT = 7.0
EPS = 1e-5

LANES = 128
SUBLANES = 8
KEY_BLOCK = WINDOW
TM_QKV = 1024
TQ_ATTN = 1024
TM_MIX = 512
TM_CROSS = 1024
BM_MOE = 512
TC_COMB = 256
NEG_BIG = -1e30
VMEM_LIMIT = 56 * 1024 * 1024


def _rms(x, g):
    var = jnp.mean(x * x, axis=-1, keepdims=True)
    return x * lax.rsqrt(var + EPS) * g


def _tile_meta(groups, tile):
    pos, first, last, bidx = [], [], [], []
    b0 = 0
    for nb, s in groups:
        per = s // tile
        for b in range(nb):
            for j in range(per):
                pos.append(j)
                first.append(int(j == 0))
                last.append(int(j == per - 1))
                bidx.append(b0 + b)
        b0 += nb
    return jnp.asarray(np.array([pos, first, last, bidx], dtype=np.int32))


def _const_spec(shape):
    nd = len(shape)
    return pl.BlockSpec(shape, lambda *_: (0,) * nd, pipeline_mode=pl.Buffered(1))


def _params(vmem=VMEM_LIMIT):
    return pltpu.CompilerParams(dimension_semantics=("arbitrary",), vmem_limit_bytes=vmem)


def _qkv_kernel(na, meta_ref, xa_ref, xb_ref, g_ref, w_ref, cos_ref, sa_ref, sb_ref, q_ref, kv_ref):
    del meta_ref
    x = jnp.where(pl.program_id(0) < na, xa_ref[...], xb_ref[...])
    xn = _rms(x, g_ref[...]).astype(BF16)
    proj = jnp.dot(xn, w_ref[...], preferred_element_type=F32)
    c, sa, sb = cos_ref[...], sa_ref[...], sb_ref[...]
    n_rot = (ATTN_W + KV_W) // LANES
    for gi in range(n_rot):
        p = proj[:, gi * LANES:(gi + 1) * LANES]
        r = p * c + pltpu.roll(p, LANES - ROT_DIM // 2, 1) * sa + pltpu.roll(p, ROT_DIM // 2, 1) * sb
        if gi < ATTN_W // LANES:
            q_ref[:, gi * LANES:(gi + 1) * LANES] = (r * (HEAD_DIM ** -0.5)).astype(BF16)
        else:
            kv_ref[:, 0:KV_W] = r.astype(BF16)
    kv_ref[:, KV_W:2 * KV_W] = proj[:, ATTN_W + KV_W:QKV_W].astype(BF16)


def _rope_tables(s_max):
    half = ROT_DIM // 2
    inv_freq = ROPE_THETA ** (-(jnp.arange(half, dtype=F32) * 2.0) / ROT_DIM)
    ang = jnp.arange(s_max, dtype=F32)[:, None] * inv_freq[None, :]
    cos, sin = jnp.cos(ang), jnp.sin(ang)
    d = np.arange(LANES) % HEAD_DIM
    j = d % half
    cos_l, sin_l = cos[:, j], sin[:, j]
    rot = jnp.asarray(d < ROT_DIM)[None, :]
    lo = jnp.asarray(d < half)[None, :]
    hi = jnp.asarray((d >= half) & (d < ROT_DIM))[None, :]
    c = jnp.where(rot, cos_l, 1.0)
    sa = jnp.where(lo, -sin_l, 0.0)
    sb = jnp.where(hi, sin_l, 0.0)
    return c, sa, sb


def _qkv_call(xa, xb, g_mix, w_qkv, tables, groups):
    tm = TM_QKV
    na = xa.shape[0] // tm
    n = xa.shape[0] + xb.shape[0]
    meta = _tile_meta(groups, tm)
    row = lambda i, m: (i, 0)
    tab = lambda i, m: (m[0, i], 0)
    gs = pltpu.PrefetchScalarGridSpec(
        num_scalar_prefetch=1, grid=(n // tm,),
        in_specs=[pl.BlockSpec((tm, D_MODEL), lambda i, m: (jnp.minimum(i, na - 1), 0)),
                  pl.BlockSpec((tm, D_MODEL), lambda i, m: (jnp.maximum(i - na, 0), 0)),
                  _const_spec((1, D_MODEL)),
                  _const_spec((D_MODEL, QKV_W)),
                  pl.BlockSpec((tm, LANES), tab), pl.BlockSpec((tm, LANES), tab),
                  pl.BlockSpec((tm, LANES), tab)],
        out_specs=[pl.BlockSpec((tm, ATTN_W), row), pl.BlockSpec((tm, 2 * KV_W), row)])
    return pl.pallas_call(
        functools.partial(_qkv_kernel, na), grid_spec=gs, name="qkv_rope",
        out_shape=[jax.ShapeDtypeStruct((n, ATTN_W), BF16), jax.ShapeDtypeStruct((n, 2 * KV_W), BF16)],
        compiler_params=_params())(meta, xa, xb, g_mix, w_qkv, *tables)


def _attn_kernel(meta_ref, sink_ref, q_ref, kvp_ref, kvm_ref, kvn_ref, o_ref, kcat_ref):
    i = pl.program_id(0)
    kb = KEY_BLOCK
    tq = q_ref.shape[0]
    kcat_ref[0:kb, :] = kvp_ref[...]
    kcat_ref[kb:kb + tq, :] = kvm_ref[...]
    kcat_ref[kb + tq:kb + tq + kb, :] = kvn_ref[...]
    r = lax.broadcasted_iota(jnp.int32, (kb, kb), 0)
    c = lax.broadcasted_iota(jnp.int32, (kb, kb), 1)
    band_prev = jnp.where(c >= r, 0.0, NEG_BIG)
    band_next = jnp.where(c <= r, 0.0, NEG_BIG)
    klane = lax.broadcasted_iota(jnp.int32, (3 * kb, KV_W), 1)
    olane = lax.broadcasted_iota(jnp.int32, (GROUP * kb, KV_W), 1)
    nsub = tq // kb
    for s in range(nsub):
        pen_prev = jnp.where(meta_ref[1, i * nsub + s] == 1, NEG_BIG, 0.0)
        pen_next = jnp.where(meta_ref[2, i * nsub + s] == 1, NEG_BIG, 0.0)
        bias_prev = jnp.concatenate([band_prev + pen_prev] * GROUP, axis=0)
        bias_next = jnp.concatenate([band_next + pen_next] * GROUP, axis=0)
        kw = kcat_ref[s * kb:(s + 3) * kb, :]
        kc = kw[:, 0:KV_W]
        vc = jnp.concatenate([kw[:, KV_W:2 * KV_W], jnp.ones((3 * kb, KV_W), BF16)], axis=1)
        q4 = jnp.concatenate([q_ref[s * kb:(s + 1) * kb, g * KV_W:(g + 1) * KV_W] for g in range(GROUP)],
                             axis=0)
        res = []
        for kh in range(N_KV_HEADS):
            own = (klane >= kh * HEAD_DIM) & (klane < (kh + 1) * HEAD_DIM)
            kk = jnp.where(own, kc, jnp.zeros_like(kc))
            sc = lax.dot_general(q4, kk, (((1,), (1,)), ((), ())), preferred_element_type=F32)
            sc = jnp.concatenate([sc[:, 0:kb] + bias_prev, sc[:, kb:2 * kb], sc[:, 2 * kb:3 * kb] + bias_next],
                                 axis=1)
            snk = jnp.concatenate([jnp.full((kb, 1), sink_ref[kh * GROUP + g], F32) for g in range(GROUP)],
                                  axis=0)
            m = jnp.maximum(jnp.max(sc, axis=-1, keepdims=True), snk)
            p = jnp.exp(sc - m).astype(BF16)
            pv = jnp.dot(p, vc, preferred_element_type=F32)
            denom = pv[:, KV_W:KV_W + 1] + jnp.exp(snk - m)
            res.append(pv[:, 0:KV_W] / denom)
        out = res[0]
        for kh in range(1, N_KV_HEADS):
            out = jnp.where(olane >= kh * HEAD_DIM, res[kh], out)
        out = out.astype(BF16)
        for g in range(GROUP):
            o_ref[s * kb:(s + 1) * kb, g * KV_W:(g + 1) * KV_W] = out[g * kb:(g + 1) * kb, :]


def _attn_call(q, kv, sink, groups):
    n = q.shape[0]
    tq, kb = TQ_ATTN, KEY_BLOCK
    per = tq // kb
    nkb = n // kb
    meta = _tile_meta(groups, kb)
    gs = pltpu.PrefetchScalarGridSpec(
        num_scalar_prefetch=1, grid=(n // tq,),
        in_specs=[pl.BlockSpec(memory_space=pltpu.SMEM),
                  pl.BlockSpec((tq, ATTN_W), lambda i, m: (i, 0)),
                  pl.BlockSpec((kb, 2 * KV_W), lambda i, m: (jnp.maximum(i * per - 1, 0), 0)),
                  pl.BlockSpec((tq, 2 * KV_W), lambda i, m: (i, 0)),
                  pl.BlockSpec((kb, 2 * KV_W), lambda i, m: (jnp.minimum((i + 1) * per, nkb - 1), 0))],
        out_specs=pl.BlockSpec((tq, ATTN_W), lambda i, m: (i, 0)),
        scratch_shapes=[pltpu.VMEM((tq + 2 * kb, 2 * KV_W), BF16)])
    return pl.pallas_call(
        _attn_kernel, grid_spec=gs, name="window_attn",
        out_shape=jax.ShapeDtypeStruct((n, ATTN_W), BF16),
        compiler_params=_params())(meta, sink, q, kv, kv, kv)


def _mixer_kernel(na, meta_ref, xa_ref, xpa_ref, xna_ref, xb_ref, xpb_ref, xnb_ref, o_ref, g_ref, w_ref,
                  cw_ref, wa_ref, wc_ref, wm_ref, h_ref):
    i = pl.program_id(0)
    tm = xa_ref.shape[0]
    g = g_ref[...]
    in_a = i < na
    x = jnp.where(in_a, xa_ref[...], xb_ref[...])
    xh = jnp.concatenate([jnp.where(in_a, xpa_ref[...], xpb_ref[...]),
                          jnp.where(in_a, xna_ref[...], xnb_ref[...])], axis=0)
    xb = jnp.concatenate([_rms(x, g).astype(BF16), _rms(xh, g).astype(BF16)], axis=0)
    proj_all = jnp.dot(xb, w_ref[...], preferred_element_type=F32)
    proj = proj_all[0:tm, :]
    cb = proj[:, 0:CONV_W]
    u = proj[:, CONV_W:2 * CONV_W] * proj[:, 2 * CONV_W:3 * CONV_W]
    uh = proj_all[tm:tm + 2 * SUBLANES, CONV_W:2 * CONV_W] * proj_all[tm:tm + 2 * SUBLANES, 2 * CONV_W:3 * CONV_W]
    first = meta_ref[1, i]
    last = meta_ref[2, i]
    u_prev = jnp.where(first == 1, 0.0, uh[SUBLANES - 1:SUBLANES, :])
    u_next = jnp.where(last == 1, 0.0, uh[SUBLANES:SUBLANES + 1, :])
    row = lax.broadcasted_iota(jnp.int32, (tm, 1), 0)
    up = jnp.where(row == 0, u_prev, pltpu.roll(u, 1, 0))
    dn = jnp.where(row == tm - 1, u_next, pltpu.roll(u, tm - 1, 0))
    cw = cw_ref[...]
    y = up * cw[0:1, :] + u * cw[1:2, :] + dn * cw[2:3, :]
    conv = (cb * y).astype(BF16)
    conv_br = jnp.dot(conv, wc_ref[...], preferred_element_type=F32)
    attn_br = jnp.dot(o_ref[...], wa_ref[...], preferred_element_type=F32)
    g0 = jax.nn.sigmoid(proj[:, 3 * CONV_W:3 * CONV_W + D_MODEL])
    g1 = jax.nn.sigmoid(proj[:, 3 * CONV_W + D_MODEL:REST_W])
    merged = (g0 * attn_br + g1 * conv_br).astype(BF16)
    h_ref[...] = x + jnp.dot(merged, wm_ref[...], preferred_element_type=F32)


def _mixer_call(xa, xb, o, g_mix, w_rest, conv_w, w_attn_br, w_conv_br, w_mix_out, groups):
    tm = TM_MIX
    per = tm // SUBLANES
    na, nb = xa.shape[0] // tm, xb.shape[0] // tm
    n = xa.shape[0] + xb.shape[0]
    meta = _tile_meta(groups, tm)

    def x_specs(tile_of, n_tiles):
        last8 = n_tiles * per - 1
        return [pl.BlockSpec((tm, D_MODEL), lambda i, m: (jnp.clip(tile_of(i), 0, n_tiles - 1), 0)),
                pl.BlockSpec((SUBLANES, D_MODEL), lambda i, m: (jnp.clip(tile_of(i) * per - 1, 0, last8), 0)),
                pl.BlockSpec((SUBLANES, D_MODEL), lambda i, m: (jnp.clip((tile_of(i) + 1) * per, 0, last8), 0))]

    gs = pltpu.PrefetchScalarGridSpec(
        num_scalar_prefetch=1, grid=(n // tm,),
        in_specs=x_specs(lambda i: i, na) + x_specs(lambda i: i - na, nb) + [
                  pl.BlockSpec((tm, ATTN_W), lambda i, m: (i, 0)),
                  _const_spec((1, D_MODEL)),
                  _const_spec((D_MODEL, REST_W)),
                  _const_spec((3, CONV_W)),
                  _const_spec((ATTN_W, D_MODEL)),
                  _const_spec((CONV_W, D_MODEL)),
                  _const_spec((D_MODEL, D_MODEL))],
        out_specs=pl.BlockSpec((tm, D_MODEL), lambda i, m: (i, 0)))
    return pl.pallas_call(
        functools.partial(_mixer_kernel, na), grid_spec=gs, name="mixer",
        out_shape=jax.ShapeDtypeStruct((n, D_MODEL), F32),
        compiler_params=_params())(meta, xa, xa, xa, xb, xb, xb, o, g_mix, w_rest, conv_w, w_attn_br,
                                   w_conv_br, w_mix_out)


def _memkv_kernel(na, mema_ref, memb_ref, g_ref, w_ref, kv_ref):
    mem = jnp.where(pl.program_id(0) < na, mema_ref[0], memb_ref[0])
    mn = _rms(mem, g_ref[...]).astype(BF16)
    kv_ref[0] = jnp.dot(mn, w_ref[...], preferred_element_type=F32).astype(BF16)


def _memkv_call(mem_a, mem_b, g_mem, w_xkv):
    na, nb = mem_a.shape[0], mem_b.shape[0]
    return pl.pallas_call(
        functools.partial(_memkv_kernel, na), grid=(na + nb,), name="mem_kv",
        in_specs=[pl.BlockSpec((1, N_MEM, D_MODEL), lambda b: (jnp.minimum(b, na - 1), 0, 0)),
                  pl.BlockSpec((1, N_MEM, D_MODEL), lambda b: (jnp.maximum(b - na, 0), 0, 0)),
                  _const_spec((1, D_MODEL)),
                  _const_spec((D_MODEL, 2 * D_MODEL))],
        out_specs=pl.BlockSpec((1, N_MEM, 2 * D_MODEL), lambda b: (b, 0, 0)),
        out_shape=jax.ShapeDtypeStruct((na + nb, N_MEM, 2 * D_MODEL), BF16),
        compiler_params=_params())(mem_a, mem_b, g_mem, w_xkv)


def _cross_kernel(meta_ref, h_ref, kv_ref, gc_ref, wq_ref, wo_ref, gm_ref, wr_ref, br_ref,
                  h2_ref, xs_ref, idx_ref, tot_ref):
    del meta_ref
    tm = h_ref.shape[0]
    h = h_ref[...]
    hn = _rms(h, gc_ref[...]).astype(BF16)
    q = (jnp.dot(hn, wq_ref[...], preferred_element_type=F32) * (X_HEAD_DIM ** -0.5)).astype(BF16)
    outs = []
    for hd in range(X_HEADS):
        qh = q[:, hd * X_HEAD_DIM:(hd + 1) * X_HEAD_DIM]
        kh = kv_ref[0, :, hd * X_HEAD_DIM:(hd + 1) * X_HEAD_DIM]
        vh = kv_ref[0, :, D_MODEL + hd * X_HEAD_DIM:D_MODEL + (hd + 1) * X_HEAD_DIM]
        s = lax.dot_general(qh, kh, (((1,), (1,)), ((), ())), preferred_element_type=F32)
        m = jnp.max(s, axis=-1, keepdims=True)
        p = jnp.exp(s - m)
        p = (p / jnp.sum(p, axis=-1, keepdims=True)).astype(BF16)
        outs.append(jnp.dot(p, vh, preferred_element_type=F32).astype(BF16))
    o = jnp.concatenate(outs, axis=1)
    h2 = h + jnp.dot(o, wo_ref[...], preferred_element_type=F32)
    h2_ref[...] = h2

    hn3 = _rms(h2, gm_ref[...])
    hi = hn3.astype(BF16)
    hi32 = hi.astype(F32)
    lo = (hn3 - hi32).astype(BF16)
    xs_ref[:, 0:D_MODEL] = hn3

    nt = (((1,), (1,)), ((), ()))
    r1 = lax.dot_general(wr_ref[...], hi, nt, preferred_element_type=F32)
    r2 = lax.dot_general(wr_ref[...], lo, nt, preferred_element_type=F32)
    e = N_EXPERTS
    logits = ((r2[e:2 * e] + r2[0:e]) + r1[e:2 * e]) + r1[0:e] + br_ref[...]
    eio = lax.broadcasted_iota(jnp.int32, (e, tm), 0)
    cur = logits
    vals, sels = [], []
    for _ in range(TOP_K):
        mx = jnp.max(cur, axis=0, keepdims=True)
        sel = jnp.min(jnp.where(cur == mx, eio, e), axis=0, keepdims=True)
        vals.append(mx)
        sels.append(sel)
        cur = jnp.where(eio == sel, -jnp.inf, cur)
    ex = [jnp.exp(v - vals[0]) for v in vals]
    tot = ex[0] + ex[1] + ex[2] + ex[3]
    idx_ref[...] = jnp.concatenate(sels, axis=0)
    assign = jnp.where(eio == sels[0], 1.0, 0.0)
    for k in range(1, TOP_K):
        assign = assign + jnp.where(eio == sels[k], 1.0, 0.0)
    segs = jnp.zeros((e, 1), jnp.int32)
    for j in range(tm // TC_COMB):
        cnt = jnp.sum(assign[:, j * TC_COMB:(j + 1) * TC_COMB], axis=1, keepdims=True).astype(jnp.int32)
        segs = segs + lax.shift_left(lax.shift_right_logical(cnt + (SEG_ALIGN - 1), LOG2_SEG), LOG2_SEG)

    @pl.when(pl.program_id(0) == 0)
    def _():
        tot_ref[...] = jnp.zeros_like(tot_ref)
    tot_ref[...] += jnp.broadcast_to(segs, tot_ref.shape)
    gws = [x / tot for x in ex]
    w_hi = [g.astype(BF16).astype(F32) for g in gws]
    w_lo = [(g - h).astype(BF16).astype(F32) for g, h in zip(gws, w_hi)]
    wt = jnp.transpose(jnp.concatenate(w_hi + w_lo, axis=0))
    et = jnp.transpose(jnp.concatenate([s.astype(F32) for s in sels]
                                       + [jnp.zeros((SUBLANES - TOP_K, tm), F32)], axis=0))
    xs_ref[:, D_MODEL:XS_W] = jnp.concatenate(
        [wt, et, jnp.zeros((tm, LANES - 2 * SUBLANES), F32)], axis=1)


def _cross_call(h1, memkv, g_cross, w_xq, w_xo, g_moe, w_r2t, b_router, groups):
    n = h1.shape[0]
    tm = TM_CROSS
    meta = _tile_meta(groups, tm)
    row = lambda i, m: (i, 0)
    col = lambda i, m: (0, i)
    gs = pltpu.PrefetchScalarGridSpec(
        num_scalar_prefetch=1, grid=(n // tm,),
        in_specs=[pl.BlockSpec((tm, D_MODEL), row),
                  pl.BlockSpec((1, N_MEM, 2 * D_MODEL), lambda i, m: (m[3, i], 0, 0)),
                  _const_spec((1, D_MODEL)),
                  _const_spec((D_MODEL, D_MODEL)),
                  _const_spec((D_MODEL, D_MODEL)),
                  _const_spec((1, D_MODEL)),
                  _const_spec((2 * N_EXPERTS, D_MODEL)),
                  _const_spec((N_EXPERTS, 1))],
        out_specs=[pl.BlockSpec((tm, D_MODEL), row),
                   pl.BlockSpec((tm, XS_W), row),
                   pl.BlockSpec((TOP_K, tm), col),
                   pl.BlockSpec((N_EXPERTS, LANES), lambda i, m: (0, 0))])
    return pl.pallas_call(
        _cross_kernel, grid_spec=gs, name="cross_router",
        out_shape=[jax.ShapeDtypeStruct((n, D_MODEL), F32),
                   jax.ShapeDtypeStruct((n, XS_W), F32),
                   jax.ShapeDtypeStruct((TOP_K, n), jnp.int32),
                   jax.ShapeDtypeStruct((N_EXPERTS, LANES), jnp.int32)],
        compiler_params=_params())(meta, h1, memkv, g_cross, w_xq, w_xo, g_moe, w_r2t, b_router)


LOG2_BM = BM_MOE.bit_length() - 1
assert 1 << LOG2_BM == BM_MOE
SEG_ALIGN = 4
LOG2_SEG = SEG_ALIGN.bit_length() - 1
COPY_ROWS = tuple(SEG_ALIGN << j for j in (3, 2, 1, 0))
K_ROWS = 1152
M_CHUNK = 576
DESC_ROWS = 2 * SUBLANES
PLAN_TILES = 4
assert 1 << LOG2_SEG == SEG_ALIGN and K_ROWS % M_CHUNK == 0 and K_ROWS // COPY_ROWS[0] <= LANES
assert K_ROWS >= TOP_K * TC_COMB + N_EXPERTS * (SEG_ALIGN - 1)
DESC_COUNT = 2 * len(COPY_ROWS)
assert DESC_COUNT < DESC_ROWS and all(a == 2 * b for a, b in zip(COPY_ROWS, COPY_ROWS[1:]))
AUX_W_HI, AUX_W_LO, AUX_EID = 0, TOP_K, 2 * TOP_K
XS_W = D_MODEL + LANES


def _sublane_scan(x, n):
    row = lax.broadcasted_iota(jnp.int32, x.shape, 0)
    sft = 1
    while sft < n:
        x = x + jnp.where(row >= sft, pltpu.roll(x, sft, 0), 0)
        sft *= 2
    return x


def _owner(ends, lane):
    ebl = lax.broadcasted_iota(jnp.int32, ends.shape, 0)
    owner = jnp.sum(jnp.where(ends <= lane, 1.0, 0.0), axis=0, keepdims=True).astype(jnp.int32)
    return ebl == owner


def _pick(sel, val):
    return jnp.sum(jnp.where(sel, val.astype(F32), 0.0), axis=0, keepdims=True).astype(jnp.int32)


def _plan_kernel(idx_ref, tot_ref, kpos_ref, desc_ref, meta_ref, tri_ref, run_ref):
    i = pl.program_id(0)
    e, tp = N_EXPERTS, TC_COMB
    nb_pad = meta_ref.shape[1]

    @pl.when(i == 0)
    def _():
        r = lax.broadcasted_iota(jnp.int32, (tp, tp), 0)
        c = lax.broadcasted_iota(jnp.int32, (tp, tp), 1)
        tri_ref[...] = jnp.where(r < c, 1.0, 0.0).astype(BF16)
        tot = tot_ref[...]
        pad = lax.shift_left(lax.shift_right_logical(tot + (BM_MOE - 1), LOG2_BM), LOG2_BM)
        pad_end = _sublane_scan(pad, e)
        pad_start = pad_end - pad
        run_ref[...] = pad_start
        bpos = lax.broadcasted_iota(jnp.int32, (e, nb_pad), 1) * BM_MOE
        ebl = lax.broadcasted_iota(jnp.int32, (e, nb_pad), 0)
        done = jnp.where(pad_end[:, 0:1] <= bpos, 1.0, 0.0)
        bexp = jnp.minimum(jnp.sum(done, axis=0, keepdims=True).astype(jnp.int32), e - 1)
        row_end = (pad_start + tot)[:, 0:1].astype(F32)
        rend_b = jnp.sum(jnp.where(ebl == bexp, row_end, 0.0), axis=0, keepdims=True).astype(jnp.int32)
        nvalid = jnp.clip(rend_b - bpos[0:1, :], 0, BM_MOE)
        total = jnp.sum(jnp.where(ebl == e - 1, pad_end[:, 0:1].astype(F32), 0.0), axis=0, keepdims=True)
        nused = lax.shift_right_logical(total.astype(jnp.int32), LOG2_BM)
        meta_ref[...] = jnp.concatenate(
            [bexp, nvalid, nused, jnp.zeros((SUBLANES - 3, nb_pad), jnp.int32)], axis=0)

    for j in range(PLAN_TILES):
        _plan_tile(j, idx_ref[:, j * tp:(j + 1) * tp], kpos_ref, desc_ref, tri_ref, run_ref)


def _plan_tile(j, idx, kpos_ref, desc_ref, tri_ref, run_ref):
    e, tp = N_EXPERTS, TC_COMB
    eio = lax.broadcasted_iota(jnp.int32, (e, tp), 0)
    onehots = [eio == idx[k:k + 1, :] for k in range(TOP_K)]
    s = jnp.where(onehots[0], 1.0, 0.0)
    for k in range(1, TOP_K):
        s = s + jnp.where(onehots[k], 1.0, 0.0)
    cnt_t = jnp.sum(s, axis=1, keepdims=True).astype(jnp.int32)
    cnt = jnp.broadcast_to(cnt_t, (e, LANES))
    seg = lax.shift_left(lax.shift_right_logical(cnt + (SEG_ALIGN - 1), LOG2_SEG), LOG2_SEG)
    before = jnp.dot(s.astype(BF16), tri_ref[...], preferred_element_type=F32)
    kbase = _sublane_scan(seg, e) - seg
    pos = kbase[:, 0:1].astype(F32) + before
    kp = [jnp.sum(jnp.where(onehots[k], pos, 0.0), axis=0, keepdims=True).astype(jnp.int32)
          for k in range(TOP_K)]
    kpos_ref[j] = jnp.concatenate(kp + [jnp.zeros((SUBLANES - TOP_K, tp), jnp.int32)], axis=0)

    start = run_ref[...]
    lane = lax.broadcasted_iota(jnp.int32, (e, LANES), 1)
    lane1 = lane[0:1, :]
    lists = []
    counts = jnp.zeros((1, LANES), jnp.int32)
    done = jnp.zeros_like(seg)
    for ci, rows in enumerate(COPY_ROWS):
        n = lax.shift_right_logical(seg - done, rows.bit_length() - 1)
        end = _sublane_scan(n, e)
        sel = _owner(end, lane)
        off = _pick(sel, done) + (lane1 - _pick(sel, end - n)) * rows
        n_c = end[e - 1:e, :]
        ok = lane1 < n_c
        lists += [jnp.where(ok, _pick(sel, start) + off, 0), jnp.where(ok, _pick(sel, kbase) + off, 0)]
        counts = jnp.where(lane1 == ci, n_c, counts)
        done = done + n * rows
    desc_ref[j] = jnp.concatenate(
        lists + [counts, jnp.zeros((DESC_ROWS - DESC_COUNT - 1, LANES), jnp.int32)], axis=0)
    run_ref[...] = run_ref[...] + seg


def _plan_call(idx, tot, n_blocks):
    n = idx.shape[1]
    tp = TC_COMB
    n_tiles = n // tp
    assert n_tiles % PLAN_TILES == 0
    nb_pad = -(-n_blocks // LANES) * LANES
    tile = lambda i: (i, 0, 0)
    return pl.pallas_call(
        _plan_kernel, grid=(n_tiles // PLAN_TILES,), name="route_plan",
        in_specs=[pl.BlockSpec((TOP_K, PLAN_TILES * tp), lambda i: (0, i)), _const_spec((N_EXPERTS, LANES))],
        out_specs=[pl.BlockSpec((PLAN_TILES, SUBLANES, tp), tile),
                   pl.BlockSpec((PLAN_TILES, DESC_ROWS, LANES), tile),
                   pl.BlockSpec((SUBLANES, nb_pad), lambda i: (0, 0))],
        out_shape=[jax.ShapeDtypeStruct((n_tiles, SUBLANES, tp), jnp.int32),
                   jax.ShapeDtypeStruct((n_tiles, DESC_ROWS, LANES), jnp.int32),
                   jax.ShapeDtypeStruct((SUBLANES, nb_pad), jnp.int32)],
        scratch_shapes=[pltpu.VMEM((tp, tp), BF16), pltpu.VMEM((N_EXPERTS, LANES), jnp.int32)],
        compiler_params=_params())(idx, tot)


def _slot_hits(kp, iota):
    hit = jnp.where(iota == kp[0], 1.0, 0.0)
    for k in range(1, TOP_K):
        hit = hit + jnp.where(iota == kp[k], 1.0, 0.0)
    return hit


def _for_each_copy(count, fn):
    for ci, rows in enumerate(COPY_ROWS):
        n = count(ci)

        def pair(j, carry, ci=ci, rows=rows):
            fn(2 * ci, 2 * j, rows, 0)
            fn(2 * ci, 2 * j + 1, rows, 1)
            return carry
        lax.fori_loop(0, lax.shift_right_logical(n, 1), pair, 0)

        @pl.when((n & 1) == 1)
        def _(ci=ci, rows=rows, n=n):
            fn(2 * ci, n - 1, rows, 0)


def _dispatch_kernel(n_tiles, n_blocks, nvalid_ref, desc_ref, kpos_ref, xs_ref, out_hbm, zero_ref, buf,
                     sems, zsem, issued_ref):
    i = pl.program_id(0)
    td = xs_ref.shape[0]
    bg = zero_ref.shape[0]
    cur = i % 2

    def copy(b, dst, src, rows):
        groups = rows // SEG_ALIGN
        return pltpu.make_async_copy(buf.at[b, pl.ds(lax.shift_right_logical(src, LOG2_SEG), groups)],
                                     out_hbm.at[pl.ds(lax.shift_right_logical(dst, LOG2_SEG), groups)], sems.at[b])

    def drain(b):
        _for_each_copy(lambda ci: issued_ref[b, ci], lambda row, j, rows, prio: copy(b, 0, 0, rows).wait())

    @pl.when(i == 0)
    def _():
        zero_ref[...] = jnp.zeros_like(zero_ref)

        def fill(b):
            return pltpu.make_async_copy(zero_ref, out_hbm.at[pl.ds(b * bg, bg)], zsem)

        def start(b, carry):
            @pl.when(nvalid_ref[b] < bg * SEG_ALIGN)
            def _():
                fill(b).start()
            return carry

        def wait(b, carry):
            @pl.when(nvalid_ref[b] < bg * SEG_ALIGN)
            def _():
                fill(b).wait()
            return carry
        lax.fori_loop(0, n_blocks, start, 0)
        lax.fori_loop(0, n_blocks, wait, 0)

    @pl.when(i >= 2)
    def _():
        drain(cur)

    kp = kpos_ref[0]
    x = xs_ref[...].astype(BF16)
    for c in range(K_ROWS // M_CHUNK):
        rows = lax.broadcasted_iota(jnp.int32, (M_CHUNK, td), 0) + c * M_CHUNK
        sel = _slot_hits([kp[k:k + 1, :] for k in range(TOP_K)], rows).astype(BF16)
        mg = M_CHUNK // SEG_ALIGN
        buf[cur, c * mg:(c + 1) * mg] = jnp.dot(sel, x, preferred_element_type=F32).reshape(mg, SEG_ALIGN, -1)

    _for_each_copy(lambda ci: desc_ref[0, DESC_COUNT, ci],
                   lambda row, j, rows, prio: copy(cur, desc_ref[0, row, j], desc_ref[0, row + 1, j],
                                                   rows).start(priority=prio))
    for ci in range(len(COPY_ROWS)):
        issued_ref[cur, ci] = desc_ref[0, DESC_COUNT, ci]

    @pl.when(i == n_tiles - 1)
    def _():
        drain(cur)

        @pl.when(i >= 1)
        def _():
            drain(1 - cur)


def _dispatch_call(nvalid, desc, kpos, xs, n_blocks):
    n, width = xs.shape
    td, bm = TC_COMB, BM_MOE
    n_tiles = n // td
    gs = pltpu.PrefetchScalarGridSpec(
        num_scalar_prefetch=1, grid=(n_tiles,),
        in_specs=[pl.BlockSpec((1, DESC_ROWS, LANES), lambda i, nv: (i, 0, 0), memory_space=pltpu.SMEM),
                  pl.BlockSpec((1, SUBLANES, td), lambda i, nv: (i, 0, 0)),
                  pl.BlockSpec((td, width), lambda i, nv: (i, 0))],
        out_specs=pl.BlockSpec(memory_space=pl.ANY),
        scratch_shapes=[pltpu.VMEM((bm // SEG_ALIGN, SEG_ALIGN, width), xs.dtype),
                        pltpu.VMEM((2, K_ROWS // SEG_ALIGN, SEG_ALIGN, width), xs.dtype),
                        pltpu.SemaphoreType.DMA((2,)), pltpu.SemaphoreType.DMA(()),
                        pltpu.SMEM((2, len(COPY_ROWS)), jnp.int32)])
    return pl.pallas_call(
        functools.partial(_dispatch_kernel, n_tiles, n_blocks), grid_spec=gs, name="moe_dispatch",
        out_shape=jax.ShapeDtypeStruct((n_blocks * bm // SEG_ALIGN, SEG_ALIGN, width), xs.dtype),
        compiler_params=_params())(nvalid, desc, kpos, xs)


def _ffn_kernel(bexp_ref, nused_ref, x_ref, wg_hbm, bg_ref, wu_hbm, bu_ref, wd_hbm, bd_ref,
                y_ref, wf32, wbf, wsems, slot_ref):
    b = pl.program_id(0)
    nused = nused_ref[0]
    e = bexp_ref[b]
    wgb, wub, wdb = wbf.at[0], wbf.at[1], wbf.at[2]

    def fetch(expert, slot):
        return [pltpu.make_async_copy(src.at[expert], wf32.at[slot, j], wsems.at[slot])
                for j, src in enumerate((wg_hbm, wu_hbm, wd_hbm))]

    @pl.when(b == 0)
    def _():
        slot_ref[0] = 0
        for d in fetch(e, 0):
            d.start()

    @pl.when((b < nused) & ((b == 0) | (e != bexp_ref[jnp.maximum(b - 1, 0)])))
    def _():
        slot = slot_ref[0]
        for d in fetch(e, slot):
            d.wait()
        for j in range(3):
            wbf[j] = wf32[slot, j].astype(BF16)
        nxt = lax.while_loop(lambda j: (j < nused) & (bexp_ref[jnp.minimum(j, nused - 1)] == e),
                             lambda j: j + 1, b + 1)

        @pl.when(nxt < nused)
        def _():
            for d in fetch(bexp_ref[nxt], 1 - slot):
                d.start()
        slot_ref[0] = 1 - slot

    @pl.when(b < nused)
    def _():
        rows = x_ref[...].reshape(-1, XS_W)
        x = rows[:, 0:D_MODEL].astype(BF16)
        aux = rows[:, D_MODEL:XS_W]
        eid = bexp_ref[b].astype(F32)
        row_w = jnp.zeros((x.shape[0], 1), F32)
        for k in range(TOP_K):
            w_k = aux[:, AUX_W_HI + k:AUX_W_HI + k + 1] + aux[:, AUX_W_LO + k:AUX_W_LO + k + 1]
            row_w = row_w + jnp.where(aux[:, AUX_EID + k:AUX_EID + k + 1] == eid, w_k, 0.0)

        def proj(w_ref, b_ref):
            return jnp.dot(x, w_ref[...], preferred_element_type=F32) + b_ref[0]

        a = jnp.minimum(proj(wgb, bg_ref), SWIGLU_LIMIT)
        u = jnp.clip(proj(wub, bu_ref), -SWIGLU_LIMIT, SWIGLU_LIMIT)
        hid = (a * jax.nn.sigmoid(SWIGLU_ALPHA * a) * (u + 1.0)).astype(BF16)
        y = (jnp.dot(hid, wdb[...], preferred_element_type=F32) + bd_ref[0]) * row_w
        y_ref[...] = y.reshape(y_ref.shape)

    @pl.when(b >= nused)
    def _():
        y_ref[...] = jnp.zeros_like(y_ref)


def _ffn_call(bexp, nused, xs_sorted, w_gate, b_gate, w_up, b_up, w_down, b_down):
    bm = BM_MOE
    bg = bm // SEG_ALIGN
    n_blocks = xs_sorted.shape[0] // bg
    assert D_FF == D_MODEL, "the three expert matrices share one staging shape"
    bspec = lambda c: pl.BlockSpec((1, 1, c), lambda b, be, nu: (be[b], 0, 0))
    whole = pl.BlockSpec(memory_space=pl.ANY)
    gs = pltpu.PrefetchScalarGridSpec(
        num_scalar_prefetch=2, grid=(n_blocks,),
        in_specs=[pl.BlockSpec((bg, SEG_ALIGN, XS_W), lambda b, be, nu: (jnp.minimum(b, nu[0] - 1), 0, 0)),
                  whole, bspec(D_FF), whole, bspec(D_FF), whole, bspec(D_MODEL)],
        out_specs=pl.BlockSpec((bg, SEG_ALIGN, D_MODEL), lambda b, be, nu: (b, 0, 0)),
        scratch_shapes=[pltpu.VMEM((2, 3, D_MODEL, D_FF), F32), pltpu.VMEM((3, D_MODEL, D_FF), BF16),
                        pltpu.SemaphoreType.DMA((2,)), pltpu.SMEM((1,), jnp.int32)])
    return pl.pallas_call(
        _ffn_kernel, grid_spec=gs, name="expert_ffn",
        out_shape=jax.ShapeDtypeStruct((n_blocks * bg, SEG_ALIGN, D_MODEL), F32),
        compiler_params=_params())(bexp, nused, xs_sorted, w_gate, b_gate, w_up, b_up, w_down, b_down)


def _combine_kernel(nt, desc_ref, descn_ref, kpos_ref, y_hbm, h_ref, gf_ref, out_ref, ybuf, sems):
    i = pl.program_id(0)
    tc = h_ref.shape[0]
    cur = i % 2

    def copy(b, dst, src, rows):
        groups = rows // SEG_ALIGN
        return pltpu.make_async_copy(y_hbm.at[pl.ds(lax.shift_right_logical(dst, LOG2_SEG), groups)],
                                     ybuf.at[b, pl.ds(lax.shift_right_logical(src, LOG2_SEG), groups)], sems.at[b])

    def fetch(b, ref):
        _for_each_copy(lambda ci: ref[0, DESC_COUNT, ci],
                       lambda row, j, rows, prio: copy(b, ref[0, row, j], ref[0, row + 1, j],
                                                       rows).start(priority=prio))

    @pl.when(i == 0)
    def _():
        ybuf[...] = jnp.zeros_like(ybuf)
        fetch(0, desc_ref)

    @pl.when(i + 1 < nt)
    def _():
        fetch(1 - cur, descn_ref)

    _for_each_copy(lambda ci: desc_ref[0, DESC_COUNT, ci],
                   lambda row, j, rows, prio: copy(cur, 0, 0, rows).wait())

    kpt = jnp.transpose(kpos_ref[0].astype(F32))
    kp = [kpt[:, k:k + 1] for k in range(TOP_K)]
    cols = lax.broadcasted_iota(jnp.int32, (tc, K_ROWS), 1).astype(F32)
    pick = _slot_hits(kp, cols).astype(BF16)
    y = ybuf[cur].reshape(K_ROWS, -1).astype(BF16)
    out_ref[...] = _rms(h_ref[...] + jnp.dot(pick, y, preferred_element_type=F32), gf_ref[...])


def _combine_call(desc, kpos, y, h2, g_final, tile0, n_tiles):
    tc = TC_COMB
    last = tile0 + n_tiles - 1
    return pl.pallas_call(
        functools.partial(_combine_kernel, n_tiles), grid=(n_tiles,), name="moe_combine",
        in_specs=[pl.BlockSpec((1, DESC_ROWS, LANES), lambda i: (tile0 + i, 0, 0), memory_space=pltpu.SMEM),
                  pl.BlockSpec((1, DESC_ROWS, LANES), lambda i: (jnp.minimum(tile0 + i + 1, last), 0, 0),
                               memory_space=pltpu.SMEM),
                  pl.BlockSpec((1, SUBLANES, tc), lambda i: (tile0 + i, 0, 0)),
                  pl.BlockSpec(memory_space=pl.ANY),
                  pl.BlockSpec((tc, D_MODEL), lambda i: (tile0 + i, 0)),
                  _const_spec((1, D_MODEL))],
        out_specs=pl.BlockSpec((tc, D_MODEL), lambda i: (i, 0)),
        out_shape=jax.ShapeDtypeStruct((n_tiles * tc, D_MODEL), F32),
        scratch_shapes=[pltpu.VMEM((2, K_ROWS // SEG_ALIGN, SEG_ALIGN, D_MODEL), F32),
                        pltpu.SemaphoreType.DMA((2,))],
        compiler_params=_params())(desc, desc, kpos, y, h2, g_final)


def _forward(xs, mems, g_mix, w_in, sink, conv_w, w_attn_br, w_conv_br, w_mix_out, g_cross, g_mem,
             w_xq, w_xkv, w_xo, g_moe, w_router, b_router, w_gate, b_gate, w_up, b_up, w_down,
             b_down, g_final):
    groups = [(x.shape[0], x.shape[1]) for x in xs]
    for _, s in groups:
        assert s % max(TM_QKV, TQ_ATTN, TM_MIX, TM_CROSS, TC_COMB) == 0
    assert len(xs) == 2, "two request groups"
    xa, xb = (x.reshape(-1, D_MODEL) for x in xs)
    n = xa.shape[0] + xb.shape[0]
    tables = _rope_tables(max(s for _, s in groups))
    assert w_in.shape[0] == 1, "single-layer trunk: the final norm is fused into the combine kernel"
    l = 0
    row2 = lambda v: v.reshape(1, -1)
    w_in_b = w_in[l].astype(BF16)
    perm = np.array([(kv * GROUP + g) * HEAD_DIM + d
                     for g in range(GROUP) for kv in range(N_KV_HEADS) for d in range(HEAD_DIM)])
    w_qkv = jnp.concatenate([w_in_b[:, :ATTN_W][:, perm], w_in_b[:, ATTN_W:QKV_W]], axis=1)
    q, kv = _qkv_call(xa, xb, row2(g_mix[l]), w_qkv, tables, groups)
    o = _attn_call(q, kv, sink[l], groups)
    h = _mixer_call(xa, xb, o, row2(g_mix[l]), w_in_b[:, QKV_W:], conv_w[l],
                    w_attn_br[l].astype(BF16)[perm, :], w_conv_br[l].astype(BF16),
                    w_mix_out[l].astype(BF16), groups)
    memkv = _memkv_call(mems[0], mems[1], row2(g_mem[l]), w_xkv[l].astype(BF16))
    wr = w_router[l]
    wr_hi = wr.astype(BF16)
    wr_lo = (wr - wr_hi.astype(F32)).astype(BF16)
    w_r2t = jnp.concatenate([wr_hi.T, wr_lo.T], axis=0)
    h2, xs_rows, idx, tot = _cross_call(h, memkv, row2(g_cross[l]), w_xq[l].astype(BF16),
                                        w_xo[l].astype(BF16), row2(g_moe[l]), w_r2t,
                                        b_router[l].reshape(-1, 1), groups)
    seg_rows = TOP_K * n + (n // TC_COMB) * N_EXPERTS * (SEG_ALIGN - 1)
    n_blocks = -(-seg_rows // BM_MOE) + N_EXPERTS
    kpos, desc, meta = _plan_call(idx, tot, n_blocks)
    xs_sorted = _dispatch_call(meta[1], desc, kpos, xs_rows, n_blocks)
    y = _ffn_call(meta[0], meta[2, 0:1], xs_sorted, w_gate[l], b_gate[l][:, None, :],
                  w_up[l], b_up[l][:, None, :], w_down[l], b_down[l][:, None, :])
    tc = TC_COMB
    outs, t0 = [], 0
    for nb, s in groups:
        nt = nb * s // tc
        outs.append(_combine_call(desc, kpos, y, h2, row2(g_final), t0, nt))
        t0 += nt
    return tuple(o.reshape(x.shape) for o, x in zip(outs, xs))


def kernel(x_prompt, x_sample, mem_prompt, mem_sample, g_mix, w_in, sink, conv_w, w_attn_br, w_conv_br,
           w_mix_out, g_cross, g_mem, w_xq, w_xkv, w_xo, g_moe, w_router, b_router, w_gate, b_gate,
           w_up, b_up, w_down, b_down, g_final):
    return _forward([x_prompt, x_sample], [mem_prompt, mem_sample], g_mix, w_in, sink, conv_w,
                    w_attn_br, w_conv_br, w_mix_out, g_cross, g_mem, w_xq, w_xkv, w_xo, g_moe,
                    w_router, b_router, w_gate, b_gate, w_up, b_up, w_down, b_down, g_final)
```
